```python
import math
import jax, jax.numpy as jnp
from jax import lax
import numpy as np

D_MODEL = 2048
BATCH = 2
SEQ = 4096
DEPTH = 1

MOBA_HEADS = 8
MOBA_HEAD_DIM = 128
MOBA_WIDTH = MOBA_HEADS * MOBA_HEAD_DIM
MOBA_BLOCK = 256
MOBA_TOPK = 3
MOBA_Q_CHUNK = 32
DIFF_HEADS = 8
DIFF_QK_DIM = 64
DIFF_V_DIM = 2 * DIFF_QK_DIM
DIFF_QK_WIDTH = DIFF_HEADS * 2 * DIFF_QK_DIM
DIFF_WIDTH = DIFF_HEADS * DIFF_V_DIM
DIFF_Q_BLOCK = 128
ROPE_THETA = 10000.0
FFN_HIDDEN = ((8 * D_MODEL + 3 * 256 - 1) // (3 * 256)) * 256
LN_EPS = 1e-5
RMS_EPS = 1e-5
DEEPNORM_ALPHA = (2.0 * DEPTH) ** 0.25
DEEPNORM_BETA = (8.0 * DEPTH) ** -0.25
IN_SIZES = (MOBA_WIDTH, MOBA_WIDTH, MOBA_WIDTH,
            DIFF_QK_WIDTH, DIFF_QK_WIDTH, DIFF_WIDTH,
            D_MODEL, D_MODEL)
IN_COLS = sum(IN_SIZES)
IN_SPLITS = tuple(int(s) for s in np.cumsum(IN_SIZES)[:-1])

kernel_name = 'moba_diffattn_gated_deepnorm_block'


def _rope(t, pos):
    d = t.shape[-1]
    half = d // 2
    inv_freq = ROPE_THETA ** (-jnp.arange(half, dtype=jnp.float32) * 2.0 / d)
    ang = pos.astype(jnp.float32)[:, None] * inv_freq[None, :]
    shape = (pos.shape[0],) + (1,) * (t.ndim - 3) + (half,)
    cos = jnp.cos(ang).reshape(shape)
    sin = jnp.sin(ang).reshape(shape)
    tf = t.astype(jnp.float32)
    t1, t2 = tf[..., :half], tf[..., half:]
    return jnp.concatenate([t1 * cos - t2 * sin, t2 * cos + t1 * sin], axis=-1).astype(t.dtype)


def _layer_norm(t, g, b):
    tf = t.astype(jnp.float32)
    mu = jnp.mean(tf, axis=-1, keepdims=True)
    var = jnp.mean(jnp.square(tf - mu), axis=-1, keepdims=True)
    return ((tf - mu) * lax.rsqrt(var + LN_EPS) * g.astype(jnp.float32) + b.astype(jnp.float32)).astype(t.dtype)


def _moba_attention(q, k, v):
    B, S, H, Dh = q.shape
    nb = -(-S // MOBA_BLOCK)
    s_pad = nb * MOBA_BLOCK
    pad = ((0, 0), (0, s_pad - S), (0, 0), (0, 0))
    qh = jnp.pad(q, pad).astype(jnp.float32).transpose(0, 2, 1, 3)
    kb = jnp.pad(k, pad).astype(jnp.float32).transpose(0, 2, 1, 3).reshape(B, H, nb, MOBA_BLOCK, Dh)
    vb = jnp.pad(v, pad).astype(jnp.float32).transpose(0, 2, 1, 3).reshape(B, H, nb, MOBA_BLOCK, Dh)
    k_mean = jnp.mean(kb, axis=3)
    gate = jnp.einsum('bhsd,bhnd->bhsn', qh, k_mean)
    q_blk = jnp.arange(s_pad) // MOBA_BLOCK
    past = jnp.arange(nb)[None, :] < q_blk[:, None]
    gate = jnp.where(past[None, None], gate, -jnp.inf)
    topk = min(MOBA_TOPK, nb)
    _, sel = lax.top_k(gate, topk)
    sel_valid = jnp.arange(topk)[None, :] < q_blk[:, None]
    C = MOBA_Q_CHUNK
    n_chunk = s_pad // C
    q_c = qh.reshape(B, H, n_chunk, C, Dh).transpose(2, 0, 1, 3, 4)
    sel_c = sel.reshape(B, H, n_chunk, C, topk).transpose(2, 0, 1, 3, 4)
    valid_c = sel_valid.reshape(n_chunk, C, topk)
    scale = Dh ** -0.5
    bi = jnp.arange(B)[:, None, None, None]
    hi = jnp.arange(H)[None, :, None, None]
    key_off = jnp.arange(MOBA_BLOCK)

    def chunk_fn(args):
        c, qc, selc, validc = args
        kg = kb[bi, hi, selc]
        vg = vb[bi, hi, selc]
        s_sel = jnp.einsum('bhcd,bhcjkd->bhcjk', qc, kg) * scale
        s_sel = jnp.where(validc[None, None, :, :, None], s_sel, -jnp.inf).reshape(B, H, C, topk * MOBA_BLOCK)
        blk = (c * C) // MOBA_BLOCK
        k_own = lax.dynamic_index_in_dim(kb, blk, axis=2, keepdims=False)
        v_own = lax.dynamic_index_in_dim(vb, blk, axis=2, keepdims=False)
        s_own = jnp.einsum('bhcd,bhkd->bhck', qc, k_own) * scale
        q_off = (c * C) % MOBA_BLOCK + jnp.arange(C)
        s_own = jnp.where(key_off[None, :] <= q_off[:, None], s_own, -jnp.inf)
        p = jax.nn.softmax(jnp.concatenate([s_sel, s_own], axis=-1), axis=-1)
        p_sel = p[..., :topk * MOBA_BLOCK].reshape(B, H, C, topk, MOBA_BLOCK)
        p_own = p[..., topk * MOBA_BLOCK:]
        return (jnp.einsum('bhcjk,bhcjkd->bhcd', p_sel, vg)
                + jnp.einsum('bhck,bhkd->bhcd', p_own, v_own))

    out = lax.map(chunk_fn, (jnp.arange(n_chunk), q_c, sel_c, valid_c))
    out = out.transpose(1, 0, 3, 2, 4).reshape(B, s_pad, H * Dh)
    return out[:, :S]


def _diff_attention(q, k, v, lam, subln_w, lam_init):
    B, S, H, _, dq = q.shape
    dv = v.shape[-1]
    qh = q.astype(jnp.float32).transpose(0, 2, 3, 1, 4)
    kh = k.astype(jnp.float32).transpose(0, 2, 3, 1, 4)
    vh = v.astype(jnp.float32).transpose(0, 2, 1, 3)
    nq = S // DIFF_Q_BLOCK
    q_blocks = qh.reshape(B, H, 2, nq, DIFF_Q_BLOCK, dq).transpose(3, 0, 1, 2, 4, 5)
    kpos = jnp.arange(S)
    scale = dq ** -0.5

    def block_fn(args):
        i, qb = args
        s = jnp.einsum('bhmqd,bhmkd->bhmqk', qb, kh) * scale
        qpos = i * DIFF_Q_BLOCK + jnp.arange(DIFF_Q_BLOCK)
        s = jnp.where(kpos[None, :] <= qpos[:, None], s, -jnp.inf)
        p = jax.nn.softmax(s, axis=-1)
        a = p[:, :, 0] - lam * p[:, :, 1]
        return jnp.einsum('bhqk,bhkd->bhqd', a, vh)

    o = lax.map(block_fn, (jnp.arange(nq), q_blocks))
    o = o.transpose(1, 0, 3, 2, 4).reshape(B, S, H, dv)
    o = o * lax.rsqrt(jnp.mean(jnp.square(o), axis=-1, keepdims=True) + RMS_EPS)
    o = o * subln_w.astype(jnp.float32) * (1.0 - lam_init)
    return o.reshape(B, S, H * dv)


def _mixer(h, w_in, lambda_qk, subln_w, w_branch_a, w_branch_b, w_out, lam_init):
    B, S, _ = h.shape
    pos = jnp.arange(S)
    z = h @ w_in
    qa, ka, va, qb, kb, vb, ga, gb = jnp.split(z, IN_SPLITS, axis=-1)
    qa = _rope(qa.reshape(B, S, MOBA_HEADS, MOBA_HEAD_DIM), pos)
    ka = _rope(ka.reshape(B, S, MOBA_HEADS, MOBA_HEAD_DIM), pos)
    va = va.reshape(B, S, MOBA_HEADS, MOBA_HEAD_DIM)
    y_a = _moba_attention(qa, ka, va).astype(h.dtype)
    lq = lambda_qk.astype(jnp.float32)
    lam = jnp.exp(jnp.sum(lq[0] * lq[1])) - jnp.exp(jnp.sum(lq[2] * lq[3])) + lam_init
    qb = _rope(qb.reshape(B, S, DIFF_HEADS, 2, DIFF_QK_DIM), pos)
    kb = _rope(kb.reshape(B, S, DIFF_HEADS, 2, DIFF_QK_DIM), pos)
    vb = vb.reshape(B, S, DIFF_HEADS, DIFF_V_DIM)
    y_b = _diff_attention(qb, kb, vb, lam, subln_w, lam_init).astype(h.dtype)
    m = jax.nn.sigmoid(ga) * (y_a @ w_branch_a) + jax.nn.sigmoid(gb) * (y_b @ w_branch_b)
    return m @ w_out


def _swiglu(h, w_ffn_in, w_ffn_out):
    g, u = jnp.split(h @ w_ffn_in, 2, axis=-1)
    return (jax.nn.silu(g) * u) @ w_ffn_out


def setup_inputs(seed: int = 0) -> dict:
    key = jax.random.key(seed)
    ks = jax.random.split(key, 14)
    f32 = jnp.float32
    col_scale = np.ones((IN_COLS,), np.float32)
    starts = (0,) + IN_SPLITS
    for idx in (2, 5):
        col_scale[starts[idx]:starts[idx] + IN_SIZES[idx]] = DEEPNORM_BETA
    x = jax.random.normal(ks[0], (BATCH, SEQ, D_MODEL), f32)
    w_in = jax.random.normal(ks[1], (DEPTH, D_MODEL, IN_COLS), f32) * (D_MODEL ** -0.5) * jnp.asarray(col_scale)
    lambda_qk = 0.1 * jax.random.normal(ks[2], (DEPTH, 4, DIFF_QK_DIM), f32)
    diff_subln_w = 1.0 + 0.02 * jax.random.normal(ks[3], (DEPTH, DIFF_V_DIM), f32)
    w_branch_a = jax.random.normal(ks[4], (DEPTH, MOBA_WIDTH, D_MODEL), f32) * (MOBA_WIDTH ** -0.5) * DEEPNORM_BETA
    w_branch_b = jax.random.normal(ks[5], (DEPTH, DIFF_WIDTH, D_MODEL), f32) * (DIFF_WIDTH ** -0.5) * DEEPNORM_BETA
    w_out = jax.random.normal(ks[6], (DEPTH, D_MODEL, D_MODEL), f32) * (D_MODEL ** -0.5) * DEEPNORM_BETA
    ln1_g = 1.0 + 0.02 * jax.random.normal(ks[7], (DEPTH, D_MODEL), f32)
    ln1_b = 0.02 * jax.random.normal(ks[8], (DEPTH, D_MODEL), f32)
    w_ffn_in = jax.random.normal(ks[9], (DEPTH, D_MODEL, 2 * FFN_HIDDEN), f32) * (D_MODEL ** -0.5) * DEEPNORM_BETA
    w_ffn_out = jax.random.normal(ks[10], (DEPTH, FFN_HIDDEN, D_MODEL), f32) * (FFN_HIDDEN ** -0.5) * DEEPNORM_BETA
    ln2_g = 1.0 + 0.02 * jax.random.normal(ks[11], (DEPTH, D_MODEL), f32)
    ln2_b = 0.02 * jax.random.normal(ks[12], (DEPTH, D_MODEL), f32)
    return {'x': x, 'w_in': w_in, 'lambda_qk': lambda_qk, 'diff_subln_w': diff_subln_w,
            'w_branch_a': w_branch_a, 'w_branch_b': w_branch_b, 'w_out': w_out,
            'ln1_g': ln1_g, 'ln1_b': ln1_b, 'w_ffn_in': w_ffn_in, 'w_ffn_out': w_ffn_out,
            'ln2_g': ln2_g, 'ln2_b': ln2_b}


def reference(x, w_in, lambda_qk, diff_subln_w, w_branch_a, w_branch_b, w_out,
              ln1_g, ln1_b, w_ffn_in, w_ffn_out, ln2_g, ln2_b):
    h = x
    for l in range(DEPTH):
        lam_init = 0.8 - 0.6 * math.exp(-0.3 * l)
        mix = _mixer(h, w_in[l], lambda_qk[l], diff_subln_w[l], w_branch_a[l], w_branch_b[l], w_out[l], lam_init)
        h = _layer_norm(DEEPNORM_ALPHA * h + mix, ln1_g[l], ln1_b[l])
        h = _layer_norm(DEEPNORM_ALPHA * h + _swiglu(h, w_ffn_in[l], w_ffn_out[l]), ln2_g[l], ln2_b[l])
    return h
```

```python
import functools
import math

import jax
import jax.numpy as jnp
import numpy as np
from jax import lax
from jax.experimental import pallas as pl
from jax.experimental.pallas import tpu as pltpu

D_MODEL = 2048
HEADS = 8
HEAD_DIM = 128
WIDTH = HEADS * HEAD_DIM
MOBA_BLOCK = 256
MOBA_TOPK = 3
DIFF_QK_DIM = 64
ROPE_THETA = 10000.0
FFN_HIDDEN = 5632
LN_EPS = 1e-5
RMS_EPS = 1e-5

LANES = 128
SUBLANES = 8
VMEM_LIMIT = 56 * 1024 * 1024
MASKED = -1e30

BF16 = jnp.bfloat16
F32 = jnp.float32

_NT = (((1,), (1,)), ((), ()))


def _params(sem):
    return pltpu.CompilerParams(dimension_semantics=sem, vmem_limit_bytes=VMEM_LIMIT)


def _rope_tables(seq):
    pos = jnp.arange(seq, dtype=F32)[:, None]

    def half_tables(d):
        half = d // 2
        inv = ROPE_THETA ** (-jnp.arange(half, dtype=F32) * 2.0 / d)
        ang = pos * inv[None, :]
        return jnp.cos(ang), jnp.sin(ang)

    ca, sa = half_tables(HEAD_DIM)
    cos_a = jnp.concatenate([ca, ca], axis=1)
    sin_a = jnp.concatenate([-sa, sa], axis=1)
    cb, sb = half_tables(DIFF_QK_DIM)
    zero = jnp.zeros_like(sb)
    cos_b = jnp.concatenate([cb, cb, cb, cb], axis=1)
    sin_b_lo = jnp.concatenate([-sb, zero, -sb, zero], axis=1)
    sin_b_hi = jnp.concatenate([zero, sb, zero, sb], axis=1)
    return cos_a, sin_a, cos_b, sin_b_lo, sin_b_hi


_IN_TM = 512
_IN_COLS = 6 * HEAD_DIM + 2 * 2 * HEAD_DIM


def _in_proj_kernel(x_ref, wqa, wka, wva, wqb, wkb, wvb, wga, wgb,
                    cos_a, sin_a, cos_b, sin_b_lo, sin_b_hi,
                    qa_ref, ka_ref, ksum_ref, va_ref, qb_ref, kb_ref, vb_ref, sga_ref, sgb_ref,
                    w_sc):
    @pl.when(pl.program_id(1) == 0)
    def _():
        off = 0
        for w in (wqa, wka, wva, wqb, wkb, wvb, wga, wgb):
            n = w.shape[1]
            w_sc[:, off:off + n] = w[...].astype(BF16)
            off += n

    z = jnp.dot(x_ref[...].astype(BF16), w_sc[...], preferred_element_type=F32)
    d = HEAD_DIM

    def rope_a(t):
        return t * cos_a[...] + pltpu.roll(t, d // 2, 1) * sin_a[...]

    def rope_b(t):
        return (t * cos_b[...] + pltpu.roll(t, d - DIFF_QK_DIM // 2, 1) * sin_b_lo[...]
                + pltpu.roll(t, DIFF_QK_DIM // 2, 1) * sin_b_hi[...])

    qa_ref[...] = rope_a(z[:, 0:d])
    ka = rope_a(z[:, d:2 * d])
    ka_ref[...] = ka.astype(BF16)
    for blk in range(_IN_TM // MOBA_BLOCK):
        part = ka[blk * MOBA_BLOCK:(blk + 1) * MOBA_BLOCK].reshape(MOBA_BLOCK // SUBLANES, SUBLANES, d)
        ksum_ref[blk * SUBLANES:(blk + 1) * SUBLANES, :] = jnp.sum(part, axis=0)
    va_ref[...] = z[:, 2 * d:3 * d].astype(BF16)
    qb_ref[...] = (rope_b(z[:, 3 * d:4 * d]) * (DIFF_QK_DIM ** -0.5)).astype(BF16)
    kb_ref[...] = rope_b(z[:, 4 * d:5 * d]).astype(BF16)
    vb_ref[...] = z[:, 5 * d:6 * d].astype(BF16)
    sga_ref[...] = jax.nn.sigmoid(z[:, 6 * d:8 * d]).astype(BF16)
    sgb_ref[...] = jax.nn.sigmoid(z[:, 8 * d:10 * d]).astype(BF16)


def _in_proj(x2, w_in, tables, seq):
    t = x2.shape[0]
    tm = _IN_TM
    n_seq_tiles = seq // tm
    grid = (HEADS, t // tm)
    d = HEAD_DIM

    def wspec(group):
        return pl.BlockSpec((D_MODEL, d), lambda h, i, g=group: (0, g * HEADS + h))

    def gspec(base):
        return pl.BlockSpec((D_MODEL, 2 * d), lambda h, i, b=base: (0, b + h))

    tspec = pl.BlockSpec((tm, d), lambda h, i: (i % n_seq_tiles, 0))
    ospec = pl.BlockSpec((tm, d), lambda h, i: (i, h))
    gate_ospec = pl.BlockSpec((tm, 2 * d), lambda h, i: (i, h))
    ksum_rows = tm // MOBA_BLOCK * SUBLANES
    out_shape = (
        jax.ShapeDtypeStruct((t, WIDTH), F32),
        jax.ShapeDtypeStruct((t, WIDTH), BF16),
        jax.ShapeDtypeStruct((t // MOBA_BLOCK * SUBLANES, WIDTH), F32),
        jax.ShapeDtypeStruct((t, WIDTH), BF16),
        jax.ShapeDtypeStruct((t, WIDTH), BF16),
        jax.ShapeDtypeStruct((t, WIDTH), BF16),
        jax.ShapeDtypeStruct((t, WIDTH), BF16),
        jax.ShapeDtypeStruct((t, D_MODEL), BF16),
        jax.ShapeDtypeStruct((t, D_MODEL), BF16),
    )
    out_specs = (ospec, ospec, pl.BlockSpec((ksum_rows, d), lambda h, i: (i, h)),
                 ospec, ospec, ospec, ospec, gate_ospec, gate_ospec)
    gate_a_base = 6 * WIDTH // (2 * d)
    gate_b_base = gate_a_base + D_MODEL // (2 * d)
    in_specs = [pl.BlockSpec((tm, D_MODEL), lambda h, i: (i, 0))]
    in_specs += [wspec(g) for g in range(6)]
    in_specs += [gspec(gate_a_base), gspec(gate_b_base)]
    in_specs += [tspec] * 5
    return pl.pallas_call(
        _in_proj_kernel,
        grid=grid,
        in_specs=in_specs,
        out_specs=out_specs,
        out_shape=out_shape,
        scratch_shapes=[pltpu.VMEM((D_MODEL, _IN_COLS), BF16)],
        compiler_params=_params(("arbitrary", "arbitrary")),
        name="in_proj",
    )(x2, *([w_in] * 8), *tables)


def _softmax_step(s, v_blk, m_sc, l_sc, acc_sc):
    m_prev = m_sc[...]
    m_new = jnp.maximum(m_prev, jnp.max(s, axis=1, keepdims=True))
    alpha = jnp.exp(m_prev - m_new)
    p = jnp.exp(s - m_new)
    l_sc[...] = alpha * l_sc[...] + jnp.sum(p, axis=1, keepdims=True)
    acc_sc[...] = alpha * acc_sc[...] + jnp.dot(p.astype(BF16), v_blk, preferred_element_type=F32)
    m_sc[...] = m_new


def _causal_mask(rows, cols, period):
    r = lax.broadcasted_iota(jnp.int32, (rows, cols), 0)
    c = lax.broadcasted_iota(jnp.int32, (rows, cols), 1)
    if rows != period:
        r = r & (period - 1)
    return c <= r


def _split_bf16(a):
    hi = a.astype(BF16)
    lo = (a - hi.astype(F32)).astype(BF16)
    return hi, lo


def _moba_kernel(q_ref, k_ref, v_ref, ksum_ref, o_ref, kaug_sc, m_sc, l_sc, acc_sc, *, n_blocks):
    i = pl.program_id(2)
    blk = MOBA_BLOCK
    d = HEAD_DIM
    seq = n_blocks * blk

    @pl.when(i == 0)
    def _():
        kaug_sc[:, 0:d] = k_ref[...]
        row_blk = jnp.right_shift(lax.broadcasted_iota(jnp.int32, (seq, LANES), 0), blk.bit_length() - 1)
        lane = lax.broadcasted_iota(jnp.int32, (seq, LANES), 1)
        kaug_sc[:, d:2 * d] = jnp.where(row_blk == lane, 1.0, 0.0).astype(BF16)

    q = q_ref[...]
    kmean = jnp.sum(ksum_ref[...].reshape(n_blocks, SUBLANES, d), axis=1) * (1.0 / blk)

    q_hi, q_lo = _split_bf16(q)
    m_hi, m_lo = _split_bf16(kmean)
    gate = (lax.dot_general(m_hi, q_hi, _NT, preferred_element_type=F32)
            + lax.dot_general(m_lo, q_hi, _NT, preferred_element_type=F32)
            + lax.dot_general(m_hi, q_lo, _NT, preferred_element_type=F32))

    row = lax.broadcasted_iota(jnp.int32, (n_blocks, blk), 0)
    past = row < i
    keep = row == i
    for n in range(n_blocks):
        g_n = gate[n:n + 1, :]
        ahead = (gate > g_n) | ((gate == g_n) & (row < n))
        rank_n = jnp.sum(jnp.where(ahead & past, 1.0, 0.0), axis=0, keepdims=True)
        keep = keep | ((row == n) & past & (rank_n < MOBA_TOPK))
    bias_t = jnp.where(keep, 0.0, MASKED)
    bias_t = jnp.concatenate([bias_t, jnp.zeros((LANES - n_blocks, blk), F32)], axis=0)
    bias = bias_t.T
    q_aug = jnp.concatenate([(q * (d ** -0.5)).astype(BF16), bias.astype(BF16)], axis=1)

    m_sc[...] = jnp.full(m_sc.shape, -jnp.inf, F32)
    l_sc[...] = jnp.zeros(l_sc.shape, F32)
    acc_sc[...] = jnp.zeros(acc_sc.shape, F32)

    def scores(n):
        start = pl.multiple_of(n * blk, blk)
        s = lax.dot_general(q_aug, kaug_sc[pl.ds(start, blk), :], _NT, preferred_element_type=F32)
        return s, v_ref[pl.ds(start, blk), :]

    def body(n, carry):
        s, v_blk = scores(n)
        _softmax_step(s, v_blk, m_sc, l_sc, acc_sc)
        return carry

    lax.fori_loop(0, i, body, 0)
    s, v_blk = scores(i)
    s = jnp.where(_causal_mask(blk, blk, blk), s, MASKED)
    _softmax_step(s, v_blk, m_sc, l_sc, acc_sc)
    o_ref[...] = (acc_sc[...] / l_sc[...]).astype(BF16)


def _moba(qa, ka, va, ksum, batch, seq):
    n_blocks = seq // MOBA_BLOCK
    d = HEAD_DIM
    grid = (batch, HEADS, n_blocks)
    kv_spec = pl.BlockSpec((seq, d), lambda b, h, i: (b, h))
    q_spec = pl.BlockSpec((MOBA_BLOCK, d), lambda b, h, i: (b * n_blocks + i, h))
    return pl.pallas_call(
        functools.partial(_moba_kernel, n_blocks=n_blocks),
        grid=grid,
        in_specs=[q_spec, kv_spec, kv_spec,
                  pl.BlockSpec((n_blocks * SUBLANES, d), lambda b, h, i: (b, h))],
        out_specs=q_spec,
        out_shape=jax.ShapeDtypeStruct(qa.shape, BF16),
        scratch_shapes=[pltpu.VMEM((seq, 2 * d), BF16),
                        pltpu.VMEM((MOBA_BLOCK, 1), F32),
                        pltpu.VMEM((MOBA_BLOCK, 1), F32),
                        pltpu.VMEM((MOBA_BLOCK, d), F32)],
        compiler_params=_params(("arbitrary", "arbitrary", "arbitrary")),
        name="moba",
    )(qa, ka, va, ksum)


_DIFF_TQ = 256


def _diff_kernel(q_ref, k_ref, v_ref, lam_ref, subln_ref, o_ref, m_sc, l_sc, acc_sc, *, lam_init):
    i = pl.program_id(2)
    tq = _DIFF_TQ
    d = HEAD_DIM

    q = q_ref[...]
    lane = lax.broadcasted_iota(jnp.int32, (tq, d), 1)
    zero = jnp.zeros_like(q)
    q2 = jnp.concatenate([jnp.where(lane < DIFF_QK_DIM, q, zero),
                          jnp.where(lane >= DIFF_QK_DIM, q, zero)], axis=0)

    m_sc[...] = jnp.full(m_sc.shape, -jnp.inf, F32)
    l_sc[...] = jnp.zeros(l_sc.shape, F32)
    acc_sc[...] = jnp.zeros(acc_sc.shape, F32)

    def scores(n):
        start = pl.multiple_of(n * tq, tq)
        s = lax.dot_general(q2, k_ref[pl.ds(start, tq), :], _NT, preferred_element_type=F32)
        return s, v_ref[pl.ds(start, tq), :]

    def body(n, carry):
        s, v_blk = scores(n)
        _softmax_step(s, v_blk, m_sc, l_sc, acc_sc)
        return carry

    lax.fori_loop(0, i, body, 0)
    s, v_blk = scores(i)
    s = jnp.where(_causal_mask(2 * tq, tq, tq), s, MASKED)
    _softmax_step(s, v_blk, m_sc, l_sc, acc_sc)

    lq = lam_ref[...]
    lam = (jnp.exp(jnp.sum(lq[0:1, :] * lq[1:2, :], axis=1, keepdims=True))
           - jnp.exp(jnp.sum(lq[2:3, :] * lq[3:4, :], axis=1, keepdims=True)) + lam_init)
    o = acc_sc[...] / l_sc[...]
    o = o[0:tq] - lam * o[tq:2 * tq]
    o = o * lax.rsqrt(jnp.mean(o * o, axis=1, keepdims=True) + RMS_EPS)
    o_ref[...] = (o * subln_ref[...] * (1.0 - lam_init)).astype(BF16)


def _diff(qb, kb, vb, lambda_qk, subln_w, batch, seq, lam_init):
    d = HEAD_DIM
    nq = seq // _DIFF_TQ
    grid = (batch, HEADS, nq)
    kv_spec = pl.BlockSpec((seq, d), lambda b, h, i: (b, h))
    q_spec = pl.BlockSpec((_DIFF_TQ, d), lambda b, h, i: (b * nq + i, h))
    return pl.pallas_call(
        functools.partial(_diff_kernel, lam_init=lam_init),
        grid=grid,
        in_specs=[q_spec, kv_spec, kv_spec,
                  pl.BlockSpec(lambda_qk.shape, lambda b, h, i: (0, 0)),
                  pl.BlockSpec((1, d), lambda b, h, i: (0, 0))],
        out_specs=q_spec,
        out_shape=jax.ShapeDtypeStruct(qb.shape, BF16),
        scratch_shapes=[pltpu.VMEM((2 * _DIFF_TQ, 1), F32),
                        pltpu.VMEM((2 * _DIFF_TQ, 1), F32),
                        pltpu.VMEM((2 * _DIFF_TQ, d), F32)],
        compiler_params=_params(("arbitrary", "arbitrary", "arbitrary")),
        name="diffattn",
    )(qb, kb, vb, lambda_qk, subln_w.reshape(1, d))


_ROW_TM = 512
_FFN_TH = 512


def _merge_kernel(ya_ref, yb_ref, sga_ref, sgb_ref, wa_ref, wb_ref, o_ref):
    a = jnp.dot(ya_ref[...], wa_ref[...], preferred_element_type=F32)
    b = jnp.dot(yb_ref[...], wb_ref[...], preferred_element_type=F32)
    o_ref[...] = (sga_ref[...].astype(F32) * a + sgb_ref[...].astype(F32) * b).astype(BF16)


def _merge(ya, yb, sga, sgb, wa, wb):
    t = ya.shape[0]
    tm = _ROW_TM
    row = lambda w: pl.BlockSpec((tm, w), lambda i: (i, 0))
    full = lambda a: pl.BlockSpec(a.shape, lambda i: (0, 0))
    return pl.pallas_call(
        _merge_kernel,
        grid=(t // tm,),
        in_specs=[row(WIDTH), row(WIDTH), row(D_MODEL), row(D_MODEL), full(wa), full(wb)],
        out_specs=row(D_MODEL),
        out_shape=jax.ShapeDtypeStruct((t, D_MODEL), BF16),
        compiler_params=_params(("parallel",)),
        name="merge",
    )(ya, yb, sga, sgb, wa, wb)


def _layer_norm(y, g, b):
    mu = jnp.mean(y, axis=1, keepdims=True)
    c = y - mu
    var = jnp.mean(c * c, axis=1, keepdims=True)
    return c * lax.rsqrt(var + LN_EPS) * g + b


def _out_ln_kernel(m_ref, w_ref, x_ref, g_ref, b_ref, h_ref, hb_ref, *, alpha):
    y = alpha * x_ref[...] + jnp.dot(m_ref[...], w_ref[...], preferred_element_type=F32)
    h = _layer_norm(y, g_ref[...], b_ref[...])
    h_ref[...] = h
    hb_ref[...] = h.astype(BF16)


def _out_ln(m, w_out, x2, g, b, alpha):
    t = m.shape[0]
    tm = _ROW_TM
    row = pl.BlockSpec((tm, D_MODEL), lambda i: (i, 0))
    vec = pl.BlockSpec((1, D_MODEL), lambda i: (0, 0))
    return pl.pallas_call(
        functools.partial(_out_ln_kernel, alpha=alpha),
        grid=(t // tm,),
        in_specs=[row, pl.BlockSpec(w_out.shape, lambda i: (0, 0)), row, vec, vec],
        out_specs=(row, row),
        out_shape=(jax.ShapeDtypeStruct((t, D_MODEL), F32), jax.ShapeDtypeStruct((t, D_MODEL), BF16)),
        compiler_params=_params(("parallel",)),
        name="out_ln",
    )(m, w_out, x2, g.reshape(1, D_MODEL), b.reshape(1, D_MODEL))


def _ffn_kernel(hb_ref, wg_ref, wu_ref, wo_ref, h_ref, g_ref, b_ref, o_ref, acc_sc, *, alpha):
    j = pl.program_id(1)

    @pl.when(j == 0)
    def _():
        acc_sc[...] = jnp.zeros(acc_sc.shape, F32)

    hb = hb_ref[...]
    gate = jnp.dot(hb, wg_ref[...], preferred_element_type=F32)
    up = jnp.dot(hb, wu_ref[...], preferred_element_type=F32)
    act = (gate * jax.nn.sigmoid(gate) * up).astype(BF16)
    acc_sc[...] += jnp.dot(act, wo_ref[...], preferred_element_type=F32)

    @pl.when(j == pl.num_programs(1) - 1)
    def _():
        y = alpha * h_ref[...] + acc_sc[...]
        o_ref[...] = _layer_norm(y, g_ref[...], b_ref[...])


def _ffn_ln(hb, h, w_ffn_in, w_ffn_out, g, b, alpha):
    t = h.shape[0]
    tm, th = _ROW_TM, _FFN_TH
    nj = FFN_HIDDEN // th
    row = pl.BlockSpec((tm, D_MODEL), lambda i, j: (i, 0))
    vec = pl.BlockSpec((1, D_MODEL), lambda i, j: (0, 0))
    return pl.pallas_call(
        functools.partial(_ffn_kernel, alpha=alpha),
        grid=(t // tm, nj),
        in_specs=[row,
                  pl.BlockSpec((D_MODEL, th), lambda i, j: (0, j)),
                  pl.BlockSpec((D_MODEL, th), lambda i, j: (0, nj + j)),
                  pl.BlockSpec((th, D_MODEL), lambda i, j: (j, 0)),
                  row, vec, vec],
        out_specs=row,
        out_shape=jax.ShapeDtypeStruct((t, D_MODEL), F32),
        scratch_shapes=[pltpu.VMEM((tm, D_MODEL), F32)],
        compiler_params=_params(("parallel", "arbitrary")),
        name="ffn_ln",
    )(hb, w_ffn_in, w_ffn_in, w_ffn_out, h, g.reshape(1, D_MODEL), b.reshape(1, D_MODEL))


def kernel(x, w_in, lambda_qk, diff_subln_w, w_branch_a, w_branch_b, w_out,
           ln1_g, ln1_b, w_ffn_in, w_ffn_out, ln2_g, ln2_b):
    batch, seq, _ = x.shape
    depth = w_in.shape[0]
    alpha = (2.0 * depth) ** 0.25
    tables = _rope_tables(seq)
    h = x.reshape(batch * seq, D_MODEL)
    for l in range(depth):
        lam_init = 0.8 - 0.6 * math.exp(-0.3 * l)
        qa, ka, ksum, va, qb, kb, vb, sga, sgb = _in_proj(h, w_in[l], tables, seq)
        ya = _moba(qa, ka, va, ksum, batch, seq)
        yb = _diff(qb, kb, vb, lambda_qk[l], diff_subln_w[l], batch, seq, lam_init)
        m = _merge(ya, yb, sga, sgb, w_branch_a[l].astype(BF16), w_branch_b[l].astype(BF16))
        h, hb = _out_ln(m, w_out[l].astype(BF16), h, ln1_g[l], ln1_b[l], alpha)
        h = _ffn_ln(hb, h, w_ffn_in[l].astype(BF16), w_ffn_out[l].astype(BF16), ln2_g[l], ln2_b[l], alpha)
    return h.reshape(batch, seq, D_MODEL)
```

```python
import functools
import math

import jax
import jax.numpy as jnp
from jax import lax
from jax.experimental import pallas as pl
from jax.experimental.pallas import tpu as pltpu

D_MODEL = 2048
HEADS = 8
HEAD_DIM = 128
WIDTH = HEADS * HEAD_DIM
MOBA_BLOCK = 256
MOBA_TOPK = 3
DIFF_QK_DIM = 64
ROPE_THETA = 10000.0
FFN_HIDDEN = 5632
LN_EPS = 1e-5
RMS_EPS = 1e-5

LANES = 128
SUBLANES = 8
VMEM_LIMIT = 56 * 1024 * 1024
MASKED = -1e30
LOG2E = math.log2(math.e)
KV_BLOCK = 256

BF16 = jnp.bfloat16
F32 = jnp.float32


def _params(sem):
    return pltpu.CompilerParams(dimension_semantics=sem, vmem_limit_bytes=VMEM_LIMIT)


def _rope_tables(seq):
    pos = jnp.arange(seq, dtype=F32)[:, None]

    def half_tables(d):
        half = d // 2
        inv = ROPE_THETA ** (-jnp.arange(half, dtype=F32) * 2.0 / d)
        ang = pos * inv[None, :]
        return jnp.cos(ang), jnp.sin(ang)

    ca, sa = half_tables(HEAD_DIM)
    cos_a = jnp.concatenate([ca, ca], axis=1)
    sin_a = jnp.concatenate([-sa, sa], axis=1)
    cb, sb = half_tables(DIFF_QK_DIM)
    zero = jnp.zeros_like(sb)
    cos_b = jnp.concatenate([cb, cb, cb, cb], axis=1)
    sin_b_lo = jnp.concatenate([-sb, zero, -sb, zero], axis=1)
    sin_b_hi = jnp.concatenate([zero, sb, zero, sb], axis=1)
    return cos_a, sin_a, cos_b, sin_b_lo, sin_b_hi


_IN_TM = 512
_IN_COLS = 6 * HEAD_DIM + 2 * 2 * HEAD_DIM


def _in_proj_kernel(x_ref, wqa, wka, wva, wqb, wkb, wvb, wga, wgb,
                    cos_a, sin_a, cos_b, sin_b_lo, sin_b_hi,
                    qa_ref, ka_ref, ksum_ref, va_ref, qb_ref, kb_ref, vb_ref, sga_ref, sgb_ref,
                    w_sc):
    @pl.when(pl.program_id(1) == 0)
    def _():
        off = 0
        for w in (wqa, wka, wva, wqb, wkb, wvb, wga, wgb):
            n = w.shape[1]
            w_sc[:, off:off + n] = w[...].astype(BF16)
            off += n

    z = jnp.dot(x_ref[...].astype(BF16), w_sc[...], preferred_element_type=F32)
    d = HEAD_DIM
    blk = KV_BLOCK

    def rope_a(t):
        return t * cos_a[...] + pltpu.roll(t, d // 2, 1) * sin_a[...]

    def rope_b(t):
        return (t * cos_b[...] + pltpu.roll(t, d - DIFF_QK_DIM // 2, 1) * sin_b_lo[...]
                + pltpu.roll(t, DIFF_QK_DIM // 2, 1) * sin_b_hi[...])

    def store_transposed(ref, t, dtype):
        for j in range(_IN_TM // blk):
            ref[0, j] = t[j * blk:(j + 1) * blk, :].T.astype(dtype)

    store_transposed(qa_ref, rope_a(z[:, 0:d]), F32)
    ka = rope_a(z[:, d:2 * d])
    ka_ref[...] = ka.astype(BF16)
    for j in range(_IN_TM // MOBA_BLOCK):
        part = ka[j * MOBA_BLOCK:(j + 1) * MOBA_BLOCK].reshape(MOBA_BLOCK // SUBLANES, SUBLANES, d)
        ksum_ref[j * SUBLANES:(j + 1) * SUBLANES, :] = jnp.sum(part, axis=0)
    store_transposed(va_ref, z[:, 2 * d:3 * d], BF16)
    store_transposed(qb_ref, rope_b(z[:, 3 * d:4 * d]) * (DIFF_QK_DIM ** -0.5 * LOG2E), BF16)
    kb_ref[...] = rope_b(z[:, 4 * d:5 * d]).astype(BF16)
    store_transposed(vb_ref, z[:, 5 * d:6 * d], BF16)
    sga_ref[...] = jax.nn.sigmoid(z[:, 6 * d:8 * d]).astype(BF16)
    sgb_ref[...] = jax.nn.sigmoid(z[:, 8 * d:10 * d]).astype(BF16)


def _in_proj(x2, w_in, tables, seq):
    t = x2.shape[0]
    tm = _IN_TM
    n_seq_tiles = seq // tm
    grid = (HEADS, t // tm)
    d = HEAD_DIM

    def wspec(group):
        return pl.BlockSpec((D_MODEL, d), lambda h, i, g=group: (0, g * HEADS + h))

    def gspec(base):
        return pl.BlockSpec((D_MODEL, 2 * d), lambda h, i, b=base: (0, b + h))

    tspec = pl.BlockSpec((tm, d), lambda h, i: (i % n_seq_tiles, 0))
    ospec = pl.BlockSpec((tm, d), lambda h, i: (i, h))
    tr_spec = pl.BlockSpec((1, tm // KV_BLOCK, d, KV_BLOCK), lambda h, i: (h, i, 0, 0))
    gate_ospec = pl.BlockSpec((tm, 2 * d), lambda h, i: (i, h))
    ksum_rows = tm // MOBA_BLOCK * SUBLANES
    tr_shape = (HEADS, t // KV_BLOCK, d, KV_BLOCK)
    out_shape = (
        jax.ShapeDtypeStruct(tr_shape, F32),
        jax.ShapeDtypeStruct((t, WIDTH), BF16),
        jax.ShapeDtypeStruct((t // MOBA_BLOCK * SUBLANES, WIDTH), F32),
        jax.ShapeDtypeStruct(tr_shape, BF16),
        jax.ShapeDtypeStruct(tr_shape, BF16),
        jax.ShapeDtypeStruct((t, WIDTH), BF16),
        jax.ShapeDtypeStruct(tr_shape, BF16),
        jax.ShapeDtypeStruct((t, D_MODEL), BF16),
        jax.ShapeDtypeStruct((t, D_MODEL), BF16),
    )
    out_specs = (tr_spec, ospec, pl.BlockSpec((ksum_rows, d), lambda h, i: (i, h)),
                 tr_spec, tr_spec, ospec, tr_spec, gate_ospec, gate_ospec)
    gate_a_base = 6 * WIDTH // (2 * d)
    gate_b_base = gate_a_base + D_MODEL // (2 * d)
    in_specs = [pl.BlockSpec((tm, D_MODEL), lambda h, i: (i, 0))]
    in_specs += [wspec(g) for g in range(6)]
    in_specs += [gspec(gate_a_base), gspec(gate_b_base)]
    in_specs += [tspec] * 5
    return pl.pallas_call(
        _in_proj_kernel,
        grid=grid,
        in_specs=in_specs,
        out_specs=out_specs,
        out_shape=out_shape,
        scratch_shapes=[pltpu.VMEM((D_MODEL, _IN_COLS), BF16)],
        compiler_params=_params(("arbitrary", "arbitrary")),
        name="in_proj",
    )(x2, *([w_in] * 8), *tables)


def _block_step(k_blk, vt_blk, rhs_ref, acc_ref, groups, ms, ls, masks):
    scores = [jnp.dot(k_blk, rhs_ref[c], preferred_element_type=F32) for c in groups]
    for c, s, mask in zip(groups, scores, masks):
        if mask is not None:
            s = jnp.where(mask, s, MASKED)
        m_new = jnp.maximum(ms[c], jnp.max(s, axis=0, keepdims=True))
        alpha = jnp.exp2(ms[c] - m_new)
        p = jnp.exp2(s - m_new)
        ls[c] = alpha * ls[c] + jnp.sum(p, axis=0, keepdims=True)
        acc_ref[c] = alpha * acc_ref[c] + jnp.dot(vt_blk, p.astype(BF16), preferred_element_type=F32)
        ms[c] = m_new


def _flash_columns(i, n_sub, n_qgroups, load_kv, rhs_ref, acc_ref):
    n_groups = n_sub * n_qgroups
    blk = KV_BLOCK
    init = (tuple(jnp.full((1, blk), -jnp.inf, F32) for _ in range(n_groups))
            + tuple(jnp.zeros((1, blk), F32) for _ in range(n_groups)))
    for c in range(n_groups):
        acc_ref[c] = jnp.zeros(acc_ref.shape[1:], F32)

    def body(n, carry):
        k_blk, vt_blk = load_kv(n)
        ms, ls = list(carry[:n_groups]), list(carry[n_groups:])
        _block_step(k_blk, vt_blk, rhs_ref, acc_ref, list(range(n_groups)), ms, ls, [None] * n_groups)
        return tuple(ms) + tuple(ls)

    carry = lax.fori_loop(0, n_qgroups * i, body, init)
    ms, ls = list(carry[:n_groups]), list(carry[n_groups:])
    row = lax.broadcasted_iota(jnp.int32, (blk, blk), 0)
    col = lax.broadcasted_iota(jnp.int32, (blk, blk), 1)
    causal = row <= col
    for t in range(n_qgroups):
        k_blk, vt_blk = load_kv(n_qgroups * i + t)
        groups = [sub * n_qgroups + qg for sub in range(n_sub) for qg in range(t, n_qgroups)]
        masks = [causal if c % n_qgroups == t else None for c in groups]
        _block_step(k_blk, vt_blk, rhs_ref, acc_ref, groups, ms, ls, masks)
    return ms, ls


_MOBA_QGROUPS = 4


def _split_bf16(a):
    hi = a.astype(BF16)
    lo = (a - hi.astype(F32)).astype(BF16)
    return hi, lo


def _moba_kernel(q_ref, k_ref, vt_ref, ksum_ref, o_ref, kaug_sc, rhs_sc, acc_sc, *, n_blocks):
    i = pl.program_id(2)
    blk = MOBA_BLOCK
    d = HEAD_DIM
    seq = n_blocks * blk

    @pl.when(i == 0)
    def _():
        kaug_sc[:, 0:d] = k_ref[...]
        row_blk = jnp.right_shift(lax.broadcasted_iota(jnp.int32, (seq, LANES), 0), blk.bit_length() - 1)
        lane = lax.broadcasted_iota(jnp.int32, (seq, LANES), 1)
        kaug_sc[:, d:2 * d] = jnp.where(row_blk == lane, 1.0, 0.0).astype(BF16)

    kmean = jnp.sum(ksum_ref[...].reshape(n_blocks, SUBLANES, d), axis=1) * (1.0 / blk)
    m_hi, m_lo = _split_bf16(kmean)
    row = lax.broadcasted_iota(jnp.int32, (n_blocks, blk), 0).astype(F32)

    for g in range(_MOBA_QGROUPS):
        q_t = q_ref[0, g]
        q_blk = (_MOBA_QGROUPS * i + g).astype(F32)
        q_hi, q_lo = _split_bf16(q_t)
        gate = (jnp.dot(m_hi, q_hi, preferred_element_type=F32)
                + jnp.dot(m_lo, q_hi, preferred_element_type=F32)
                + jnp.dot(m_hi, q_lo, preferred_element_type=F32))
        avail = row < q_blk
        keep = row == q_blk
        val = jnp.where(avail, gate, -jnp.inf)
        for _ in range(MOBA_TOPK):
            best = jnp.max(val, axis=0, keepdims=True)
            cand = (val == best) & avail
            first = jnp.min(jnp.where(cand, row, float(n_blocks)), axis=0, keepdims=True)
            pick = row == first
            keep = keep | pick
            avail = avail & jnp.logical_not(pick)
            val = jnp.where(pick, -jnp.inf, val)
        bias = jnp.where(keep, 0.0, MASKED)
        bias = jnp.concatenate([bias, jnp.zeros((LANES - n_blocks, blk), F32)], axis=0)
        rhs_sc[g] = jnp.concatenate([(q_t * (d ** -0.5 * LOG2E)).astype(BF16), bias.astype(BF16)], axis=0)

    def load_kv(n):
        start = pl.multiple_of(n * blk, blk)
        return kaug_sc[pl.ds(start, blk), :], vt_ref[0, n]

    _, ls = _flash_columns(i, 1, _MOBA_QGROUPS, load_kv, rhs_sc, acc_sc)
    for g in range(_MOBA_QGROUPS):
        o_ref[g * blk:(g + 1) * blk, :] = (acc_sc[g] / ls[g]).T.astype(BF16)


def _moba(qa_t, ka, va_t, ksum, batch, seq):
    n_blocks = seq // MOBA_BLOCK
    d = HEAD_DIM
    tq = _MOBA_QGROUPS * MOBA_BLOCK
    n_tiles = seq // tq
    grid = (batch, HEADS, n_tiles)
    return pl.pallas_call(
        functools.partial(_moba_kernel, n_blocks=n_blocks),
        grid=grid,
        in_specs=[pl.BlockSpec((1, _MOBA_QGROUPS, d, MOBA_BLOCK), lambda b, h, i: (h, b * n_tiles + i, 0, 0)),
                  pl.BlockSpec((seq, d), lambda b, h, i: (b, h)),
                  pl.BlockSpec((1, n_blocks, d, MOBA_BLOCK), lambda b, h, i: (h, b, 0, 0)),
                  pl.BlockSpec((n_blocks * SUBLANES, d), lambda b, h, i: (b, h))],
        out_specs=pl.BlockSpec((tq, d), lambda b, h, i: (b * n_tiles + i, h)),
        out_shape=jax.ShapeDtypeStruct(ka.shape, BF16),
        scratch_shapes=[pltpu.VMEM((seq, 2 * d), BF16),
                        pltpu.VMEM((_MOBA_QGROUPS, 2 * d, MOBA_BLOCK), BF16),
                        pltpu.VMEM((_MOBA_QGROUPS, d, MOBA_BLOCK), F32)],
        compiler_params=_params(("arbitrary", "arbitrary", "arbitrary")),
        name="moba",
    )(qa_t, ka, va_t, ksum)


_DIFF_QGROUPS = 2


def _diff_kernel(q_ref, k_ref, vt_ref, lam_ref, subln_ref, o_ref, rhs_sc, acc_sc, *, lam_init):
    i = pl.program_id(2)
    blk = KV_BLOCK
    d = HEAD_DIM
    nq = _DIFF_QGROUPS

    feat = lax.broadcasted_iota(jnp.int32, (d, blk), 0)
    for sub in range(2):
        own = (feat < DIFF_QK_DIM) if sub == 0 else (feat >= DIFF_QK_DIM)
        for g in range(nq):
            q_t = q_ref[0, g]
            rhs_sc[sub * nq + g] = jnp.where(own, q_t, jnp.zeros_like(q_t))

    def load_kv(n):
        start = pl.multiple_of(n * blk, blk)
        return k_ref[pl.ds(start, blk), :], vt_ref[0, n]

    _, ls = _flash_columns(i, 2, nq, load_kv, rhs_sc, acc_sc)

    lq = lam_ref[...]
    lam = (jnp.exp(jnp.sum(lq[0:1, :] * lq[1:2, :], axis=1, keepdims=True))
           - jnp.exp(jnp.sum(lq[2:3, :] * lq[3:4, :], axis=1, keepdims=True)) + lam_init)
    for g in range(nq):
        o = acc_sc[g] / ls[g] - lam * (acc_sc[nq + g] / ls[nq + g])
        o = o * lax.rsqrt(jnp.mean(o * o, axis=0, keepdims=True) + RMS_EPS)
        o_ref[g * blk:(g + 1) * blk, :] = (o.T * subln_ref[...] * (1.0 - lam_init)).astype(BF16)


def _diff(qb_t, kb, vb_t, lambda_qk, subln_w, batch, seq, lam_init):
    d = HEAD_DIM
    n_blocks = seq // KV_BLOCK
    tq = _DIFF_QGROUPS * KV_BLOCK
    n_tiles = seq // tq
    grid = (batch, HEADS, n_tiles)
    return pl.pallas_call(
        functools.partial(_diff_kernel, lam_init=lam_init),
        grid=grid,
        in_specs=[pl.BlockSpec((1, _DIFF_QGROUPS, d, KV_BLOCK), lambda b, h, i: (h, b * n_tiles + i, 0, 0)),
                  pl.BlockSpec((seq, d), lambda b, h, i: (b, h)),
                  pl.BlockSpec((1, n_blocks, d, KV_BLOCK), lambda b, h, i: (h, b, 0, 0)),
                  pl.BlockSpec(lambda_qk.shape, lambda b, h, i: (0, 0)),
                  pl.BlockSpec((1, d), lambda b, h, i: (0, 0))],
        out_specs=pl.BlockSpec((tq, d), lambda b, h, i: (b * n_tiles + i, h)),
        out_shape=jax.ShapeDtypeStruct(kb.shape, BF16),
        scratch_shapes=[pltpu.VMEM((2 * _DIFF_QGROUPS, d, KV_BLOCK), BF16),
                        pltpu.VMEM((2 * _DIFF_QGROUPS, d, KV_BLOCK), F32)],
        compiler_params=_params(("arbitrary", "arbitrary", "arbitrary")),
        name="diffattn",
    )(qb_t, kb, vb_t, lambda_qk, subln_w.reshape(1, d))


_ROW_TM = 512
_FFN_TH = 512


def _merge_kernel(ya_ref, yb_ref, sga_ref, sgb_ref, wa_ref, wb_ref, o_ref):
    a = jnp.dot(ya_ref[...], wa_ref[...], preferred_element_type=F32)
    b = jnp.dot(yb_ref[...], wb_ref[...], preferred_element_type=F32)
    o_ref[...] = (sga_ref[...].astype(F32) * a + sgb_ref[...].astype(F32) * b).astype(BF16)


def _merge(ya, yb, sga, sgb, wa, wb):
    t = ya.shape[0]
    tm = _ROW_TM
    row = lambda w: pl.BlockSpec((tm, w), lambda i: (i, 0))
    full = lambda a: pl.BlockSpec(a.shape, lambda i: (0, 0))
    return pl.pallas_call(
        _merge_kernel,
        grid=(t // tm,),
        in_specs=[row(WIDTH), row(WIDTH), row(D_MODEL), row(D_MODEL), full(wa), full(wb)],
        out_specs=row(D_MODEL),
        out_shape=jax.ShapeDtypeStruct((t, D_MODEL), BF16),
        compiler_params=_params(("parallel",)),
        name="merge",
    )(ya, yb, sga, sgb, wa, wb)


def _layer_norm(y, g, b):
    mu = jnp.mean(y, axis=1, keepdims=True)
    c = y - mu
    var = jnp.mean(c * c, axis=1, keepdims=True)
    return c * lax.rsqrt(var + LN_EPS) * g + b


def _out_ln_kernel(m_ref, w_ref, x_ref, g_ref, b_ref, h_ref, hb_ref, *, alpha):
    y = alpha * x_ref[...] + jnp.dot(m_ref[...], w_ref[...], preferred_element_type=F32)
    h = _layer_norm(y, g_ref[...], b_ref[...])
    h_ref[...] = h
    hb_ref[...] = h.astype(BF16)


def _out_ln(m, w_out, x2, g, b, alpha):
    t = m.shape[0]
    tm = _ROW_TM
    row = pl.BlockSpec((tm, D_MODEL), lambda i: (i, 0))
    vec = pl.BlockSpec((1, D_MODEL), lambda i: (0, 0))
    return pl.pallas_call(
        functools.partial(_out_ln_kernel, alpha=alpha),
        grid=(t // tm,),
        in_specs=[row, pl.BlockSpec(w_out.shape, lambda i: (0, 0)), row, vec, vec],
        out_specs=(row, row),
        out_shape=(jax.ShapeDtypeStruct((t, D_MODEL), F32), jax.ShapeDtypeStruct((t, D_MODEL), BF16)),
        compiler_params=_params(("parallel",)),
        name="out_ln",
    )(m, w_out, x2, g.reshape(1, D_MODEL), b.reshape(1, D_MODEL))


def _ffn_kernel(hb_ref, wg_ref, wu_ref, wo_ref, h_ref, g_ref, b_ref, o_ref, acc_sc, *, alpha):
    j = pl.program_id(1)

    @pl.when(j == 0)
    def _():
        acc_sc[...] = jnp.zeros(acc_sc.shape, F32)

    hb = hb_ref[...]
    gate = jnp.dot(hb, wg_ref[...], preferred_element_type=F32)
    up = jnp.dot(hb, wu_ref[...], preferred_element_type=F32)
    act = (gate * jax.nn.sigmoid(gate) * up).astype(BF16)
    acc_sc[...] += jnp.dot(act, wo_ref[...], preferred_element_type=F32)

    @pl.when(j == pl.num_programs(1) - 1)
    def _():
        y = alpha * h_ref[...] + acc_sc[...]
        o_ref[...] = _layer_norm(y, g_ref[...], b_ref[...])


def _ffn_ln(hb, h, w_ffn_in, w_ffn_out, g, b, alpha):
    t = h.shape[0]
    tm, th = _ROW_TM, _FFN_TH
    nj = FFN_HIDDEN // th
    row = pl.BlockSpec((tm, D_MODEL), lambda i, j: (i, 0))
    vec = pl.BlockSpec((1, D_MODEL), lambda i, j: (0, 0))
    return pl.pallas_call(
        functools.partial(_ffn_kernel, alpha=alpha),
        grid=(t // tm, nj),
        in_specs=[row,
                  pl.BlockSpec((D_MODEL, th), lambda i, j: (0, j)),
                  pl.BlockSpec((D_MODEL, th), lambda i, j: (0, nj + j)),
                  pl.BlockSpec((th, D_MODEL), lambda i, j: (j, 0)),
                  row, vec, vec],
        out_specs=row,
        out_shape=jax.ShapeDtypeStruct((t, D_MODEL), F32),
        scratch_shapes=[pltpu.VMEM((tm, D_MODEL), F32)],
        compiler_params=_params(("parallel", "arbitrary")),
        name="ffn_ln",
    )(hb, w_ffn_in, w_ffn_in, w_ffn_out, h, g.reshape(1, D_MODEL), b.reshape(1, D_MODEL))


def kernel(x, w_in, lambda_qk, diff_subln_w, w_branch_a, w_branch_b, w_out,
           ln1_g, ln1_b, w_ffn_in, w_ffn_out, ln2_g, ln2_b):
    batch, seq, _ = x.shape
    depth = w_in.shape[0]
    alpha = (2.0 * depth) ** 0.25
    tables = _rope_tables(seq)
    h = x.reshape(batch * seq, D_MODEL)
    for l in range(depth):
        lam_init = 0.8 - 0.6 * math.exp(-0.3 * l)
        qa_t, ka, ksum, va_t, qb_t, kb, vb_t, sga, sgb = _in_proj(h, w_in[l], tables, seq)
        ya = _moba(qa_t, ka, va_t, ksum, batch, seq)
        yb = _diff(qb_t, kb, vb_t, lambda_qk[l], diff_subln_w[l], batch, seq, lam_init)
        m = _merge(ya, yb, sga, sgb, w_branch_a[l].astype(BF16), w_branch_b[l].astype(BF16))
        h, hb = _out_ln(m, w_out[l].astype(BF16), h, ln1_g[l], ln1_b[l], alpha)
        h = _ffn_ln(hb, h, w_ffn_in[l].astype(BF16), w_ffn_out[l].astype(BF16), ln2_g[l], ln2_b[l], alpha)
    return h.reshape(batch, seq, D_MODEL)
```

```python
import functools
import math

import jax
import jax.numpy as jnp
from jax import lax
from jax.experimental import pallas as pl
from jax.experimental.pallas import tpu as pltpu

D_MODEL = 2048
HEADS = 8
HEAD_DIM = 128
WIDTH = HEADS * HEAD_DIM
MOBA_BLOCK = 256
MOBA_TOPK = 3
DIFF_QK_DIM = 64
ROPE_THETA = 10000.0
FFN_HIDDEN = 5632
LN_EPS = 1e-5
RMS_EPS = 1e-5

LANES = 128
SUBLANES = 8
VMEM_LIMIT = 56 * 1024 * 1024
MASKED = -1e30
LOG2E = math.log2(math.e)
KV_BLOCK = 256

BF16 = jnp.bfloat16
F32 = jnp.float32


def _params(sem):
    return pltpu.CompilerParams(dimension_semantics=sem, vmem_limit_bytes=VMEM_LIMIT)


def _rope_tables(seq):
    pos = jnp.arange(seq, dtype=F32)[:, None]

    def half_tables(d):
        half = d // 2
        inv = ROPE_THETA ** (-jnp.arange(half, dtype=F32) * 2.0 / d)
        ang = pos * inv[None, :]
        return jnp.cos(ang), jnp.sin(ang)

    ca, sa = half_tables(HEAD_DIM)
    cos_a = jnp.concatenate([ca, ca], axis=1)
    sin_a = jnp.concatenate([-sa, sa], axis=1)
    cb, sb = half_tables(DIFF_QK_DIM)
    zero = jnp.zeros_like(sb)
    cos_b = jnp.concatenate([cb, cb, cb, cb], axis=1)
    sin_b_lo = jnp.concatenate([-sb, zero, -sb, zero], axis=1)
    sin_b_hi = jnp.concatenate([zero, sb, zero, sb], axis=1)
    return cos_a, sin_a, cos_b, sin_b_lo, sin_b_hi


_IN_TM = 512
_IN_COLS = 6 * HEAD_DIM + 2 * 2 * HEAD_DIM


def _in_proj_kernel(x_ref, wqa, wka, wva, wqb, wkb, wvb, wga, wgb,
                    cos_a, sin_a, cos_b, sin_b_lo, sin_b_hi,
                    qa_ref, ka_ref, ksum_ref, va_ref, qb_ref, kb_ref, vb_ref, sga_ref, sgb_ref,
                    w_sc):
    @pl.when(pl.program_id(1) == 0)
    def _():
        off = 0
        for w in (wqa, wka, wva, wqb, wkb, wvb, wga, wgb):
            n = w.shape[1]
            w_sc[:, off:off + n] = w[...].astype(BF16)
            off += n

    z = jnp.dot(x_ref[...].astype(BF16), w_sc[...], preferred_element_type=F32)
    d = HEAD_DIM
    blk = KV_BLOCK

    def rope_a(t):
        return t * cos_a[...] + pltpu.roll(t, d // 2, 1) * sin_a[...]

    def rope_b(t):
        return (t * cos_b[...] + pltpu.roll(t, d - DIFF_QK_DIM // 2, 1) * sin_b_lo[...]
                + pltpu.roll(t, DIFF_QK_DIM // 2, 1) * sin_b_hi[...])

    def store_transposed(ref, t, dtype):
        for j in range(_IN_TM // blk):
            ref[0, j] = t[j * blk:(j + 1) * blk, :].T.astype(dtype)

    store_transposed(qa_ref, rope_a(z[:, 0:d]), F32)
    ka = rope_a(z[:, d:2 * d])
    ka_ref[...] = ka.astype(BF16)
    for j in range(_IN_TM // MOBA_BLOCK):
        part = ka[j * MOBA_BLOCK:(j + 1) * MOBA_BLOCK].reshape(MOBA_BLOCK // SUBLANES, SUBLANES, d)
        ksum_ref[j * SUBLANES:(j + 1) * SUBLANES, :] = jnp.sum(part, axis=0)
    store_transposed(va_ref, z[:, 2 * d:3 * d], BF16)
    store_transposed(qb_ref, rope_b(z[:, 3 * d:4 * d]) * (DIFF_QK_DIM ** -0.5 * LOG2E), BF16)
    kb_ref[...] = rope_b(z[:, 4 * d:5 * d]).astype(BF16)
    store_transposed(vb_ref, z[:, 5 * d:6 * d], BF16)
    sga_ref[...] = jax.nn.sigmoid(z[:, 6 * d:8 * d]).astype(BF16)
    sgb_ref[...] = jax.nn.sigmoid(z[:, 8 * d:10 * d]).astype(BF16)


def _in_proj(x2, w_in, tables, seq):
    t = x2.shape[0]
    tm = _IN_TM
    n_seq_tiles = seq // tm
    grid = (HEADS, t // tm)
    d = HEAD_DIM

    def wspec(group):
        return pl.BlockSpec((D_MODEL, d), lambda h, i, g=group: (0, g * HEADS + h))

    def gspec(base):
        return pl.BlockSpec((D_MODEL, 2 * d), lambda h, i, b=base: (0, b + h))

    tspec = pl.BlockSpec((tm, d), lambda h, i: (i % n_seq_tiles, 0))
    ospec = pl.BlockSpec((tm, d), lambda h, i: (i, h))
    tr_spec = pl.BlockSpec((1, tm // KV_BLOCK, d, KV_BLOCK), lambda h, i: (h, i, 0, 0))
    gate_ospec = pl.BlockSpec((tm, 2 * d), lambda h, i: (i, h))
    ksum_rows = tm // MOBA_BLOCK * SUBLANES
    tr_shape = (HEADS, t // KV_BLOCK, d, KV_BLOCK)
    out_shape = (
        jax.ShapeDtypeStruct(tr_shape, F32),
        jax.ShapeDtypeStruct((t, WIDTH), BF16),
        jax.ShapeDtypeStruct((t // MOBA_BLOCK * SUBLANES, WIDTH), F32),
        jax.ShapeDtypeStruct(tr_shape, BF16),
        jax.ShapeDtypeStruct(tr_shape, BF16),
        jax.ShapeDtypeStruct((t, WIDTH), BF16),
        jax.ShapeDtypeStruct(tr_shape, BF16),
        jax.ShapeDtypeStruct((t, D_MODEL), BF16),
        jax.ShapeDtypeStruct((t, D_MODEL), BF16),
    )
    out_specs = (tr_spec, ospec, pl.BlockSpec((ksum_rows, d), lambda h, i: (i, h)),
                 tr_spec, tr_spec, ospec, tr_spec, gate_ospec, gate_ospec)
    gate_a_base = 6 * WIDTH // (2 * d)
    gate_b_base = gate_a_base + D_MODEL // (2 * d)
    in_specs = [pl.BlockSpec((tm, D_MODEL), lambda h, i: (i, 0))]
    in_specs += [wspec(g) for g in range(6)]
    in_specs += [gspec(gate_a_base), gspec(gate_b_base)]
    in_specs += [tspec] * 5
    return pl.pallas_call(
        _in_proj_kernel,
        grid=grid,
        in_specs=in_specs,
        out_specs=out_specs,
        out_shape=out_shape,
        scratch_shapes=[pltpu.VMEM((D_MODEL, _IN_COLS), BF16)],
        compiler_params=_params(("arbitrary", "arbitrary")),
        name="in_proj",
    )(x2, *([w_in] * 8), *tables)


def _pipeline_stage(n, buf, cur, masks, nxt, load_k, load_vt, rhs_ref, acc_ref, s_ref, ms, ls):
    vt_blk = load_vt(n)
    k_next = load_k(n + 1) if nxt else None

    def issue_scores(c):
        s_ref[1 - buf, c] = jnp.dot(k_next, rhs_ref[c], preferred_element_type=F32)

    lead = 2
    for c in nxt[:lead]:
        issue_scores(c)
    rest = nxt[lead:]
    for j, (c, mask) in enumerate(zip(cur, masks)):
        s = s_ref[buf, c]
        if mask is not None:
            s = jnp.where(mask, s, MASKED)
        m_new = jnp.maximum(ms[c], jnp.max(s, axis=0, keepdims=True))
        alpha = jnp.exp2(ms[c] - m_new)
        p = jnp.exp2(s - m_new)
        ls[c] = alpha * ls[c] + jnp.sum(p, axis=0, keepdims=True)
        acc_ref[c] = alpha * acc_ref[c] + jnp.dot(vt_blk, p.astype(BF16), preferred_element_type=F32)
        ms[c] = m_new
        if j < len(rest):
            issue_scores(rest[j])


def _flash_columns(i, n_sub, n_qgroups, load_k, load_vt, rhs_ref, acc_ref, s_ref):
    assert n_qgroups % 2 == 0
    n_groups = n_sub * n_qgroups
    blk = KV_BLOCK
    everyone = list(range(n_groups))
    init = (tuple(jnp.full((1, blk), -jnp.inf, F32) for _ in range(n_groups))
            + tuple(jnp.zeros((1, blk), F32) for _ in range(n_groups)))
    for c in everyone:
        acc_ref[c] = jnp.zeros(acc_ref.shape[1:], F32)
    k_first = load_k(0)
    for c in everyone:
        s_ref[0, c] = jnp.dot(k_first, rhs_ref[c], preferred_element_type=F32)

    stage = functools.partial(_pipeline_stage, load_k=load_k, load_vt=load_vt,
                              rhs_ref=rhs_ref, acc_ref=acc_ref, s_ref=s_ref)

    def body(pair, carry):
        ms, ls = list(carry[:n_groups]), list(carry[n_groups:])
        for buf in range(2):
            stage(2 * pair + buf, buf, everyone, [None] * n_groups, everyone, ms=ms, ls=ls)
        return tuple(ms) + tuple(ls)

    n_past = n_qgroups * i
    carry = lax.fori_loop(0, (n_qgroups // 2) * i, body, init)
    ms, ls = list(carry[:n_groups]), list(carry[n_groups:])
    row = lax.broadcasted_iota(jnp.int32, (blk, blk), 0)
    col = lax.broadcasted_iota(jnp.int32, (blk, blk), 1)
    causal = row <= col
    for t in range(n_qgroups):
        cur = [c for c in everyone if c % n_qgroups >= t]
        nxt = [c for c in everyone if c % n_qgroups >= t + 1]
        masks = [causal if c % n_qgroups == t else None for c in cur]
        stage(n_past + t, t % 2, cur, masks, nxt, ms=ms, ls=ls)
    return ms, ls


_MOBA_QGROUPS = 4


def _split_bf16(a):
    hi = a.astype(BF16)
    lo = (a - hi.astype(F32)).astype(BF16)
    return hi, lo


def _moba_kernel(q_ref, k_ref, vt_ref, ksum_ref, o_ref, kaug_sc, rhs_sc, acc_sc, s_sc, *, n_blocks):
    i = pl.program_id(2)
    blk = MOBA_BLOCK
    d = HEAD_DIM
    seq = n_blocks * blk

    @pl.when(i == 0)
    def _():
        kaug_sc[:, 0:d] = k_ref[...]
        row_blk = jnp.right_shift(lax.broadcasted_iota(jnp.int32, (seq, LANES), 0), blk.bit_length() - 1)
        lane = lax.broadcasted_iota(jnp.int32, (seq, LANES), 1)
        kaug_sc[:, d:2 * d] = jnp.where(row_blk == lane, 1.0, 0.0).astype(BF16)

    kmean = jnp.sum(ksum_ref[...].reshape(n_blocks, SUBLANES, d), axis=1) * (1.0 / blk)
    m_hi, m_lo = _split_bf16(kmean)
    row = lax.broadcasted_iota(jnp.int32, (n_blocks, blk), 0).astype(F32)

    for g in range(_MOBA_QGROUPS):
        q_t = q_ref[0, g]
        q_blk = (_MOBA_QGROUPS * i + g).astype(F32)
        q_hi, q_lo = _split_bf16(q_t)
        gate = (jnp.dot(m_hi, q_hi, preferred_element_type=F32)
                + jnp.dot(m_lo, q_hi, preferred_element_type=F32)
                + jnp.dot(m_hi, q_lo, preferred_element_type=F32))
        avail = row < q_blk
        keep = row == q_blk
        val = jnp.where(avail, gate, -jnp.inf)
        for _ in range(MOBA_TOPK):
            best = jnp.max(val, axis=0, keepdims=True)
            cand = (val == best) & avail
            first = jnp.min(jnp.where(cand, row, float(n_blocks)), axis=0, keepdims=True)
            pick = row == first
            keep = keep | pick
            avail = avail & jnp.logical_not(pick)
            val = jnp.where(pick, -jnp.inf, val)
        bias = jnp.where(keep, 0.0, MASKED)
        bias = jnp.concatenate([bias, jnp.zeros((LANES - n_blocks, blk), F32)], axis=0)
        rhs_sc[g] = jnp.concatenate([(q_t * (d ** -0.5 * LOG2E)).astype(BF16), bias.astype(BF16)], axis=0)

    def load_k(n):
        return kaug_sc[pl.ds(pl.multiple_of(n * blk, blk), blk), :]

    def load_vt(n):
        return vt_ref[0, n]

    _, ls = _flash_columns(i, 1, _MOBA_QGROUPS, load_k, load_vt, rhs_sc, acc_sc, s_sc)
    for g in range(_MOBA_QGROUPS):
        o_ref[g * blk:(g + 1) * blk, :] = (acc_sc[g] / ls[g]).T.astype(BF16)


def _moba(qa_t, ka, va_t, ksum, batch, seq):
    n_blocks = seq // MOBA_BLOCK
    d = HEAD_DIM
    tq = _MOBA_QGROUPS * MOBA_BLOCK
    n_tiles = seq // tq
    grid = (batch, HEADS, n_tiles)
    return pl.pallas_call(
        functools.partial(_moba_kernel, n_blocks=n_blocks),
        grid=grid,
        in_specs=[pl.BlockSpec((1, _MOBA_QGROUPS, d, MOBA_BLOCK), lambda b, h, i: (h, b * n_tiles + i, 0, 0)),
                  pl.BlockSpec((seq, d), lambda b, h, i: (b, h)),
                  pl.BlockSpec((1, n_blocks, d, MOBA_BLOCK), lambda b, h, i: (h, b, 0, 0)),
                  pl.BlockSpec((n_blocks * SUBLANES, d), lambda b, h, i: (b, h))],
        out_specs=pl.BlockSpec((tq, d), lambda b, h, i: (b * n_tiles + i, h)),
        out_shape=jax.ShapeDtypeStruct(ka.shape, BF16),
        scratch_shapes=[pltpu.VMEM((seq, 2 * d), BF16),
                        pltpu.VMEM((_MOBA_QGROUPS, 2 * d, MOBA_BLOCK), BF16),
                        pltpu.VMEM((_MOBA_QGROUPS, d, MOBA_BLOCK), F32),
                        pltpu.VMEM((2, _MOBA_QGROUPS, KV_BLOCK, MOBA_BLOCK), F32)],
        compiler_params=_params(("arbitrary", "arbitrary", "arbitrary")),
        name="moba",
    )(qa_t, ka, va_t, ksum)


_DIFF_QGROUPS = 2


def _diff_kernel(q_ref, k_ref, vt_ref, lam_ref, subln_ref, o_ref, rhs_sc, acc_sc, s_sc, *, lam_init):
    i = pl.program_id(2)
    blk = KV_BLOCK
    d = HEAD_DIM
    nq = _DIFF_QGROUPS

    feat = lax.broadcasted_iota(jnp.int32, (d, blk), 0)
    for sub in range(2):
        own = (feat < DIFF_QK_DIM) if sub == 0 else (feat >= DIFF_QK_DIM)
        for g in range(nq):
            q_t = q_ref[0, g]
            rhs_sc[sub * nq + g] = jnp.where(own, q_t, jnp.zeros_like(q_t))

    def load_k(n):
        return k_ref[pl.ds(pl.multiple_of(n * blk, blk), blk), :]

    def load_vt(n):
        return vt_ref[0, n]

    _, ls = _flash_columns(i, 2, nq, load_k, load_vt, rhs_sc, acc_sc, s_sc)

    lq = lam_ref[...]
    lam = (jnp.exp(jnp.sum(lq[0:1, :] * lq[1:2, :], axis=1, keepdims=True))
           - jnp.exp(jnp.sum(lq[2:3, :] * lq[3:4, :], axis=1, keepdims=True)) + lam_init)
    for g in range(nq):
        o = acc_sc[g] / ls[g] - lam * (acc_sc[nq + g] / ls[nq + g])
        o = o * lax.rsqrt(jnp.mean(o * o, axis=0, keepdims=True) + RMS_EPS)
        o_ref[g * blk:(g + 1) * blk, :] = (o.T * subln_ref[...] * (1.0 - lam_init)).astype(BF16)


def _diff(qb_t, kb, vb_t, lambda_qk, subln_w, batch, seq, lam_init):
    d = HEAD_DIM
    n_blocks = seq // KV_BLOCK
    tq = _DIFF_QGROUPS * KV_BLOCK
    n_tiles = seq // tq
    grid = (batch, HEADS, n_tiles)
    return pl.pallas_call(
        functools.partial(_diff_kernel, lam_init=lam_init),
        grid=grid,
        in_specs=[pl.BlockSpec((1, _DIFF_QGROUPS, d, KV_BLOCK), lambda b, h, i: (h, b * n_tiles + i, 0, 0)),
                  pl.BlockSpec((seq, d), lambda b, h, i: (b, h)),
                  pl.BlockSpec((1, n_blocks, d, KV_BLOCK), lambda b, h, i: (h, b, 0, 0)),
                  pl.BlockSpec(lambda_qk.shape, lambda b, h, i: (0, 0)),
                  pl.BlockSpec((1, d), lambda b, h, i: (0, 0))],
        out_specs=pl.BlockSpec((tq, d), lambda b, h, i: (b * n_tiles + i, h)),
        out_shape=jax.ShapeDtypeStruct(kb.shape, BF16),
        scratch_shapes=[pltpu.VMEM((2 * _DIFF_QGROUPS, d, KV_BLOCK), BF16),
                        pltpu.VMEM((2 * _DIFF_QGROUPS, d, KV_BLOCK), F32),
                        pltpu.VMEM((2, 2 * _DIFF_QGROUPS, KV_BLOCK, KV_BLOCK), F32)],
        compiler_params=_params(("arbitrary", "arbitrary", "arbitrary")),
        name="diffattn",
    )(qb_t, kb, vb_t, lambda_qk, subln_w.reshape(1, d))


_ROW_TM = 512
_FFN_TH = 512


def _merge_kernel(ya_ref, yb_ref, sga_ref, sgb_ref, wa_ref, wb_ref, o_ref):
    a = jnp.dot(ya_ref[...], wa_ref[...], preferred_element_type=F32)
    b = jnp.dot(yb_ref[...], wb_ref[...], preferred_element_type=F32)
    o_ref[...] = (sga_ref[...].astype(F32) * a + sgb_ref[...].astype(F32) * b).astype(BF16)


def _merge(ya, yb, sga, sgb, wa, wb):
    t = ya.shape[0]
    tm = _ROW_TM
    row = lambda w: pl.BlockSpec((tm, w), lambda i: (i, 0))
    full = lambda a: pl.BlockSpec(a.shape, lambda i: (0, 0))
    return pl.pallas_call(
        _merge_kernel,
        grid=(t // tm,),
        in_specs=[row(WIDTH), row(WIDTH), row(D_MODEL), row(D_MODEL), full(wa), full(wb)],
        out_specs=row(D_MODEL),
        out_shape=jax.ShapeDtypeStruct((t, D_MODEL), BF16),
        compiler_params=_params(("parallel",)),
        name="merge",
    )(ya, yb, sga, sgb, wa, wb)


def _layer_norm(y, g, b):
    mu = jnp.mean(y, axis=1, keepdims=True)
    c = y - mu
    var = jnp.mean(c * c, axis=1, keepdims=True)
    return c * lax.rsqrt(var + LN_EPS) * g + b


def _out_ln_kernel(m_ref, w_ref, x_ref, g_ref, b_ref, h_ref, hb_ref, *, alpha):
    y = alpha * x_ref[...] + jnp.dot(m_ref[...], w_ref[...], preferred_element_type=F32)
    h = _layer_norm(y, g_ref[...], b_ref[...])
    h_ref[...] = h
    hb_ref[...] = h.astype(BF16)


def _out_ln(m, w_out, x2, g, b, alpha):
    t = m.shape[0]
    tm = _ROW_TM
    row = pl.BlockSpec((tm, D_MODEL), lambda i: (i, 0))
    vec = pl.BlockSpec((1, D_MODEL), lambda i: (0, 0))
    return pl.pallas_call(
        functools.partial(_out_ln_kernel, alpha=alpha),
        grid=(t // tm,),
        in_specs=[row, pl.BlockSpec(w_out.shape, lambda i: (0, 0)), row, vec, vec],
        out_specs=(row, row),
        out_shape=(jax.ShapeDtypeStruct((t, D_MODEL), F32), jax.ShapeDtypeStruct((t, D_MODEL), BF16)),
        compiler_params=_params(("parallel",)),
        name="out_ln",
    )(m, w_out, x2, g.reshape(1, D_MODEL), b.reshape(1, D_MODEL))


def _ffn_kernel(hb_ref, wg_ref, wu_ref, wo_ref, h_ref, g_ref, b_ref, o_ref, acc_sc, *, alpha):
    j = pl.program_id(1)

    @pl.when(j == 0)
    def _():
        acc_sc[...] = jnp.zeros(acc_sc.shape, F32)

    hb = hb_ref[...]
    gate = jnp.dot(hb, wg_ref[...], preferred_element_type=F32)
    up = jnp.dot(hb, wu_ref[...], preferred_element_type=F32)
    act = (gate * jax.nn.sigmoid(gate) * up).astype(BF16)
    acc_sc[...] += jnp.dot(act, wo_ref[...], preferred_element_type=F32)

    @pl.when(j == pl.num_programs(1) - 1)
    def _():
        y = alpha * h_ref[...] + acc_sc[...]
        o_ref[...] = _layer_norm(y, g_ref[...], b_ref[...])


def _ffn_ln(hb, h, w_ffn_in, w_ffn_out, g, b, alpha):
    t = h.shape[0]
    tm, th = _ROW_TM, _FFN_TH
    nj = FFN_HIDDEN // th
    row = pl.BlockSpec((tm, D_MODEL), lambda i, j: (i, 0))
    vec = pl.BlockSpec((1, D_MODEL), lambda i, j: (0, 0))
    return pl.pallas_call(
        functools.partial(_ffn_kernel, alpha=alpha),
        grid=(t // tm, nj),
        in_specs=[row,
                  pl.BlockSpec((D_MODEL, th), lambda i, j: (0, j)),
                  pl.BlockSpec((D_MODEL, th), lambda i, j: (0, nj + j)),
                  pl.BlockSpec((th, D_MODEL), lambda i, j: (j, 0)),
                  row, vec, vec],
        out_specs=row,
        out_shape=jax.ShapeDtypeStruct((t, D_MODEL), F32),
        scratch_shapes=[pltpu.VMEM((tm, D_MODEL), F32)],
        compiler_params=_params(("parallel", "arbitrary")),
        name="ffn_ln",
    )(hb, w_ffn_in, w_ffn_in, w_ffn_out, h, g.reshape(1, D_MODEL), b.reshape(1, D_MODEL))


def kernel(x, w_in, lambda_qk, diff_subln_w, w_branch_a, w_branch_b, w_out,
           ln1_g, ln1_b, w_ffn_in, w_ffn_out, ln2_g, ln2_b):
    batch, seq, _ = x.shape
    depth = w_in.shape[0]
    alpha = (2.0 * depth) ** 0.25
    tables = _rope_tables(seq)
    h = x.reshape(batch * seq, D_MODEL)
    for l in range(depth):
        lam_init = 0.8 - 0.6 * math.exp(-0.3 * l)
        qa_t, ka, ksum, va_t, qb_t, kb, vb_t, sga, sgb = _in_proj(h, w_in[l], tables, seq)
        ya = _moba(qa_t, ka, va_t, ksum, batch, seq)
        yb = _diff(qb_t, kb, vb_t, lambda_qk[l], diff_subln_w[l], batch, seq, lam_init)
        m = _merge(ya, yb, sga, sgb, w_branch_a[l].astype(BF16), w_branch_b[l].astype(BF16))
        h, hb = _out_ln(m, w_out[l].astype(BF16), h, ln1_g[l], ln1_b[l], alpha)
        h = _ffn_ln(hb, h, w_ffn_in[l].astype(BF16), w_ffn_out[l].astype(BF16), ln2_g[l], ln2_b[l], alpha)
    return h.reshape(batch, seq, D_MODEL)
```

```python
import functools
import math

import jax
import jax.numpy as jnp
from jax import lax
from jax.experimental import pallas as pl
from jax.experimental.pallas import tpu as pltpu

D_MODEL = 2048
HEADS = 8
HEAD_DIM = 128
WIDTH = HEADS * HEAD_DIM
MOBA_BLOCK = 256
MOBA_TOPK = 3
DIFF_QK_DIM = 64
ROPE_THETA = 10000.0
FFN_HIDDEN = 5632
LN_EPS = 1e-5
RMS_EPS = 1e-5

LANES = 128
SUBLANES = 8
VMEM_LIMIT = 56 * 1024 * 1024
MASKED = -1e30
LOG2E = math.log2(math.e)
KV_BLOCK = 256

BF16 = jnp.bfloat16
F32 = jnp.float32


def _params(sem):
    return pltpu.CompilerParams(dimension_semantics=sem, vmem_limit_bytes=VMEM_LIMIT)


def _rope_tables(seq):
    pos = jnp.arange(seq, dtype=F32)[:, None]

    def half_tables(d):
        half = d // 2
        inv = ROPE_THETA ** (-jnp.arange(half, dtype=F32) * 2.0 / d)
        ang = pos * inv[None, :]
        return jnp.cos(ang), jnp.sin(ang)

    ca, sa = half_tables(HEAD_DIM)
    cos_a = jnp.concatenate([ca, ca], axis=1)
    sin_a = jnp.concatenate([-sa, sa], axis=1)
    cb, sb = half_tables(DIFF_QK_DIM)
    zero = jnp.zeros_like(sb)
    cos_b = jnp.concatenate([cb, cb, cb, cb], axis=1)
    sin_b_lo = jnp.concatenate([-sb, zero, -sb, zero], axis=1)
    sin_b_hi = jnp.concatenate([zero, sb, zero, sb], axis=1)
    return cos_a, sin_a, cos_b, sin_b_lo, sin_b_hi


_IN_TM = 512
_IN_COLS = 6 * HEAD_DIM + 2 * 2 * HEAD_DIM


def _sigmoid(t):
    return 0.5 * jnp.tanh(0.5 * t) + 0.5


def _in_proj_kernel(x_ref, wqa, wka, wva, wqb, wkb, wvb, wga, wgb,
                    cos_a, sin_a, cos_b, sin_b_lo, sin_b_hi,
                    qa_ref, ka_ref, ksum_ref, va_ref, qb_ref, kb_ref, vb_ref, sga_ref, sgb_ref,
                    w_sc):
    @pl.when(pl.program_id(1) == 0)
    def _():
        off = 0
        for w in (wqa, wka, wva, wqb, wkb, wvb, wga, wgb):
            n = w.shape[1]
            w_sc[:, off:off + n] = w[...].astype(BF16)
            off += n

    z = jnp.dot(x_ref[...].astype(BF16), w_sc[...], preferred_element_type=F32)
    d = HEAD_DIM
    blk = KV_BLOCK
    zqa, zka, zva, zqb, zkb, zvb = (z[:, j * d:(j + 1) * d] for j in range(6))
    zga, zgb = z[:, 6 * d:8 * d], z[:, 8 * d:10 * d]

    def rope_a(t):
        return t * cos_a[...] + pltpu.roll(t, d // 2, 1) * sin_a[...]

    def rope_b(t):
        return (t * cos_b[...] + pltpu.roll(t, d - DIFF_QK_DIM // 2, 1) * sin_b_lo[...]
                + pltpu.roll(t, DIFF_QK_DIM // 2, 1) * sin_b_hi[...])

    def store_transposed(ref, t, dtype):
        for j in range(_IN_TM // blk):
            ref[0, j] = t[j * blk:(j + 1) * blk, :].T.astype(dtype)

    store_transposed(qa_ref, rope_a(zqa), F32)
    ka = rope_a(zka)
    ka_ref[...] = ka.astype(BF16)
    for j in range(_IN_TM // MOBA_BLOCK):
        part = ka[j * MOBA_BLOCK:(j + 1) * MOBA_BLOCK].reshape(MOBA_BLOCK // SUBLANES, SUBLANES, d)
        ksum_ref[j * SUBLANES:(j + 1) * SUBLANES, :] = jnp.sum(part, axis=0)
    store_transposed(qb_ref, rope_b(zqb) * (DIFF_QK_DIM ** -0.5 * LOG2E), BF16)
    kb_ref[...] = rope_b(zkb).astype(BF16)
    store_transposed(va_ref, zva, BF16)
    store_transposed(vb_ref, zvb, BF16)
    sga_ref[...] = _sigmoid(zga).astype(BF16)
    sgb_ref[...] = _sigmoid(zgb).astype(BF16)


def _in_proj(x2, w_in, tables, seq):
    t = x2.shape[0]
    tm = _IN_TM
    n_seq_tiles = seq // tm
    grid = (HEADS, t // tm)
    d = HEAD_DIM

    def wspec(group):
        return pl.BlockSpec((D_MODEL, d), lambda h, i, g=group: (0, g * HEADS + h))

    def gspec(base):
        return pl.BlockSpec((D_MODEL, 2 * d), lambda h, i, b=base: (0, b + h))

    tspec = pl.BlockSpec((tm, d), lambda h, i: (i % n_seq_tiles, 0))
    ospec = pl.BlockSpec((tm, d), lambda h, i: (i, h))
    tr_spec = pl.BlockSpec((1, tm // KV_BLOCK, d, KV_BLOCK), lambda h, i: (h, i, 0, 0))
    gate_ospec = pl.BlockSpec((tm, 2 * d), lambda h, i: (i, h))
    ksum_rows = tm // MOBA_BLOCK * SUBLANES
    tr_shape = (HEADS, t // KV_BLOCK, d, KV_BLOCK)
    out_shape = (
        jax.ShapeDtypeStruct(tr_shape, F32),
        jax.ShapeDtypeStruct((t, WIDTH), BF16),
        jax.ShapeDtypeStruct((t // MOBA_BLOCK * SUBLANES, WIDTH), F32),
        jax.ShapeDtypeStruct(tr_shape, BF16),
        jax.ShapeDtypeStruct(tr_shape, BF16),
        jax.ShapeDtypeStruct((t, WIDTH), BF16),
        jax.ShapeDtypeStruct(tr_shape, BF16),
        jax.ShapeDtypeStruct((t, D_MODEL), BF16),
        jax.ShapeDtypeStruct((t, D_MODEL), BF16),
    )
    out_specs = (tr_spec, ospec, pl.BlockSpec((ksum_rows, d), lambda h, i: (i, h)),
                 tr_spec, tr_spec, ospec, tr_spec, gate_ospec, gate_ospec)
    gate_a_base = 6 * WIDTH // (2 * d)
    gate_b_base = gate_a_base + D_MODEL // (2 * d)
    in_specs = [pl.BlockSpec((tm, D_MODEL), lambda h, i: (i, 0))]
    in_specs += [wspec(g) for g in range(6)]
    in_specs += [gspec(gate_a_base), gspec(gate_b_base)]
    in_specs += [tspec] * 5
    return pl.pallas_call(
        _in_proj_kernel,
        grid=grid,
        in_specs=in_specs,
        out_specs=out_specs,
        out_shape=out_shape,
        scratch_shapes=[pltpu.VMEM((D_MODEL, _IN_COLS), BF16)],
        compiler_params=_params(("arbitrary", "arbitrary")),
        name="in_proj",
    )(x2, *([w_in] * 8), *tables)


def _pipeline_stage(n, buf, cur, masks, nxt, load_k, load_vt, rhs_ref, acc_ref, s_ref, ms, ls):
    vt_blk = load_vt(n)
    k_next = load_k(n + 1) if nxt else None

    def issue_scores(c):
        s_ref[1 - buf, c] = jnp.dot(k_next, rhs_ref[c], preferred_element_type=F32)

    lead = 2
    for c in nxt[:lead]:
        issue_scores(c)
    rest = nxt[lead:]
    for j, (c, mask) in enumerate(zip(cur, masks)):
        s = s_ref[buf, c]
        if mask is not None:
            s = jnp.where(mask, s, MASKED)
        m_new = jnp.maximum(ms[c], jnp.max(s, axis=0, keepdims=True))
        alpha = jnp.exp2(ms[c] - m_new)
        p = jnp.exp2(s - m_new)
        ls[c] = alpha * ls[c] + jnp.sum(p, axis=0, keepdims=True)
        acc_ref[c] = alpha * acc_ref[c] + jnp.dot(vt_blk, p.astype(BF16), preferred_element_type=F32)
        ms[c] = m_new
        if j < len(rest):
            issue_scores(rest[j])


def _flash_columns(i, n_sub, n_qgroups, blocks_per_iter, load_k, load_vt, rhs_ref, acc_ref, s_ref):
    assert blocks_per_iter % 2 == 0
    assert n_qgroups % blocks_per_iter == 0
    n_groups = n_sub * n_qgroups
    blk = KV_BLOCK
    everyone = list(range(n_groups))
    init = (tuple(jnp.full((1, blk), -jnp.inf, F32) for _ in range(n_groups))
            + tuple(jnp.zeros((1, blk), F32) for _ in range(n_groups)))
    for c in everyone:
        acc_ref[c] = jnp.zeros(acc_ref.shape[1:], F32)
    k_first = load_k(0)
    for c in everyone:
        s_ref[0, c] = jnp.dot(k_first, rhs_ref[c], preferred_element_type=F32)

    stage = functools.partial(_pipeline_stage, load_k=load_k, load_vt=load_vt,
                              rhs_ref=rhs_ref, acc_ref=acc_ref, s_ref=s_ref)

    def body(it, carry):
        ms, ls = list(carry[:n_groups]), list(carry[n_groups:])
        for j in range(blocks_per_iter):
            stage(blocks_per_iter * it + j, j % 2, everyone, [None] * n_groups, everyone, ms=ms, ls=ls)
        return tuple(ms) + tuple(ls)

    n_past = n_qgroups * i
    carry = lax.fori_loop(0, (n_qgroups // blocks_per_iter) * i, body, init)
    ms, ls = list(carry[:n_groups]), list(carry[n_groups:])
    row = lax.broadcasted_iota(jnp.int32, (blk, blk), 0)
    col = lax.broadcasted_iota(jnp.int32, (blk, blk), 1)
    causal = row <= col
    for t in range(n_qgroups):
        cur = [c for c in everyone if c % n_qgroups >= t]
        nxt = [c for c in everyone if c % n_qgroups >= t + 1]
        masks = [causal if c % n_qgroups == t else None for c in cur]
        stage(n_past + t, t % 2, cur, masks, nxt, ms=ms, ls=ls)
    return ms, ls


_MOBA_QGROUPS = 4


def _split_bf16(a):
    hi = a.astype(BF16)
    lo = (a - hi.astype(F32)).astype(BF16)
    return hi, lo


def _moba_kernel(q_ref, k_ref, vt_ref, ksum_ref, o_ref, kaug_sc, rhs_sc, acc_sc, s_sc, *, n_blocks):
    i = pl.program_id(2)
    blk = MOBA_BLOCK
    d = HEAD_DIM
    seq = n_blocks * blk

    @pl.when(i == 0)
    def _():
        kaug_sc[:, 0:d] = k_ref[...]
        row_blk = jnp.right_shift(lax.broadcasted_iota(jnp.int32, (seq, LANES), 0), blk.bit_length() - 1)
        lane = lax.broadcasted_iota(jnp.int32, (seq, LANES), 1)
        kaug_sc[:, d:2 * d] = jnp.where(row_blk == lane, 1.0, 0.0).astype(BF16)

    kmean = jnp.sum(ksum_ref[...].reshape(n_blocks, SUBLANES, d), axis=1) * (1.0 / blk)
    m_hi, m_lo = _split_bf16(kmean)
    row = lax.broadcasted_iota(jnp.int32, (n_blocks, blk), 0).astype(F32)

    for g in range(_MOBA_QGROUPS):
        q_t = q_ref[0, g]
        q_blk = (_MOBA_QGROUPS * i + g).astype(F32)
        q_hi, q_lo = _split_bf16(q_t)
        gate = (jnp.dot(m_hi, q_hi, preferred_element_type=F32)
                + jnp.dot(m_lo, q_hi, preferred_element_type=F32)
                + jnp.dot(m_hi, q_lo, preferred_element_type=F32))
        avail = row < q_blk
        keep = row == q_blk
        val = jnp.where(avail, gate, -jnp.inf)
        for _ in range(MOBA_TOPK):
            best = jnp.max(val, axis=0, keepdims=True)
            cand = (val == best) & avail
            first = jnp.min(jnp.where(cand, row, float(n_blocks)), axis=0, keepdims=True)
            pick = row == first
            keep = keep | pick
            avail = avail & jnp.logical_not(pick)
            val = jnp.where(pick, -jnp.inf, val)
        bias = jnp.where(keep, 0.0, MASKED)
        bias = jnp.concatenate([bias, jnp.zeros((LANES - n_blocks, blk), F32)], axis=0)
        rhs_sc[g] = jnp.concatenate([(q_t * (d ** -0.5 * LOG2E)).astype(BF16), bias.astype(BF16)], axis=0)

    def load_k(n):
        return kaug_sc[pl.ds(pl.multiple_of(n * blk, blk), blk), :]

    def load_vt(n):
        return vt_ref[0, n]

    _, ls = _flash_columns(i, 1, _MOBA_QGROUPS, 4, load_k, load_vt, rhs_sc, acc_sc, s_sc)
    for g in range(_MOBA_QGROUPS):
        o_ref[g * blk:(g + 1) * blk, :] = (acc_sc[g] / ls[g]).T.astype(BF16)


def _moba(qa_t, ka, va_t, ksum, batch, seq):
    n_blocks = seq // MOBA_BLOCK
    d = HEAD_DIM
    tq = _MOBA_QGROUPS * MOBA_BLOCK
    n_tiles = seq // tq
    grid = (batch, HEADS, n_tiles)
    return pl.pallas_call(
        functools.partial(_moba_kernel, n_blocks=n_blocks),
        grid=grid,
        in_specs=[pl.BlockSpec((1, _MOBA_QGROUPS, d, MOBA_BLOCK), lambda b, h, i: (h, b * n_tiles + i, 0, 0)),
                  pl.BlockSpec((seq, d), lambda b, h, i: (b, h)),
                  pl.BlockSpec((1, n_blocks, d, MOBA_BLOCK), lambda b, h, i: (h, b, 0, 0)),
                  pl.BlockSpec((n_blocks * SUBLANES, d), lambda b, h, i: (b, h))],
        out_specs=pl.BlockSpec((tq, d), lambda b, h, i: (b * n_tiles + i, h)),
        out_shape=jax.ShapeDtypeStruct(ka.shape, BF16),
        scratch_shapes=[pltpu.VMEM((seq, 2 * d), BF16),
                        pltpu.VMEM((_MOBA_QGROUPS, 2 * d, MOBA_BLOCK), BF16),
                        pltpu.VMEM((_MOBA_QGROUPS, d, MOBA_BLOCK), F32),
                        pltpu.VMEM((2, _MOBA_QGROUPS, KV_BLOCK, MOBA_BLOCK), F32)],
        compiler_params=_params(("arbitrary", "arbitrary", "arbitrary")),
        name="moba",
    )(qa_t, ka, va_t, ksum)


_DIFF_QGROUPS = 4


def _diff_kernel(q_ref, k_ref, vt_ref, lam_ref, subln_ref, o_ref, rhs_sc, acc_sc, s_sc, *, lam_init):
    i = pl.program_id(2)
    blk = KV_BLOCK
    d = HEAD_DIM
    nq = _DIFF_QGROUPS

    feat = lax.broadcasted_iota(jnp.int32, (d, blk), 0)
    for sub in range(2):
        own = (feat < DIFF_QK_DIM) if sub == 0 else (feat >= DIFF_QK_DIM)
        for g in range(nq):
            q_t = q_ref[0, g]
            rhs_sc[sub * nq + g] = jnp.where(own, q_t, jnp.zeros_like(q_t))

    def load_k(n):
        return k_ref[pl.ds(pl.multiple_of(n * blk, blk), blk), :]

    def load_vt(n):
        return vt_ref[0, n]

    _, ls = _flash_columns(i, 2, nq, 2, load_k, load_vt, rhs_sc, acc_sc, s_sc)

    lq = lam_ref[...]
    lam = (jnp.exp(jnp.sum(lq[0:1, :] * lq[1:2, :], axis=1, keepdims=True))
           - jnp.exp(jnp.sum(lq[2:3, :] * lq[3:4, :], axis=1, keepdims=True)) + lam_init)
    for g in range(nq):
        o = acc_sc[g] / ls[g] - lam * (acc_sc[nq + g] / ls[nq + g])
        o = o * lax.rsqrt(jnp.mean(o * o, axis=0, keepdims=True) + RMS_EPS)
        o_ref[g * blk:(g + 1) * blk, :] = (o.T * subln_ref[...] * (1.0 - lam_init)).astype(BF16)


def _diff(qb_t, kb, vb_t, lambda_qk, subln_w, batch, seq, lam_init):
    d = HEAD_DIM
    n_blocks = seq // KV_BLOCK
    tq = _DIFF_QGROUPS * KV_BLOCK
    n_tiles = seq // tq
    grid = (batch, HEADS, n_tiles)
    return pl.pallas_call(
        functools.partial(_diff_kernel, lam_init=lam_init),
        grid=grid,
        in_specs=[pl.BlockSpec((1, _DIFF_QGROUPS, d, KV_BLOCK), lambda b, h, i: (h, b * n_tiles + i, 0, 0)),
                  pl.BlockSpec((seq, d), lambda b, h, i: (b, h)),
                  pl.BlockSpec((1, n_blocks, d, KV_BLOCK), lambda b, h, i: (h, b, 0, 0)),
                  pl.BlockSpec(lambda_qk.shape, lambda b, h, i: (0, 0)),
                  pl.BlockSpec((1, d), lambda b, h, i: (0, 0))],
        out_specs=pl.BlockSpec((tq, d), lambda b, h, i: (b * n_tiles + i, h)),
        out_shape=jax.ShapeDtypeStruct(kb.shape, BF16),
        scratch_shapes=[pltpu.VMEM((2 * _DIFF_QGROUPS, d, KV_BLOCK), BF16),
                        pltpu.VMEM((2 * _DIFF_QGROUPS, d, KV_BLOCK), F32),
                        pltpu.VMEM((2, 2 * _DIFF_QGROUPS, KV_BLOCK, KV_BLOCK), F32)],
        compiler_params=_params(("arbitrary", "arbitrary", "arbitrary")),
        name="diffattn",
    )(qb_t, kb, vb_t, lambda_qk, subln_w.reshape(1, d))


_ROW_TM = 512
_FFN_TH = 512


def _merge_kernel(ya_ref, yb_ref, sga_ref, sgb_ref, wa_ref, wb_ref, o_ref):
    a = jnp.dot(ya_ref[...], wa_ref[...], preferred_element_type=F32)
    b = jnp.dot(yb_ref[...], wb_ref[...], preferred_element_type=F32)
    o_ref[...] = (sga_ref[...].astype(F32) * a + sgb_ref[...].astype(F32) * b).astype(BF16)


def _merge(ya, yb, sga, sgb, wa, wb):
    t = ya.shape[0]
    tm = _ROW_TM
    row = lambda w: pl.BlockSpec((tm, w), lambda i: (i, 0))
    full = lambda a: pl.BlockSpec(a.shape, lambda i: (0, 0))
    return pl.pallas_call(
        _merge_kernel,
        grid=(t // tm,),
        in_specs=[row(WIDTH), row(WIDTH), row(D_MODEL), row(D_MODEL), full(wa), full(wb)],
        out_specs=row(D_MODEL),
        out_shape=jax.ShapeDtypeStruct((t, D_MODEL), BF16),
        compiler_params=_params(("parallel",)),
        name="merge",
    )(ya, yb, sga, sgb, wa, wb)


def _layer_norm(y, g, b):
    mu = jnp.mean(y, axis=1, keepdims=True)
    c = y - mu
    var = jnp.mean(c * c, axis=1, keepdims=True)
    return c * lax.rsqrt(var + LN_EPS) * g + b


def _out_ln_kernel(m_ref, w_ref, x_ref, g_ref, b_ref, h_ref, hb_ref, *, alpha):
    y = alpha * x_ref[...] + jnp.dot(m_ref[...], w_ref[...], preferred_element_type=F32)
    h = _layer_norm(y, g_ref[...], b_ref[...])
    h_ref[...] = h
    hb_ref[...] = h.astype(BF16)


def _out_ln(m, w_out, x2, g, b, alpha):
    t = m.shape[0]
    tm = _ROW_TM
    row = pl.BlockSpec((tm, D_MODEL), lambda i: (i, 0))
    vec = pl.BlockSpec((1, D_MODEL), lambda i: (0, 0))
    return pl.pallas_call(
        functools.partial(_out_ln_kernel, alpha=alpha),
        grid=(t // tm,),
        in_specs=[row, pl.BlockSpec(w_out.shape, lambda i: (0, 0)), row, vec, vec],
        out_specs=(row, row),
        out_shape=(jax.ShapeDtypeStruct((t, D_MODEL), F32), jax.ShapeDtypeStruct((t, D_MODEL), BF16)),
        compiler_params=_params(("parallel",)),
        name="out_ln",
    )(m, w_out, x2, g.reshape(1, D_MODEL), b.reshape(1, D_MODEL))


def _ffn_kernel(hb_ref, wg_ref, wu_ref, wo_ref, h_ref, g_ref, b_ref, o_ref, *, alpha):
    j = pl.program_id(1)

    @pl.when(j == 0)
    def _():
        o_ref[...] = alpha * h_ref[...]

    hb = hb_ref[...]
    gate = jnp.dot(hb, wg_ref[...], preferred_element_type=F32)
    up = jnp.dot(hb, wu_ref[...], preferred_element_type=F32)
    act = (gate * jax.nn.sigmoid(gate) * up).astype(BF16)
    o_ref[...] += jnp.dot(act, wo_ref[...], preferred_element_type=F32)

    @pl.when(j == pl.num_programs(1) - 1)
    def _():
        o_ref[...] = _layer_norm(o_ref[...], g_ref[...], b_ref[...])


def _ffn_ln(hb, h, w_ffn_in, w_ffn_out, g, b, alpha):
    t = h.shape[0]
    tm, th = _ROW_TM, _FFN_TH
    nj = FFN_HIDDEN // th
    row = pl.BlockSpec((tm, D_MODEL), lambda i, j: (i, 0))
    vec = pl.BlockSpec((1, D_MODEL), lambda i, j: (0, 0))
    return pl.pallas_call(
        functools.partial(_ffn_kernel, alpha=alpha),
        grid=(t // tm, nj),
        in_specs=[row,
                  pl.BlockSpec((D_MODEL, th), lambda i, j: (0, j)),
                  pl.BlockSpec((D_MODEL, th), lambda i, j: (0, nj + j)),
                  pl.BlockSpec((th, D_MODEL), lambda i, j: (j, 0)),
                  row, vec, vec],
        out_specs=row,
        out_shape=jax.ShapeDtypeStruct((t, D_MODEL), F32),
        compiler_params=_params(("parallel", "arbitrary")),
        name="ffn_ln",
    )(hb, w_ffn_in, w_ffn_in, w_ffn_out, h, g.reshape(1, D_MODEL), b.reshape(1, D_MODEL))


def kernel(x, w_in, lambda_qk, diff_subln_w, w_branch_a, w_branch_b, w_out,
           ln1_g, ln1_b, w_ffn_in, w_ffn_out, ln2_g, ln2_b):
    batch, seq, _ = x.shape
    depth = w_in.shape[0]
    alpha = (2.0 * depth) ** 0.25
    tables = _rope_tables(seq)
    h = x.reshape(batch * seq, D_MODEL)
    for l in range(depth):
        lam_init = 0.8 - 0.6 * math.exp(-0.3 * l)
        qa_t, ka, ksum, va_t, qb_t, kb, vb_t, sga, sgb = _in_proj(h, w_in[l], tables, seq)
        ya = _moba(qa_t, ka, va_t, ksum, batch, seq)
        yb = _diff(qb_t, kb, vb_t, lambda_qk[l], diff_subln_w[l], batch, seq, lam_init)
        m = _merge(ya, yb, sga, sgb, w_branch_a[l].astype(BF16), w_branch_b[l].astype(BF16))
        h, hb = _out_ln(m, w_out[l].astype(BF16), h, ln1_g[l], ln1_b[l], alpha)
        h = _ffn_ln(hb, h, w_ffn_in[l].astype(BF16), w_ffn_out[l].astype(BF16), ln2_g[l], ln2_b[l], alpha)
    return h.reshape(batch, seq, D_MODEL)
```

```python
import functools
import math

import jax
import jax.numpy as jnp
from jax import lax
from jax.experimental import pallas as pl
from jax.experimental.pallas import tpu as pltpu

D_MODEL = 2048
HEADS = 8
HEAD_DIM = 128
WIDTH = HEADS * HEAD_DIM
MOBA_BLOCK = 256
MOBA_TOPK = 3
DIFF_QK_DIM = 64
ROPE_THETA = 10000.0
FFN_HIDDEN = 5632
LN_EPS = 1e-5
RMS_EPS = 1e-5

LANES = 128
SUBLANES = 8
VMEM_LIMIT = 56 * 1024 * 1024
MASKED = -1e30
LOG2E = math.log2(math.e)
KV_BLOCK = 256
V_ROWS = HEAD_DIM + 16

BF16 = jnp.bfloat16
F32 = jnp.float32


def _params(sem):
    return pltpu.CompilerParams(dimension_semantics=sem, vmem_limit_bytes=VMEM_LIMIT)


def _rope_tables(seq):
    pos = jnp.arange(seq, dtype=F32)[:, None]

    def half_tables(d):
        half = d // 2
        inv = ROPE_THETA ** (-jnp.arange(half, dtype=F32) * 2.0 / d)
        ang = pos * inv[None, :]
        return jnp.cos(ang), jnp.sin(ang)

    ca, sa = half_tables(HEAD_DIM)
    cos_a = jnp.concatenate([ca, ca], axis=1)
    sin_a = jnp.concatenate([-sa, sa], axis=1)
    cb, sb = half_tables(DIFF_QK_DIM)
    zero = jnp.zeros_like(sb)
    cos_b = jnp.concatenate([cb, cb, cb, cb], axis=1)
    sin_b_lo = jnp.concatenate([-sb, zero, -sb, zero], axis=1)
    sin_b_hi = jnp.concatenate([zero, sb, zero, sb], axis=1)
    return cos_a, sin_a, cos_b, sin_b_lo, sin_b_hi


_IN_TM = 512
_IN_COLS = 6 * HEAD_DIM + 2 * 2 * HEAD_DIM


def _sigmoid(t):
    return 0.5 * jnp.tanh(0.5 * t) + 0.5


def _in_proj_kernel(x_ref, wqa, wka, wva, wqb, wkb, wvb, wga, wgb,
                    cos_a, sin_a, cos_b, sin_b_lo, sin_b_hi,
                    qa_ref, ka_ref, ksum_ref, va_ref, qb_ref, kb_ref, vb_ref, sga_ref, sgb_ref,
                    w_sc):
    @pl.when(pl.program_id(1) == 0)
    def _():
        off = 0
        for w in (wqa, wka, wva, wqb, wkb, wvb, wga, wgb):
            n = w.shape[1]
            w_sc[:, off:off + n] = w[...].astype(BF16)
            off += n

    z = jnp.dot(x_ref[...].astype(BF16), w_sc[...], preferred_element_type=F32)
    d = HEAD_DIM
    blk = KV_BLOCK
    zqa, zka, zva, zqb, zkb, zvb = (z[:, j * d:(j + 1) * d] for j in range(6))
    zga, zgb = z[:, 6 * d:8 * d], z[:, 8 * d:10 * d]

    def rope_a(t):
        return t * cos_a[...] + pltpu.roll(t, d // 2, 1) * sin_a[...]

    def rope_b(t):
        return (t * cos_b[...] + pltpu.roll(t, d - DIFF_QK_DIM // 2, 1) * sin_b_lo[...]
                + pltpu.roll(t, DIFF_QK_DIM // 2, 1) * sin_b_hi[...])

    def store_transposed(ref, t, dtype):
        for j in range(_IN_TM // blk):
            ref[0, j, 0:d] = t[j * blk:(j + 1) * blk, :].T.astype(dtype)
            if ref.shape[2] > d:
                ref[0, j, d:] = jnp.ones((ref.shape[2] - d, blk), dtype)

    store_transposed(qa_ref, rope_a(zqa), F32)
    ka = rope_a(zka)
    ka_ref[...] = ka.astype(BF16)
    for j in range(_IN_TM // MOBA_BLOCK):
        part = ka[j * MOBA_BLOCK:(j + 1) * MOBA_BLOCK].reshape(MOBA_BLOCK // SUBLANES, SUBLANES, d)
        ksum_ref[j * SUBLANES:(j + 1) * SUBLANES, :] = jnp.sum(part, axis=0)
    store_transposed(qb_ref, rope_b(zqb) * (DIFF_QK_DIM ** -0.5 * LOG2E), BF16)
    kb_ref[...] = rope_b(zkb).astype(BF16)
    store_transposed(va_ref, zva, BF16)
    store_transposed(vb_ref, zvb, BF16)
    sga_ref[...] = _sigmoid(zga).astype(BF16)
    sgb_ref[...] = _sigmoid(zgb).astype(BF16)


def _in_proj(x2, w_in, tables, seq):
    t = x2.shape[0]
    tm = _IN_TM
    n_seq_tiles = seq // tm
    grid = (HEADS, t // tm)
    d = HEAD_DIM

    def wspec(group):
        return pl.BlockSpec((D_MODEL, d), lambda h, i, g=group: (0, g * HEADS + h))

    def gspec(base):
        return pl.BlockSpec((D_MODEL, 2 * d), lambda h, i, b=base: (0, b + h))

    tspec = pl.BlockSpec((tm, d), lambda h, i: (i % n_seq_tiles, 0))
    ospec = pl.BlockSpec((tm, d), lambda h, i: (i, h))
    tr_spec = pl.BlockSpec((1, tm // KV_BLOCK, d, KV_BLOCK), lambda h, i: (h, i, 0, 0))
    gate_ospec = pl.BlockSpec((tm, 2 * d), lambda h, i: (i, h))
    ksum_rows = tm // MOBA_BLOCK * SUBLANES
    tr_shape = (HEADS, t // KV_BLOCK, d, KV_BLOCK)
    v_spec = pl.BlockSpec((1, tm // KV_BLOCK, V_ROWS, KV_BLOCK), lambda h, i: (h, i, 0, 0))
    v_shape = (HEADS, t // KV_BLOCK, V_ROWS, KV_BLOCK)
    out_shape = (
        jax.ShapeDtypeStruct(tr_shape, F32),
        jax.ShapeDtypeStruct((t, WIDTH), BF16),
        jax.ShapeDtypeStruct((t // MOBA_BLOCK * SUBLANES, WIDTH), F32),
        jax.ShapeDtypeStruct(v_shape, BF16),
        jax.ShapeDtypeStruct(tr_shape, BF16),
        jax.ShapeDtypeStruct((t, WIDTH), BF16),
        jax.ShapeDtypeStruct(v_shape, BF16),
        jax.ShapeDtypeStruct((t, D_MODEL), BF16),
        jax.ShapeDtypeStruct((t, D_MODEL), BF16),
    )
    out_specs = (tr_spec, ospec, pl.BlockSpec((ksum_rows, d), lambda h, i: (i, h)),
                 v_spec, tr_spec, ospec, v_spec, gate_ospec, gate_ospec)
    gate_a_base = 6 * WIDTH // (2 * d)
    gate_b_base = gate_a_base + D_MODEL // (2 * d)
    in_specs = [pl.BlockSpec((tm, D_MODEL), lambda h, i: (i, 0))]
    in_specs += [wspec(g) for g in range(6)]
    in_specs += [gspec(gate_a_base), gspec(gate_b_base)]
    in_specs += [tspec] * 5
    return pl.pallas_call(
        _in_proj_kernel,
        grid=grid,
        in_specs=in_specs,
        out_specs=out_specs,
        out_shape=out_shape,
        scratch_shapes=[pltpu.VMEM((D_MODEL, _IN_COLS), BF16)],
        compiler_params=_params(("arbitrary", "arbitrary")),
        name="in_proj",
    )(x2, *([w_in] * 8), *tables)


def _pipeline_stage(n, buf, cur, masks, nxt, load_k, load_vt, rhs_ref, acc_ref, s_ref, ms):
    vt_blk = load_vt(n)
    k_next = load_k(n + 1) if nxt else None

    def issue_scores(c):
        s_ref[1 - buf, c] = jnp.dot(k_next, rhs_ref[c], preferred_element_type=F32)

    lead = 2
    for c in nxt[:lead]:
        issue_scores(c)
    rest = nxt[lead:]
    for j, (c, mask) in enumerate(zip(cur, masks)):
        s = s_ref[buf, c]
        if mask is not None:
            s = jnp.where(mask, s, MASKED)
        m_new = jnp.maximum(ms[c], jnp.max(s, axis=0, keepdims=True))
        alpha = jnp.exp2(ms[c] - m_new)
        p = jnp.exp2(s - m_new).astype(BF16)
        acc_ref[c] = alpha * acc_ref[c] + jnp.dot(vt_blk, p, preferred_element_type=F32)
        ms[c] = m_new
        if j < len(rest):
            issue_scores(rest[j])


def _flash_columns(i, n_sub, n_qgroups, blocks_per_iter, load_k, load_vt, rhs_ref, acc_ref, s_ref):
    assert blocks_per_iter % 2 == 0
    assert n_qgroups % blocks_per_iter == 0
    n_groups = n_sub * n_qgroups
    blk = KV_BLOCK
    everyone = list(range(n_groups))
    init = tuple(jnp.full((1, blk), -jnp.inf, F32) for _ in range(n_groups))
    for c in everyone:
        acc_ref[c] = jnp.zeros(acc_ref.shape[1:], F32)
    k_first = load_k(0)
    for c in everyone:
        s_ref[0, c] = jnp.dot(k_first, rhs_ref[c], preferred_element_type=F32)

    stage = functools.partial(_pipeline_stage, load_k=load_k, load_vt=load_vt,
                              rhs_ref=rhs_ref, acc_ref=acc_ref, s_ref=s_ref)

    def body(it, carry):
        ms = list(carry)
        for j in range(blocks_per_iter):
            stage(blocks_per_iter * it + j, j % 2, everyone, [None] * n_groups, everyone, ms=ms)
        return tuple(ms)

    n_past = n_qgroups * i
    ms = list(lax.fori_loop(0, (n_qgroups // blocks_per_iter) * i, body, init))
    row = lax.broadcasted_iota(jnp.int32, (blk, blk), 0)
    col = lax.broadcasted_iota(jnp.int32, (blk, blk), 1)
    causal = row <= col
    for t in range(n_qgroups):
        cur = [c for c in everyone if c % n_qgroups >= t]
        nxt = [c for c in everyone if c % n_qgroups >= t + 1]
        masks = [causal if c % n_qgroups == t else None for c in cur]
        stage(n_past + t, t % 2, cur, masks, nxt, ms=ms)


def _normalised(acc):
    return acc[0:HEAD_DIM] / acc[HEAD_DIM:HEAD_DIM + 1]


BF16_ROW_TILE = 16


def _cast_specs(weights, grid):
    n_steps = grid[0] * grid[1] * grid[2]
    in_specs, out_specs, out_shapes = [], [], []
    for w in weights:
        rows, cols = w.shape
        share = 1
        while rows * share % (n_steps * BF16_ROW_TILE):
            share *= 2
            assert share <= n_steps, (rows, n_steps)
        spec = pl.BlockSpec((rows * share // n_steps, cols),
                            lambda b, h, i, s=share: (((b * grid[1] + h) * grid[2] + i) // s, 0))
        in_specs.append(spec)
        out_specs.append(spec)
        out_shapes.append(jax.ShapeDtypeStruct(w.shape, BF16))
    return in_specs, out_specs, out_shapes


def _cast_blocks(src_refs, dst_refs):
    for src, dst in zip(src_refs, dst_refs):
        dst[...] = src[...].astype(BF16)


_MOBA_QGROUPS = 4


def _split_bf16(a):
    hi = a.astype(BF16)
    lo = (a - hi.astype(F32)).astype(BF16)
    return hi, lo


def _moba_kernel(*refs, n_blocks, n_cast):
    q_ref, k_ref, vt_ref, ksum_ref = refs[:4]
    o_ref = refs[4 + n_cast]
    kaug_sc, rhs_sc, acc_sc, s_sc = refs[5 + 2 * n_cast:]
    _cast_blocks(refs[4:4 + n_cast], refs[5 + n_cast:5 + 2 * n_cast])
    i = pl.program_id(2)
    blk = MOBA_BLOCK
    d = HEAD_DIM
    seq = n_blocks * blk

    @pl.when(i == 0)
    def _():
        kaug_sc[:, 0:d] = k_ref[...]
        row_blk = jnp.right_shift(lax.broadcasted_iota(jnp.int32, (seq, LANES), 0), blk.bit_length() - 1)
        lane = lax.broadcasted_iota(jnp.int32, (seq, LANES), 1)
        kaug_sc[:, d:2 * d] = jnp.where(row_blk == lane, 1.0, 0.0).astype(BF16)

    kmean = jnp.sum(ksum_ref[...].reshape(n_blocks, SUBLANES, d), axis=1) * (1.0 / blk)
    m_hi, m_lo = _split_bf16(kmean)
    row = lax.broadcasted_iota(jnp.int32, (n_blocks, blk), 0).astype(F32)

    for g in range(_MOBA_QGROUPS):
        q_t = q_ref[0, g]
        q_blk = (_MOBA_QGROUPS * i + g).astype(F32)
        q_hi, q_lo = _split_bf16(q_t)
        gate = (jnp.dot(m_hi, q_hi, preferred_element_type=F32)
                + jnp.dot(m_lo, q_hi, preferred_element_type=F32)
                + jnp.dot(m_hi, q_lo, preferred_element_type=F32))
        avail = row < q_blk
        keep = row == q_blk
        val = jnp.where(avail, gate, -jnp.inf)
        for _ in range(MOBA_TOPK):
            best = jnp.max(val, axis=0, keepdims=True)
            cand = (val == best) & avail
            first = jnp.min(jnp.where(cand, row, float(n_blocks)), axis=0, keepdims=True)
            pick = row == first
            keep = keep | pick
            avail = avail & jnp.logical_not(pick)
            val = jnp.where(pick, -jnp.inf, val)
        bias = jnp.where(keep, 0.0, MASKED)
        bias = jnp.concatenate([bias, jnp.zeros((LANES - n_blocks, blk), F32)], axis=0)
        rhs_sc[g] = jnp.concatenate([(q_t * (d ** -0.5 * LOG2E)).astype(BF16), bias.astype(BF16)], axis=0)

    def load_k(n):
        return kaug_sc[pl.ds(pl.multiple_of(n * blk, blk), blk), :]

    def load_vt(n):
        return vt_ref[0, n]

    _flash_columns(i, 1, _MOBA_QGROUPS, 4, load_k, load_vt, rhs_sc, acc_sc, s_sc)
    for g in range(_MOBA_QGROUPS):
        o_ref[g * blk:(g + 1) * blk, :] = _normalised(acc_sc[g]).T.astype(BF16)


def _moba(qa_t, ka, va_t, ksum, batch, seq, weights_to_cast):
    n_blocks = seq // MOBA_BLOCK
    d = HEAD_DIM
    tq = _MOBA_QGROUPS * MOBA_BLOCK
    n_tiles = seq // tq
    grid = (batch, HEADS, n_tiles)
    cast_in, cast_out, cast_shapes = _cast_specs(weights_to_cast, grid)
    return pl.pallas_call(
        functools.partial(_moba_kernel, n_blocks=n_blocks, n_cast=len(weights_to_cast)),
        grid=grid,
        in_specs=[pl.BlockSpec((1, _MOBA_QGROUPS, d, MOBA_BLOCK), lambda b, h, i: (h, b * n_tiles + i, 0, 0)),
                  pl.BlockSpec((seq, d), lambda b, h, i: (b, h)),
                  pl.BlockSpec((1, n_blocks, V_ROWS, MOBA_BLOCK), lambda b, h, i: (h, b, 0, 0)),
                  pl.BlockSpec((n_blocks * SUBLANES, d), lambda b, h, i: (b, h))] + cast_in,
        out_specs=[pl.BlockSpec((tq, d), lambda b, h, i: (b * n_tiles + i, h))] + cast_out,
        out_shape=[jax.ShapeDtypeStruct(ka.shape, BF16)] + cast_shapes,
        scratch_shapes=[pltpu.VMEM((seq, 2 * d), BF16),
                        pltpu.VMEM((_MOBA_QGROUPS, 2 * d, MOBA_BLOCK), BF16),
                        pltpu.VMEM((_MOBA_QGROUPS, V_ROWS, MOBA_BLOCK), F32),
                        pltpu.VMEM((2, _MOBA_QGROUPS, KV_BLOCK, MOBA_BLOCK), F32)],
        compiler_params=_params(("arbitrary", "arbitrary", "arbitrary")),
        name="moba",
    )(qa_t, ka, va_t, ksum, *weights_to_cast)


_DIFF_QGROUPS = 4


def _diff_kernel(*refs, lam_init, n_cast):
    q_ref, k_ref, vt_ref, lam_ref, subln_ref = refs[:5]
    o_ref = refs[5 + n_cast]
    rhs_sc, acc_sc, s_sc = refs[6 + 2 * n_cast:]
    _cast_blocks(refs[5:5 + n_cast], refs[6 + n_cast:6 + 2 * n_cast])
    i = pl.program_id(2)
    blk = KV_BLOCK
    d = HEAD_DIM
    nq = _DIFF_QGROUPS

    feat = lax.broadcasted_iota(jnp.int32, (d, blk), 0)
    for sub in range(2):
        own = (feat < DIFF_QK_DIM) if sub == 0 else (feat >= DIFF_QK_DIM)
        for g in range(nq):
            q_t = q_ref[0, g]
            rhs_sc[sub * nq + g] = jnp.where(own, q_t, jnp.zeros_like(q_t))

    def load_k(n):
        return k_ref[pl.ds(pl.multiple_of(n * blk, blk), blk), :]

    def load_vt(n):
        return vt_ref[0, n]

    _flash_columns(i, 2, nq, 2, load_k, load_vt, rhs_sc, acc_sc, s_sc)

    lq = lam_ref[...]
    lam = (jnp.exp(jnp.sum(lq[0:1, :] * lq[1:2, :], axis=1, keepdims=True))
           - jnp.exp(jnp.sum(lq[2:3, :] * lq[3:4, :], axis=1, keepdims=True)) + lam_init)
    for g in range(nq):
        o = _normalised(acc_sc[g]) - lam * _normalised(acc_sc[nq + g])
        o = o * lax.rsqrt(jnp.mean(o * o, axis=0, keepdims=True) + RMS_EPS)
        o_ref[g * blk:(g + 1) * blk, :] = (o.T * subln_ref[...] * (1.0 - lam_init)).astype(BF16)


def _diff(qb_t, kb, vb_t, lambda_qk, subln_w, batch, seq, lam_init, weights_to_cast):
    d = HEAD_DIM
    n_blocks = seq // KV_BLOCK
    tq = _DIFF_QGROUPS * KV_BLOCK
    n_tiles = seq // tq
    grid = (batch, HEADS, n_tiles)
    cast_in, cast_out, cast_shapes = _cast_specs(weights_to_cast, grid)
    return pl.pallas_call(
        functools.partial(_diff_kernel, lam_init=lam_init, n_cast=len(weights_to_cast)),
        grid=grid,
        in_specs=[pl.BlockSpec((1, _DIFF_QGROUPS, d, KV_BLOCK), lambda b, h, i: (h, b * n_tiles + i, 0, 0)),
                  pl.BlockSpec((seq, d), lambda b, h, i: (b, h)),
                  pl.BlockSpec((1, n_blocks, V_ROWS, KV_BLOCK), lambda b, h, i: (h, b, 0, 0)),
                  pl.BlockSpec(lambda_qk.shape, lambda b, h, i: (0, 0)),
                  pl.BlockSpec((1, d), lambda b, h, i: (0, 0))] + cast_in,
        out_specs=[pl.BlockSpec((tq, d), lambda b, h, i: (b * n_tiles + i, h))] + cast_out,
        out_shape=[jax.ShapeDtypeStruct(kb.shape, BF16)] + cast_shapes,
        scratch_shapes=[pltpu.VMEM((2 * _DIFF_QGROUPS, d, KV_BLOCK), BF16),
                        pltpu.VMEM((2 * _DIFF_QGROUPS, V_ROWS, KV_BLOCK), F32),
                        pltpu.VMEM((2, 2 * _DIFF_QGROUPS, KV_BLOCK, KV_BLOCK), F32)],
        compiler_params=_params(("arbitrary", "arbitrary", "arbitrary")),
        name="diffattn",
    )(qb_t, kb, vb_t, lambda_qk, subln_w.reshape(1, d), *weights_to_cast)


_ROW_TM = 512
_FFN_TH = 512


def _merge_kernel(ya_ref, yb_ref, sga_ref, sgb_ref, wa_ref, wb_ref, o_ref):
    a = jnp.dot(ya_ref[...], wa_ref[...], preferred_element_type=F32)
    b = jnp.dot(yb_ref[...], wb_ref[...], preferred_element_type=F32)
    o_ref[...] = (sga_ref[...].astype(F32) * a + sgb_ref[...].astype(F32) * b).astype(BF16)


def _merge(ya, yb, sga, sgb, wa, wb):
    t = ya.shape[0]
    tm = _ROW_TM
    row = lambda w: pl.BlockSpec((tm, w), lambda i: (i, 0))
    full = lambda a: pl.BlockSpec(a.shape, lambda i: (0, 0))
    return pl.pallas_call(
        _merge_kernel,
        grid=(t // tm,),
        in_specs=[row(WIDTH), row(WIDTH), row(D_MODEL), row(D_MODEL), full(wa), full(wb)],
        out_specs=row(D_MODEL),
        out_shape=jax.ShapeDtypeStruct((t, D_MODEL), BF16),
        compiler_params=_params(("parallel",)),
        name="merge",
    )(ya, yb, sga, sgb, wa, wb)


def _layer_norm(y, g, b):
    mu = jnp.mean(y, axis=1, keepdims=True)
    c = y - mu
    var = jnp.mean(c * c, axis=1, keepdims=True)
    return c * lax.rsqrt(var + LN_EPS) * g + b


def _out_ln_kernel(m_ref, w_ref, x_ref, g_ref, b_ref, h_ref, hb_ref, *, alpha):
    y = alpha * x_ref[...] + jnp.dot(m_ref[...], w_ref[...], preferred_element_type=F32)
    h = _layer_norm(y, g_ref[...], b_ref[...])
    h_ref[...] = h
    hb_ref[...] = h.astype(BF16)


def _out_ln(m, w_out, x2, g, b, alpha):
    t = m.shape[0]
    tm = _ROW_TM
    row = pl.BlockSpec((tm, D_MODEL), lambda i: (i, 0))
    vec = pl.BlockSpec((1, D_MODEL), lambda i: (0, 0))
    return pl.pallas_call(
        functools.partial(_out_ln_kernel, alpha=alpha),
        grid=(t // tm,),
        in_specs=[row, pl.BlockSpec(w_out.shape, lambda i: (0, 0)), row, vec, vec],
        out_specs=(row, row),
        out_shape=(jax.ShapeDtypeStruct((t, D_MODEL), F32), jax.ShapeDtypeStruct((t, D_MODEL), BF16)),
        compiler_params=_params(("parallel",)),
        name="out_ln",
    )(m, w_out, x2, g.reshape(1, D_MODEL), b.reshape(1, D_MODEL))


def _ffn_kernel(hb_ref, wg_ref, wu_ref, wo_ref, h_ref, g_ref, b_ref, o_ref, *, alpha):
    j = pl.program_id(1)

    @pl.when(j == 0)
    def _():
        o_ref[...] = alpha * h_ref[...]

    hb = hb_ref[...]
    gate = jnp.dot(hb, wg_ref[...], preferred_element_type=F32)
    up = jnp.dot(hb, wu_ref[...], preferred_element_type=F32)
    act = (gate * jax.nn.sigmoid(gate) * up).astype(BF16)
    o_ref[...] += jnp.dot(act, wo_ref[...], preferred_element_type=F32)

    @pl.when(j == pl.num_programs(1) - 1)
    def _():
        o_ref[...] = _layer_norm(o_ref[...], g_ref[...], b_ref[...])


def _ffn_ln(hb, h, w_ffn_in, w_ffn_out, g, b, alpha):
    t = h.shape[0]
    tm, th = _ROW_TM, _FFN_TH
    nj = FFN_HIDDEN // th
    row = pl.BlockSpec((tm, D_MODEL), lambda i, j: (i, 0))
    vec = pl.BlockSpec((1, D_MODEL), lambda i, j: (0, 0))
    return pl.pallas_call(
        functools.partial(_ffn_kernel, alpha=alpha),
        grid=(t // tm, nj),
        in_specs=[row,
                  pl.BlockSpec((D_MODEL, th), lambda i, j: (0, j)),
                  pl.BlockSpec((D_MODEL, th), lambda i, j: (0, nj + j)),
                  pl.BlockSpec((th, D_MODEL), lambda i, j: (j, 0)),
                  row, vec, vec],
        out_specs=row,
        out_shape=jax.ShapeDtypeStruct((t, D_MODEL), F32),
        compiler_params=_params(("parallel", "arbitrary")),
        name="ffn_ln",
    )(hb, w_ffn_in, w_ffn_in, w_ffn_out, h, g.reshape(1, D_MODEL), b.reshape(1, D_MODEL))


def kernel(x, w_in, lambda_qk, diff_subln_w, w_branch_a, w_branch_b, w_out,
           ln1_g, ln1_b, w_ffn_in, w_ffn_out, ln2_g, ln2_b):
    batch, seq, _ = x.shape
    depth = w_in.shape[0]
    alpha = (2.0 * depth) ** 0.25
    tables = _rope_tables(seq)
    h = x.reshape(batch * seq, D_MODEL)
    for l in range(depth):
        lam_init = 0.8 - 0.6 * math.exp(-0.3 * l)
        qa_t, ka, ksum, va_t, qb_t, kb, vb_t, sga, sgb = _in_proj(h, w_in[l], tables, seq)
        ya, wa, wb, wo, w_down = _moba(qa_t, ka, va_t, ksum, batch, seq,
                                       (w_branch_a[l], w_branch_b[l], w_out[l], w_ffn_out[l]))
        yb, w_up = _diff(qb_t, kb, vb_t, lambda_qk[l], diff_subln_w[l], batch, seq, lam_init, (w_ffn_in[l],))
        m = _merge(ya, yb, sga, sgb, wa, wb)
        h, hb = _out_ln(m, wo, h, ln1_g[l], ln1_b[l], alpha)
        h = _ffn_ln(hb, h, w_up, w_down, ln2_g[l], ln2_b[l], alpha)
    return h.reshape(batch, seq, D_MODEL)
```

```python
import functools
import math

import jax
import jax.numpy as jnp
import numpy as np
from jax import lax
from jax.experimental import pallas as pl
from jax.experimental.pallas import tpu as pltpu

D_MODEL = 2048
HEADS = 8
HEAD_DIM = 128
WIDTH = HEADS * HEAD_DIM
MOBA_BLOCK = 256
MOBA_TOPK = 3
DIFF_QK_DIM = 64
ROPE_THETA = 10000.0
FFN_HIDDEN = 5632
LN_EPS = 1e-5
RMS_EPS = 1e-5

LANES = 128
SUBLANES = 8
VMEM_LIMIT = 56 * 1024 * 1024
MASKED = -1e30
LOG2E = math.log2(math.e)
KV_BLOCK = 256
V_ROWS = HEAD_DIM + 16

BF16 = jnp.bfloat16
F32 = jnp.float32


def _params(sem):
    return pltpu.CompilerParams(dimension_semantics=sem, vmem_limit_bytes=VMEM_LIMIT)


def _rope_tables(seq):
    pos = jnp.arange(seq, dtype=F32)[:, None]

    def angles(d):
        half = d // 2
        inv = ROPE_THETA ** (-jnp.arange(half, dtype=F32) * 2.0 / d)
        return pos * jnp.concatenate([inv] * (LANES // half))[None, :]

    lane = np.arange(LANES)
    upper_a = lane % HEAD_DIM >= HEAD_DIM // 2
    upper_b = lane % DIFF_QK_DIM >= DIFF_QK_DIM // 2
    sign_a = np.where(upper_a, 1.0, -1.0).astype(np.float32)
    lo_b = np.where(upper_b, 0.0, -1.0).astype(np.float32)
    hi_b = np.where(upper_b, 1.0, 0.0).astype(np.float32)
    ang_a, ang_b = angles(HEAD_DIM), angles(DIFF_QK_DIM)
    sin_b = jnp.sin(ang_b)
    return jnp.stack([jnp.cos(ang_a), jnp.sin(ang_a) * sign_a, jnp.cos(ang_b), sin_b * lo_b, sin_b * hi_b])


_IN_TM = 512
_IN_COLS = 6 * HEAD_DIM + 2 * 2 * HEAD_DIM


def _sigmoid(t):
    return 0.5 * jnp.tanh(0.5 * t) + 0.5


def _in_proj_kernel(x_ref, wqa, wka, wva, wqb, wkb, wvb, wga, wgb, rope_ref,
                    qa_ref, ka_ref, ksum_ref, va_ref, qb_ref, kb_ref, vb_ref, ga_ref, gb_ref,
                    w_sc):
    cos_a, sin_a, cos_b, sin_b_lo, sin_b_hi = (rope_ref.at[j] for j in range(5))
    @pl.when(pl.program_id(1) == 0)
    def _():
        off = 0
        for w in (wqa, wka, wva, wqb, wkb, wvb, wga, wgb):
            n = w.shape[1]
            w_sc[:, off:off + n] = w[...].astype(BF16)
            off += n

    z = jnp.dot(x_ref[...].astype(BF16), w_sc[...], preferred_element_type=F32)
    d = HEAD_DIM
    blk = KV_BLOCK
    zqa, zka, zva, zqb, zkb, zvb = (z[:, j * d:(j + 1) * d] for j in range(6))
    zga, zgb = z[:, 6 * d:8 * d], z[:, 8 * d:10 * d]

    def rope_a(t):
        return t * cos_a[...] + pltpu.roll(t, d // 2, 1) * sin_a[...]

    def rope_b(t):
        return (t * cos_b[...] + pltpu.roll(t, d - DIFF_QK_DIM // 2, 1) * sin_b_lo[...]
                + pltpu.roll(t, DIFF_QK_DIM // 2, 1) * sin_b_hi[...])

    def store_transposed(ref, t, dtype):
        for j in range(_IN_TM // blk):
            ref[0, j, 0:d] = t[j * blk:(j + 1) * blk, :].T.astype(dtype)
            if ref.shape[2] > d:
                ref[0, j, d:] = jnp.ones((ref.shape[2] - d, blk), dtype)

    store_transposed(qa_ref, rope_a(zqa), F32)
    ka = rope_a(zka)
    ka_ref[...] = ka.astype(BF16)
    for j in range(_IN_TM // MOBA_BLOCK):
        part = ka[j * MOBA_BLOCK:(j + 1) * MOBA_BLOCK].reshape(MOBA_BLOCK // SUBLANES, SUBLANES, d)
        ksum_ref[j * SUBLANES:(j + 1) * SUBLANES, :] = jnp.sum(part, axis=0)
    store_transposed(qb_ref, rope_b(zqb) * (DIFF_QK_DIM ** -0.5 * LOG2E), BF16)
    kb_ref[...] = rope_b(zkb).astype(BF16)
    store_transposed(va_ref, zva, BF16)
    store_transposed(vb_ref, zvb, BF16)
    ga_ref[...] = zga.astype(BF16)
    gb_ref[...] = zgb.astype(BF16)


def _in_proj(x2, w_in, tables, seq):
    t = x2.shape[0]
    tm = _IN_TM
    n_seq_tiles = seq // tm
    grid = (HEADS, t // tm)
    d = HEAD_DIM

    def wspec(group):
        return pl.BlockSpec((D_MODEL, d), lambda h, i, g=group: (0, g * HEADS + h))

    def gspec(base):
        return pl.BlockSpec((D_MODEL, 2 * d), lambda h, i, b=base: (0, b + h))

    tspec = pl.BlockSpec((5, tm, d), lambda h, i: (0, i % n_seq_tiles, 0))
    ospec = pl.BlockSpec((tm, d), lambda h, i: (i, h))
    tr_spec = pl.BlockSpec((1, tm // KV_BLOCK, d, KV_BLOCK), lambda h, i: (h, i, 0, 0))
    gate_ospec = pl.BlockSpec((tm, 2 * d), lambda h, i: (i, h))
    ksum_rows = tm // MOBA_BLOCK * SUBLANES
    tr_shape = (HEADS, t // KV_BLOCK, d, KV_BLOCK)
    v_spec = pl.BlockSpec((1, tm // KV_BLOCK, V_ROWS, KV_BLOCK), lambda h, i: (h, i, 0, 0))
    v_shape = (HEADS, t // KV_BLOCK, V_ROWS, KV_BLOCK)
    out_shape = (
        jax.ShapeDtypeStruct(tr_shape, F32),
        jax.ShapeDtypeStruct((t, WIDTH), BF16),
        jax.ShapeDtypeStruct((t // MOBA_BLOCK * SUBLANES, WIDTH), F32),
        jax.ShapeDtypeStruct(v_shape, BF16),
        jax.ShapeDtypeStruct(tr_shape, BF16),
        jax.ShapeDtypeStruct((t, WIDTH), BF16),
        jax.ShapeDtypeStruct(v_shape, BF16),
        jax.ShapeDtypeStruct((t, D_MODEL), BF16),
        jax.ShapeDtypeStruct((t, D_MODEL), BF16),
    )
    out_specs = (tr_spec, ospec, pl.BlockSpec((ksum_rows, d), lambda h, i: (i, h)),
                 v_spec, tr_spec, ospec, v_spec, gate_ospec, gate_ospec)
    gate_a_base = 6 * WIDTH // (2 * d)
    gate_b_base = gate_a_base + D_MODEL // (2 * d)
    in_specs = [pl.BlockSpec((tm, D_MODEL), lambda h, i: (i, 0))]
    in_specs += [wspec(g) for g in range(6)]
    in_specs += [gspec(gate_a_base), gspec(gate_b_base)]
    in_specs += [tspec]
    return pl.pallas_call(
        _in_proj_kernel,
        grid=grid,
        in_specs=in_specs,
        out_specs=out_specs,
        out_shape=out_shape,
        scratch_shapes=[pltpu.VMEM((D_MODEL, _IN_COLS), BF16)],
        compiler_params=_params(("arbitrary", "arbitrary")),
        name="in_proj",
    )(x2, *([w_in] * 8), tables)


def _pipeline_stage(n, buf, cur, masks, nxt, load_k, load_vt, rhs_ref, acc_ref, s_ref, ms):
    vt_blk = load_vt(n)
    k_next = load_k(n + 1) if nxt else None

    def issue_scores(c):
        s_ref[1 - buf, c] = jnp.dot(k_next, rhs_ref[c], preferred_element_type=F32)

    lead = 2
    for c in nxt[:lead]:
        issue_scores(c)
    rest = nxt[lead:]
    for j, (c, mask) in enumerate(zip(cur, masks)):
        s = s_ref[buf, c]
        if mask is not None:
            s = jnp.where(mask, s, MASKED)
        m_new = jnp.maximum(ms[c], jnp.max(s, axis=0, keepdims=True))
        alpha = jnp.exp2(ms[c] - m_new)
        p = jnp.exp2(s - m_new).astype(BF16)
        acc_ref[c] = alpha * acc_ref[c] + jnp.dot(vt_blk, p, preferred_element_type=F32)
        ms[c] = m_new
        if j < len(rest):
            issue_scores(rest[j])


def _flash_columns(i, n_sub, n_qgroups, blocks_per_iter, load_k, load_vt, rhs_ref, acc_ref, s_ref):
    assert blocks_per_iter % 2 == 0
    assert n_qgroups % blocks_per_iter == 0
    n_groups = n_sub * n_qgroups
    blk = KV_BLOCK
    everyone = list(range(n_groups))
    init = tuple(jnp.full((1, blk), -jnp.inf, F32) for _ in range(n_groups))
    for c in everyone:
        acc_ref[c] = jnp.zeros(acc_ref.shape[1:], F32)
    k_first = load_k(0)
    for c in everyone:
        s_ref[0, c] = jnp.dot(k_first, rhs_ref[c], preferred_element_type=F32)

    stage = functools.partial(_pipeline_stage, load_k=load_k, load_vt=load_vt,
                              rhs_ref=rhs_ref, acc_ref=acc_ref, s_ref=s_ref)

    def body(it, carry):
        ms = list(carry)
        for j in range(blocks_per_iter):
            stage(blocks_per_iter * it + j, j % 2, everyone, [None] * n_groups, everyone, ms=ms)
        return tuple(ms)

    n_past = n_qgroups * i
    ms = list(lax.fori_loop(0, (n_qgroups // blocks_per_iter) * i, body, init))
    row = lax.broadcasted_iota(jnp.int32, (blk, blk), 0)
    col = lax.broadcasted_iota(jnp.int32, (blk, blk), 1)
    causal = row <= col
    for t in range(n_qgroups):
        cur = [c for c in everyone if c % n_qgroups >= t]
        nxt = [c for c in everyone if c % n_qgroups >= t + 1]
        masks = [causal if c % n_qgroups == t else None for c in cur]
        stage(n_past + t, t % 2, cur, masks, nxt, ms=ms)


def _normalised(acc):
    return acc[0:HEAD_DIM] / acc[HEAD_DIM:HEAD_DIM + 1]


BF16_ROW_TILE = 16


def _cast_specs(weights, grid):
    n_steps = grid[0] * grid[1] * grid[2]
    in_specs, out_specs, out_shapes = [], [], []
    for w in weights:
        rows, cols = w.shape
        share = 1
        while rows * share % (n_steps * BF16_ROW_TILE):
            share *= 2
            assert share <= n_steps, (rows, n_steps)
        spec = pl.BlockSpec((rows * share // n_steps, cols),
                            lambda b, h, i, s=share: (((b * grid[1] + h) * grid[2] + i) // s, 0))
        in_specs.append(spec)
        out_specs.append(spec)
        out_shapes.append(jax.ShapeDtypeStruct(w.shape, BF16))
    return in_specs, out_specs, out_shapes


def _cast_blocks(src_refs, dst_refs):
    for src, dst in zip(src_refs, dst_refs):
        dst[...] = src[...].astype(BF16)


_MOBA_QGROUPS = 4


def _split_bf16(a):
    hi = a.astype(BF16)
    lo = (a - hi.astype(F32)).astype(BF16)
    return hi, lo


def _moba_kernel(*refs, n_blocks, n_cast):
    q_ref, k_ref, vt_ref, ksum_ref = refs[:4]
    o_ref = refs[4 + n_cast]
    kaug_sc, rhs_sc, acc_sc, s_sc = refs[5 + 2 * n_cast:]
    _cast_blocks(refs[4:4 + n_cast], refs[5 + n_cast:5 + 2 * n_cast])
    i = pl.program_id(2)
    blk = MOBA_BLOCK
    d = HEAD_DIM
    seq = n_blocks * blk

    @pl.when(i == 0)
    def _():
        kaug_sc[:, 0:d] = k_ref[...]
        row_blk = jnp.right_shift(lax.broadcasted_iota(jnp.int32, (seq, LANES), 0), blk.bit_length() - 1)
        lane = lax.broadcasted_iota(jnp.int32, (seq, LANES), 1)
        kaug_sc[:, d:2 * d] = jnp.where(row_blk == lane, 1.0, 0.0).astype(BF16)

    kmean = jnp.sum(ksum_ref[...].reshape(n_blocks, SUBLANES, d), axis=1) * (1.0 / blk)
    m_hi, m_lo = _split_bf16(kmean)
    row = lax.broadcasted_iota(jnp.int32, (n_blocks, blk), 0).astype(F32)

    for g in range(_MOBA_QGROUPS):
        q_t = q_ref[0, g]
        q_blk = (_MOBA_QGROUPS * i + g).astype(F32)
        q_hi, q_lo = _split_bf16(q_t)
        gate = (jnp.dot(m_hi, q_hi, preferred_element_type=F32)
                + jnp.dot(m_lo, q_hi, preferred_element_type=F32)
                + jnp.dot(m_hi, q_lo, preferred_element_type=F32))
        avail = row < q_blk
        keep = row == q_blk
        val = jnp.where(avail, gate, -jnp.inf)
        for _ in range(MOBA_TOPK):
            best = jnp.max(val, axis=0, keepdims=True)
            cand = (val == best) & avail
            first = jnp.min(jnp.where(cand, row, float(n_blocks)), axis=0, keepdims=True)
            pick = row == first
            keep = keep | pick
            avail = avail & jnp.logical_not(pick)
            val = jnp.where(pick, -jnp.inf, val)
        bias = jnp.where(keep, 0.0, MASKED)
        bias = jnp.concatenate([bias, jnp.zeros((LANES - n_blocks, blk), F32)], axis=0)
        rhs_sc[g] = jnp.concatenate([(q_t * (d ** -0.5 * LOG2E)).astype(BF16), bias.astype(BF16)], axis=0)

    def load_k(n):
        return kaug_sc[pl.ds(pl.multiple_of(n * blk, blk), blk), :]

    def load_vt(n):
        return vt_ref[0, n]

    _flash_columns(i, 1, _MOBA_QGROUPS, 4, load_k, load_vt, rhs_sc, acc_sc, s_sc)
    for g in range(_MOBA_QGROUPS):
        o_ref[g * blk:(g + 1) * blk, :] = _normalised(acc_sc[g]).T.astype(BF16)


def _moba(qa_t, ka, va_t, ksum, batch, seq, weights_to_cast):
    n_blocks = seq // MOBA_BLOCK
    d = HEAD_DIM
    tq = _MOBA_QGROUPS * MOBA_BLOCK
    n_tiles = seq // tq
    grid = (batch, HEADS, n_tiles)
    cast_in, cast_out, cast_shapes = _cast_specs(weights_to_cast, grid)
    return pl.pallas_call(
        functools.partial(_moba_kernel, n_blocks=n_blocks, n_cast=len(weights_to_cast)),
        grid=grid,
        in_specs=[pl.BlockSpec((1, _MOBA_QGROUPS, d, MOBA_BLOCK), lambda b, h, i: (h, b * n_tiles + i, 0, 0)),
                  pl.BlockSpec((seq, d), lambda b, h, i: (b, h)),
                  pl.BlockSpec((1, n_blocks, V_ROWS, MOBA_BLOCK), lambda b, h, i: (h, b, 0, 0)),
                  pl.BlockSpec((n_blocks * SUBLANES, d), lambda b, h, i: (b, h))] + cast_in,
        out_specs=[pl.BlockSpec((tq, d), lambda b, h, i: (b * n_tiles + i, h))] + cast_out,
        out_shape=[jax.ShapeDtypeStruct(ka.shape, BF16)] + cast_shapes,
        scratch_shapes=[pltpu.VMEM((seq, 2 * d), BF16),
                        pltpu.VMEM((_MOBA_QGROUPS, 2 * d, MOBA_BLOCK), BF16),
                        pltpu.VMEM((_MOBA_QGROUPS, V_ROWS, MOBA_BLOCK), F32),
                        pltpu.VMEM((2, _MOBA_QGROUPS, KV_BLOCK, MOBA_BLOCK), F32)],
        compiler_params=_params(("arbitrary", "arbitrary", "arbitrary")),
        name="moba",
    )(qa_t, ka, va_t, ksum, *weights_to_cast)


_DIFF_QGROUPS = 4


def _diff_kernel(*refs, lam_init, n_cast):
    q_ref, k_ref, vt_ref, lam_ref, subln_ref = refs[:5]
    o_ref = refs[5 + n_cast]
    rhs_sc, acc_sc, s_sc = refs[6 + 2 * n_cast:]
    _cast_blocks(refs[5:5 + n_cast], refs[6 + n_cast:6 + 2 * n_cast])
    i = pl.program_id(2)
    blk = KV_BLOCK
    d = HEAD_DIM
    nq = _DIFF_QGROUPS

    feat = lax.broadcasted_iota(jnp.int32, (d, blk), 0)
    for sub in range(2):
        own = (feat < DIFF_QK_DIM) if sub == 0 else (feat >= DIFF_QK_DIM)
        for g in range(nq):
            q_t = q_ref[0, g]
            rhs_sc[sub * nq + g] = jnp.where(own, q_t, jnp.zeros_like(q_t))

    def load_k(n):
        return k_ref[pl.ds(pl.multiple_of(n * blk, blk), blk), :]

    def load_vt(n):
        return vt_ref[0, n]

    _flash_columns(i, 2, nq, 2, load_k, load_vt, rhs_sc, acc_sc, s_sc)

    lq = lam_ref[...]
    lam = (jnp.exp(jnp.sum(lq[0:1, :] * lq[1:2, :], axis=1, keepdims=True))
           - jnp.exp(jnp.sum(lq[2:3, :] * lq[3:4, :], axis=1, keepdims=True)) + lam_init)
    for g in range(nq):
        o = _normalised(acc_sc[g]) - lam * _normalised(acc_sc[nq + g])
        o = o * lax.rsqrt(jnp.mean(o * o, axis=0, keepdims=True) + RMS_EPS)
        o_ref[g * blk:(g + 1) * blk, :] = (o.T * subln_ref[...] * (1.0 - lam_init)).astype(BF16)


def _diff(qb_t, kb, vb_t, lambda_qk, subln_w, batch, seq, lam_init, weights_to_cast):
    d = HEAD_DIM
    n_blocks = seq // KV_BLOCK
    tq = _DIFF_QGROUPS * KV_BLOCK
    n_tiles = seq // tq
    grid = (batch, HEADS, n_tiles)
    cast_in, cast_out, cast_shapes = _cast_specs(weights_to_cast, grid)
    return pl.pallas_call(
        functools.partial(_diff_kernel, lam_init=lam_init, n_cast=len(weights_to_cast)),
        grid=grid,
        in_specs=[pl.BlockSpec((1, _DIFF_QGROUPS, d, KV_BLOCK), lambda b, h, i: (h, b * n_tiles + i, 0, 0)),
                  pl.BlockSpec((seq, d), lambda b, h, i: (b, h)),
                  pl.BlockSpec((1, n_blocks, V_ROWS, KV_BLOCK), lambda b, h, i: (h, b, 0, 0)),
                  pl.BlockSpec(lambda_qk.shape, lambda b, h, i: (0, 0)),
                  pl.BlockSpec((1, d), lambda b, h, i: (0, 0))] + cast_in,
        out_specs=[pl.BlockSpec((tq, d), lambda b, h, i: (b * n_tiles + i, h))] + cast_out,
        out_shape=[jax.ShapeDtypeStruct(kb.shape, BF16)] + cast_shapes,
        scratch_shapes=[pltpu.VMEM((2 * _DIFF_QGROUPS, d, KV_BLOCK), BF16),
                        pltpu.VMEM((2 * _DIFF_QGROUPS, V_ROWS, KV_BLOCK), F32),
                        pltpu.VMEM((2, 2 * _DIFF_QGROUPS, KV_BLOCK, KV_BLOCK), F32)],
        compiler_params=_params(("arbitrary", "arbitrary", "arbitrary")),
        name="diffattn",
    )(qb_t, kb, vb_t, lambda_qk, subln_w.reshape(1, d), *weights_to_cast)


_ROW_TM = 512
_FFN_TH = 512


def _merge_kernel(ya_ref, yb_ref, ga_ref, gb_ref, wa_ref, wb_ref, o_ref):
    a = jnp.dot(ya_ref[...], wa_ref[...], preferred_element_type=F32)
    b = jnp.dot(yb_ref[...], wb_ref[...], preferred_element_type=F32)
    o_ref[...] = (_sigmoid(ga_ref[...].astype(F32)) * a + _sigmoid(gb_ref[...].astype(F32)) * b).astype(BF16)


def _merge(ya, yb, ga, gb, wa, wb):
    t = ya.shape[0]
    tm = _ROW_TM
    row = lambda w: pl.BlockSpec((tm, w), lambda i: (i, 0))
    full = lambda a: pl.BlockSpec(a.shape, lambda i: (0, 0))
    return pl.pallas_call(
        _merge_kernel,
        grid=(t // tm,),
        in_specs=[row(WIDTH), row(WIDTH), row(D_MODEL), row(D_MODEL), full(wa), full(wb)],
        out_specs=row(D_MODEL),
        out_shape=jax.ShapeDtypeStruct((t, D_MODEL), BF16),
        compiler_params=_params(("parallel",)),
        name="merge",
    )(ya, yb, ga, gb, wa, wb)


def _layer_norm(y, g, b):
    mu = jnp.mean(y, axis=1, keepdims=True)
    c = y - mu
    var = jnp.mean(c * c, axis=1, keepdims=True)
    return c * lax.rsqrt(var + LN_EPS) * g + b


def _row_halves(ref):
    half = ref.shape[0] // 2
    return slice(0, half), slice(half, 2 * half)


def _out_ln_kernel(m_ref, w_ref, x_ref, g_ref, b_ref, h_ref, hb_ref, *, alpha):
    halves = _row_halves(m_ref)
    ys = [alpha * x_ref[rows, :] + jnp.dot(m_ref[rows, :], w_ref[...], preferred_element_type=F32)
          for rows in halves]
    for rows, y in zip(halves, ys):
        h = _layer_norm(y, g_ref[...], b_ref[...])
        h_ref[rows, :] = h
        hb_ref[rows, :] = h.astype(BF16)


def _out_ln(m, w_out, x2, g, b, alpha):
    t = m.shape[0]
    tm = _ROW_TM
    row = pl.BlockSpec((tm, D_MODEL), lambda i: (i, 0))
    vec = pl.BlockSpec((1, D_MODEL), lambda i: (0, 0))
    return pl.pallas_call(
        functools.partial(_out_ln_kernel, alpha=alpha),
        grid=(t // tm,),
        in_specs=[row, pl.BlockSpec(w_out.shape, lambda i: (0, 0)), row, vec, vec],
        out_specs=(row, row),
        out_shape=(jax.ShapeDtypeStruct((t, D_MODEL), F32), jax.ShapeDtypeStruct((t, D_MODEL), BF16)),
        compiler_params=_params(("parallel",)),
        name="out_ln",
    )(m, w_out, x2, g.reshape(1, D_MODEL), b.reshape(1, D_MODEL))


def _ffn_kernel(hb_ref, wg_ref, wu_ref, wo_ref, h_ref, g_ref, b_ref, o_ref, *, alpha):
    j = pl.program_id(1)

    @pl.when(j == 0)
    def _():
        o_ref[...] = alpha * h_ref[...]

    hb = hb_ref[...]
    gate = jnp.dot(hb, wg_ref[...], preferred_element_type=F32)
    up = jnp.dot(hb, wu_ref[...], preferred_element_type=F32)
    act = (gate * _sigmoid(gate) * up).astype(BF16)
    o_ref[...] += jnp.dot(act, wo_ref[...], preferred_element_type=F32)

    @pl.when(j == pl.num_programs(1) - 1)
    def _():
        o_ref[...] = _layer_norm(o_ref[...], g_ref[...], b_ref[...])


def _ffn_ln(hb, h, w_ffn_in, w_ffn_out, g, b, alpha):
    t = h.shape[0]
    tm, th = _ROW_TM, _FFN_TH
    nj = FFN_HIDDEN // th
    row = pl.BlockSpec((tm, D_MODEL), lambda i, j: (i, 0))
    vec = pl.BlockSpec((1, D_MODEL), lambda i, j: (0, 0))
    return pl.pallas_call(
        functools.partial(_ffn_kernel, alpha=alpha),
        grid=(t // tm, nj),
        in_specs=[row,
                  pl.BlockSpec((D_MODEL, th), lambda i, j: (0, j)),
                  pl.BlockSpec((D_MODEL, th), lambda i, j: (0, nj + j)),
                  pl.BlockSpec((th, D_MODEL), lambda i, j: (j, 0)),
                  row, vec, vec],
        out_specs=row,
        out_shape=jax.ShapeDtypeStruct((t, D_MODEL), F32),
        compiler_params=_params(("parallel", "arbitrary")),
        name="ffn_ln",
    )(hb, w_ffn_in, w_ffn_in, w_ffn_out, h, g.reshape(1, D_MODEL), b.reshape(1, D_MODEL))


def kernel(x, w_in, lambda_qk, diff_subln_w, w_branch_a, w_branch_b, w_out,
           ln1_g, ln1_b, w_ffn_in, w_ffn_out, ln2_g, ln2_b):
    batch, seq, _ = x.shape
    depth = w_in.shape[0]
    alpha = (2.0 * depth) ** 0.25
    tables = _rope_tables(seq)
    h = x.reshape(batch * seq, D_MODEL)
    for l in range(depth):
        lam_init = 0.8 - 0.6 * math.exp(-0.3 * l)
        qa_t, ka, ksum, va_t, qb_t, kb, vb_t, ga, gb = _in_proj(h, w_in[l], tables, seq)
        ya, wa, wb, wo, w_down = _moba(qa_t, ka, va_t, ksum, batch, seq,
                                       (w_branch_a[l], w_branch_b[l], w_out[l], w_ffn_out[l]))
        yb, w_up = _diff(qb_t, kb, vb_t, lambda_qk[l], diff_subln_w[l], batch, seq, lam_init, (w_ffn_in[l],))
        m = _merge(ya, yb, ga, gb, wa, wb)
        h, hb = _out_ln(m, wo, h, ln1_g[l], ln1_b[l], alpha)
        h = _ffn_ln(hb, h, w_up, w_down, ln2_g[l], ln2_b[l], alpha)
    return h.reshape(batch, seq, D_MODEL)
```

```python
import functools
import math

import jax
import jax.numpy as jnp
import numpy as np
from jax import lax
from jax.experimental import pallas as pl
from jax.experimental.pallas import tpu as pltpu

D_MODEL = 2048
HEADS = 8
HEAD_DIM = 128
WIDTH = HEADS * HEAD_DIM
MOBA_BLOCK = 256
MOBA_TOPK = 3
DIFF_QK_DIM = 64
ROPE_THETA = 10000.0
FFN_HIDDEN = 5632
LN_EPS = 1e-5
RMS_EPS = 1e-5

LANES = 128
SUBLANES = 8
VMEM_LIMIT = 56 * 1024 * 1024
MASKED = -1e30
LOG2E = math.log2(math.e)
KV_BLOCK = 256
V_ROWS = HEAD_DIM + 16

BF16 = jnp.bfloat16
F32 = jnp.float32


def _params(sem):
    return pltpu.CompilerParams(dimension_semantics=sem, vmem_limit_bytes=VMEM_LIMIT)


_N_ROPE_TABLES = 5


def _rope_tables(seq):
    half_a, half_b = HEAD_DIM // 2, DIFF_QK_DIM // 2
    pos = jnp.arange(seq, dtype=F32)[:, None]
    inv = ROPE_THETA ** (-jnp.arange(half_a, dtype=F32) * 2.0 / HEAD_DIM)
    ang = pos * inv[None, :]
    cos_sin = jnp.concatenate([jnp.cos(ang), jnp.sin(ang)], axis=1)

    spread = np.zeros((2 * half_a, _N_ROPE_TABLES * LANES), np.float32)
    for lane in range(LANES):
        fa = lane % half_a
        fb = 2 * (lane % half_b)
        upper_a = lane % HEAD_DIM >= half_a
        upper_b = lane % DIFF_QK_DIM >= half_b
        spread[fa, 0 * LANES + lane] = 1.0
        spread[half_a + fa, 1 * LANES + lane] = 1.0 if upper_a else -1.0
        spread[fb, 2 * LANES + lane] = 1.0
        if upper_b:
            spread[half_a + fb, 4 * LANES + lane] = 1.0
        else:
            spread[half_a + fb, 3 * LANES + lane] = -1.0
    return jnp.dot(cos_sin, spread, precision=lax.Precision.HIGHEST)


_IN_TM = 512
_IN_COLS = 6 * HEAD_DIM + 2 * 2 * HEAD_DIM


def _sigmoid(t):
    return 0.5 * jnp.tanh(0.5 * t) + 0.5


def _in_proj_kernel(x_ref, wqa, wka, wva, wqb, wkb, wvb, wga, wgb, rope_ref,
                    qa_hi_ref, qa_lo_ref, ka_ref, ksum_ref, va_ref, qb_ref, kb_ref, vb_ref, ga_ref, gb_ref,
                    w_sc):
    cos_a, sin_a, cos_b, sin_b_lo, sin_b_hi = (
        rope_ref.at[:, j * LANES:(j + 1) * LANES] for j in range(_N_ROPE_TABLES))
    @pl.when(pl.program_id(1) == 0)
    def _():
        off = 0
        for w in (wqa, wka, wva, wqb, wkb, wvb, wga, wgb):
            n = w.shape[1]
            w_sc[:, off:off + n] = w[...].astype(BF16)
            off += n

    z = jnp.dot(x_ref[...].astype(BF16), w_sc[...], preferred_element_type=F32)
    d = HEAD_DIM
    blk = KV_BLOCK
    zqa, zka, zva, zqb, zkb, zvb = (z[:, j * d:(j + 1) * d] for j in range(6))
    zga, zgb = z[:, 6 * d:8 * d], z[:, 8 * d:10 * d]

    def rope_a(t):
        return t * cos_a[...] + pltpu.roll(t, d // 2, 1) * sin_a[...]

    def rope_b(t):
        return (t * cos_b[...] + pltpu.roll(t, d - DIFF_QK_DIM // 2, 1) * sin_b_lo[...]
                + pltpu.roll(t, DIFF_QK_DIM // 2, 1) * sin_b_hi[...])

    def store_transposed(ref, t, dtype):
        for j in range(_IN_TM // blk):
            ref[0, j, 0:d] = t[j * blk:(j + 1) * blk, :].T.astype(dtype)
            if ref.shape[2] > d:
                ref[0, j, d:] = jnp.ones((ref.shape[2] - d, blk), dtype)

    qa = rope_a(zqa)
    store_transposed(qa_hi_ref, qa, BF16)
    store_transposed(qa_lo_ref, qa - qa.astype(BF16).astype(F32), BF16)
    ka = rope_a(zka) * (d ** -0.5 * LOG2E)
    ka_ref[...] = ka.astype(BF16)
    for j in range(_IN_TM // MOBA_BLOCK):
        part = ka[j * MOBA_BLOCK:(j + 1) * MOBA_BLOCK].reshape(MOBA_BLOCK // SUBLANES, SUBLANES, d)
        ksum_ref[j * SUBLANES:(j + 1) * SUBLANES, :] = jnp.sum(part, axis=0)
    store_transposed(qb_ref, rope_b(zqb) * (DIFF_QK_DIM ** -0.5 * LOG2E), BF16)
    kb_ref[...] = rope_b(zkb).astype(BF16)
    store_transposed(va_ref, zva, BF16)
    store_transposed(vb_ref, zvb, BF16)
    ga_ref[...] = zga.astype(BF16)
    gb_ref[...] = zgb.astype(BF16)


def _in_proj(x2, w_in, tables, seq):
    t = x2.shape[0]
    tm = _IN_TM
    n_seq_tiles = seq // tm
    grid = (HEADS, t // tm)
    d = HEAD_DIM

    def wspec(group):
        return pl.BlockSpec((D_MODEL, d), lambda h, i, g=group: (0, g * HEADS + h))

    def gspec(base):
        return pl.BlockSpec((D_MODEL, 2 * d), lambda h, i, b=base: (0, b + h))

    tspec = pl.BlockSpec((tm, _N_ROPE_TABLES * LANES), lambda h, i: (i % n_seq_tiles, 0))
    ospec = pl.BlockSpec((tm, d), lambda h, i: (i, h))
    tr_spec = pl.BlockSpec((1, tm // KV_BLOCK, d, KV_BLOCK), lambda h, i: (h, i, 0, 0))
    gate_ospec = pl.BlockSpec((tm, 2 * d), lambda h, i: (i, h))
    ksum_rows = tm // MOBA_BLOCK * SUBLANES
    tr_shape = (HEADS, t // KV_BLOCK, d, KV_BLOCK)
    v_spec = pl.BlockSpec((1, tm // KV_BLOCK, V_ROWS, KV_BLOCK), lambda h, i: (h, i, 0, 0))
    v_shape = (HEADS, t // KV_BLOCK, V_ROWS, KV_BLOCK)
    out_shape = (
        jax.ShapeDtypeStruct(tr_shape, BF16),
        jax.ShapeDtypeStruct(tr_shape, BF16),
        jax.ShapeDtypeStruct((t, WIDTH), BF16),
        jax.ShapeDtypeStruct((t // MOBA_BLOCK * SUBLANES, WIDTH), F32),
        jax.ShapeDtypeStruct(v_shape, BF16),
        jax.ShapeDtypeStruct(tr_shape, BF16),
        jax.ShapeDtypeStruct((t, WIDTH), BF16),
        jax.ShapeDtypeStruct(v_shape, BF16),
        jax.ShapeDtypeStruct((t, D_MODEL), BF16),
        jax.ShapeDtypeStruct((t, D_MODEL), BF16),
    )
    out_specs = (tr_spec, tr_spec, ospec, pl.BlockSpec((ksum_rows, d), lambda h, i: (i, h)),
                 v_spec, tr_spec, ospec, v_spec, gate_ospec, gate_ospec)
    gate_a_base = 6 * WIDTH // (2 * d)
    gate_b_base = gate_a_base + D_MODEL // (2 * d)
    in_specs = [pl.BlockSpec((tm, D_MODEL), lambda h, i: (i, 0))]
    in_specs += [wspec(g) for g in range(6)]
    in_specs += [gspec(gate_a_base), gspec(gate_b_base)]
    in_specs += [tspec]
    return pl.pallas_call(
        _in_proj_kernel,
        grid=grid,
        in_specs=in_specs,
        out_specs=out_specs,
        out_shape=out_shape,
        scratch_shapes=[pltpu.VMEM((D_MODEL, _IN_COLS), BF16)],
        compiler_params=_params(("arbitrary", "arbitrary")),
        name="in_proj",
    )(x2, *([w_in] * 8), tables)


def _pipeline_stage(n, buf, cur, masks, nxt, load_k, load_vt, rhs_ref, acc_ref, s_ref, ms):
    vt_blk = load_vt(n)
    k_next = load_k(n + 1) if nxt else None

    def issue_scores(c):
        s_ref[1 - buf, c] = jnp.dot(k_next, rhs_ref[c], preferred_element_type=F32)

    lead = 2
    for c in nxt[:lead]:
        issue_scores(c)
    rest = nxt[lead:]
    for j, (c, mask) in enumerate(zip(cur, masks)):
        s = s_ref[buf, c]
        if mask is not None:
            s = jnp.where(mask, s, MASKED)
        m_new = jnp.maximum(ms[c], jnp.max(s, axis=0, keepdims=True))
        alpha = jnp.exp2(ms[c] - m_new)
        p = jnp.exp2(s - m_new).astype(BF16)
        acc_ref[c] = alpha * acc_ref[c] + jnp.dot(vt_blk, p, preferred_element_type=F32)
        ms[c] = m_new
        if j < len(rest):
            issue_scores(rest[j])


def _flash_columns(i, n_sub, n_qgroups, blocks_per_iter, load_k, load_vt, rhs_ref, acc_ref, s_ref):
    assert blocks_per_iter % 2 == 0
    assert n_qgroups % blocks_per_iter == 0
    n_groups = n_sub * n_qgroups
    blk = KV_BLOCK
    everyone = list(range(n_groups))
    init = tuple(jnp.full((1, blk), -jnp.inf, F32) for _ in range(n_groups))
    for c in everyone:
        acc_ref[c] = jnp.zeros(acc_ref.shape[1:], F32)
    k_first = load_k(0)
    for c in everyone:
        s_ref[0, c] = jnp.dot(k_first, rhs_ref[c], preferred_element_type=F32)

    stage = functools.partial(_pipeline_stage, load_k=load_k, load_vt=load_vt,
                              rhs_ref=rhs_ref, acc_ref=acc_ref, s_ref=s_ref)

    def body(it, carry):
        ms = list(carry)
        for j in range(blocks_per_iter):
            stage(blocks_per_iter * it + j, j % 2, everyone, [None] * n_groups, everyone, ms=ms)
        return tuple(ms)

    n_past = n_qgroups * i
    ms = list(lax.fori_loop(0, (n_qgroups // blocks_per_iter) * i, body, init))
    row = lax.broadcasted_iota(jnp.int32, (blk, blk), 0)
    col = lax.broadcasted_iota(jnp.int32, (blk, blk), 1)
    causal = row <= col
    for t in range(n_qgroups):
        cur = [c for c in everyone if c % n_qgroups >= t]
        nxt = [c for c in everyone if c % n_qgroups >= t + 1]
        masks = [causal if c % n_qgroups == t else None for c in cur]
        stage(n_past + t, t % 2, cur, masks, nxt, ms=ms)


def _normalised(acc):
    return acc[0:HEAD_DIM] / acc[HEAD_DIM:HEAD_DIM + 1]


BF16_ROW_TILE = 16


def _cast_specs(weights, grid):
    n_steps = grid[0] * grid[1] * grid[2]
    in_specs, out_specs, out_shapes = [], [], []
    for w in weights:
        rows, cols = w.shape
        share = 1
        while rows * share % (n_steps * BF16_ROW_TILE):
            share *= 2
            assert share <= n_steps, (rows, n_steps)
        spec = pl.BlockSpec((rows * share // n_steps, cols),
                            lambda b, h, i, s=share: (((b * grid[1] + h) * grid[2] + i) // s, 0))
        in_specs.append(spec)
        out_specs.append(spec)
        out_shapes.append(jax.ShapeDtypeStruct(w.shape, BF16))
    return in_specs, out_specs, out_shapes


def _cast_blocks(src_refs, dst_refs):
    for src, dst in zip(src_refs, dst_refs):
        dst[...] = src[...].astype(BF16)


_MOBA_QGROUPS = 4


def _split_bf16(a):
    hi = a.astype(BF16)
    lo = (a - hi.astype(F32)).astype(BF16)
    return hi, lo


def _moba_kernel(*refs, n_blocks, n_cast):
    q_hi_ref, q_lo_ref, k_ref, vt_ref, ksum_ref = refs[:5]
    o_ref = refs[5 + n_cast]
    kaug_sc, rhs_sc, acc_sc, s_sc = refs[6 + 2 * n_cast:]
    _cast_blocks(refs[5:5 + n_cast], refs[6 + n_cast:6 + 2 * n_cast])
    i = pl.program_id(2)
    blk = MOBA_BLOCK
    d = HEAD_DIM
    seq = n_blocks * blk

    @pl.when(i == 0)
    def _():
        kaug_sc[:, 0:d] = k_ref[...]
        row_blk = jnp.right_shift(lax.broadcasted_iota(jnp.int32, (seq, LANES), 0), blk.bit_length() - 1)
        lane = lax.broadcasted_iota(jnp.int32, (seq, LANES), 1)
        kaug_sc[:, d:2 * d] = jnp.where(row_blk == lane, 1.0, 0.0).astype(BF16)

    kmean = jnp.sum(ksum_ref[...].reshape(n_blocks, SUBLANES, d), axis=1) * (1.0 / blk)
    m_hi, m_lo = _split_bf16(kmean)
    row = lax.broadcasted_iota(jnp.int32, (n_blocks, blk), 0).astype(F32)

    for g in range(_MOBA_QGROUPS):
        q_hi, q_lo = q_hi_ref[0, g], q_lo_ref[0, g]
        q_blk = (_MOBA_QGROUPS * i + g).astype(F32)
        gate = (jnp.dot(m_hi, q_hi, preferred_element_type=F32)
                + jnp.dot(m_lo, q_hi, preferred_element_type=F32)
                + jnp.dot(m_hi, q_lo, preferred_element_type=F32))
        avail = row < q_blk
        keep = row == q_blk
        val = jnp.where(avail, gate, -jnp.inf)
        for _ in range(MOBA_TOPK):
            best = jnp.max(val, axis=0, keepdims=True)
            cand = (val == best) & avail
            first = jnp.min(jnp.where(cand, row, float(n_blocks)), axis=0, keepdims=True)
            pick = row == first
            keep = keep | pick
            avail = avail & jnp.logical_not(pick)
            val = jnp.where(pick, -jnp.inf, val)
        bias = jnp.where(keep, 0.0, MASKED)
        bias = jnp.concatenate([bias, jnp.zeros((LANES - n_blocks, blk), F32)], axis=0)
        rhs_sc[g] = jnp.concatenate([q_hi, bias.astype(BF16)], axis=0)

    def load_k(n):
        return kaug_sc[pl.ds(pl.multiple_of(n * blk, blk), blk), :]

    def load_vt(n):
        return vt_ref[0, n]

    _flash_columns(i, 1, _MOBA_QGROUPS, 4, load_k, load_vt, rhs_sc, acc_sc, s_sc)
    for g in range(_MOBA_QGROUPS):
        o_ref[g * blk:(g + 1) * blk, :] = _normalised(acc_sc[g]).T.astype(BF16)


def _moba(qa_hi, qa_lo, ka, va_t, ksum, batch, seq, weights_to_cast):
    n_blocks = seq // MOBA_BLOCK
    d = HEAD_DIM
    tq = _MOBA_QGROUPS * MOBA_BLOCK
    n_tiles = seq // tq
    grid = (batch, HEADS, n_tiles)
    cast_in, cast_out, cast_shapes = _cast_specs(weights_to_cast, grid)
    q_spec = pl.BlockSpec((1, _MOBA_QGROUPS, d, MOBA_BLOCK), lambda b, h, i: (h, b * n_tiles + i, 0, 0))
    return pl.pallas_call(
        functools.partial(_moba_kernel, n_blocks=n_blocks, n_cast=len(weights_to_cast)),
        grid=grid,
        in_specs=[q_spec, q_spec,
                  pl.BlockSpec((seq, d), lambda b, h, i: (b, h)),
                  pl.BlockSpec((1, n_blocks, V_ROWS, MOBA_BLOCK), lambda b, h, i: (h, b, 0, 0)),
                  pl.BlockSpec((n_blocks * SUBLANES, d), lambda b, h, i: (b, h))] + cast_in,
        out_specs=[pl.BlockSpec((tq, d), lambda b, h, i: (b * n_tiles + i, h))] + cast_out,
        out_shape=[jax.ShapeDtypeStruct(ka.shape, BF16)] + cast_shapes,
        scratch_shapes=[pltpu.VMEM((seq, 2 * d), BF16),
                        pltpu.VMEM((_MOBA_QGROUPS, 2 * d, MOBA_BLOCK), BF16),
                        pltpu.VMEM((_MOBA_QGROUPS, V_ROWS, MOBA_BLOCK), F32),
                        pltpu.VMEM((2, _MOBA_QGROUPS, KV_BLOCK, MOBA_BLOCK), F32)],
        compiler_params=_params(("arbitrary", "arbitrary", "arbitrary")),
        name="moba",
    )(qa_hi, qa_lo, ka, va_t, ksum, *weights_to_cast)


_DIFF_QGROUPS = 4


def _diff_kernel(*refs, lam_init, n_cast):
    q_ref, k_ref, vt_ref, lam_ref, subln_ref = refs[:5]
    o_ref = refs[5 + n_cast]
    rhs_sc, acc_sc, s_sc = refs[6 + 2 * n_cast:]
    _cast_blocks(refs[5:5 + n_cast], refs[6 + n_cast:6 + 2 * n_cast])
    i = pl.program_id(2)
    blk = KV_BLOCK
    d = HEAD_DIM
    nq = _DIFF_QGROUPS

    feat = lax.broadcasted_iota(jnp.int32, (d, blk), 0)
    for sub in range(2):
        own = (feat < DIFF_QK_DIM) if sub == 0 else (feat >= DIFF_QK_DIM)
        for g in range(nq):
            q_t = q_ref[0, g]
            rhs_sc[sub * nq + g] = jnp.where(own, q_t, jnp.zeros_like(q_t))

    def load_k(n):
        return k_ref[pl.ds(pl.multiple_of(n * blk, blk), blk), :]

    def load_vt(n):
        return vt_ref[0, n]

    _flash_columns(i, 2, nq, 4, load_k, load_vt, rhs_sc, acc_sc, s_sc)

    lq = lam_ref[...]
    lam = (jnp.exp(jnp.sum(lq[0:1, :] * lq[1:2, :], axis=1, keepdims=True))
           - jnp.exp(jnp.sum(lq[2:3, :] * lq[3:4, :], axis=1, keepdims=True)) + lam_init)
    for g in range(nq):
        o = _normalised(acc_sc[g]) - lam * _normalised(acc_sc[nq + g])
        o = o * lax.rsqrt(jnp.mean(o * o, axis=0, keepdims=True) + RMS_EPS)
        o_ref[g * blk:(g + 1) * blk, :] = (o.T * subln_ref[...] * (1.0 - lam_init)).astype(BF16)


def _diff(qb_t, kb, vb_t, lambda_qk, subln_w, batch, seq, lam_init, weights_to_cast):
    d = HEAD_DIM
    n_blocks = seq // KV_BLOCK
    tq = _DIFF_QGROUPS * KV_BLOCK
    n_tiles = seq // tq
    grid = (batch, HEADS, n_tiles)
    cast_in, cast_out, cast_shapes = _cast_specs(weights_to_cast, grid)
    return pl.pallas_call(
        functools.partial(_diff_kernel, lam_init=lam_init, n_cast=len(weights_to_cast)),
        grid=grid,
        in_specs=[pl.BlockSpec((1, _DIFF_QGROUPS, d, KV_BLOCK), lambda b, h, i: (h, b * n_tiles + i, 0, 0)),
                  pl.BlockSpec((seq, d), lambda b, h, i: (b, h)),
                  pl.BlockSpec((1, n_blocks, V_ROWS, KV_BLOCK), lambda b, h, i: (h, b, 0, 0)),
                  pl.BlockSpec(lambda_qk.shape, lambda b, h, i: (0, 0)),
                  pl.BlockSpec((1, d), lambda b, h, i: (0, 0))] + cast_in,
        out_specs=[pl.BlockSpec((tq, d), lambda b, h, i: (b * n_tiles + i, h))] + cast_out,
        out_shape=[jax.ShapeDtypeStruct(kb.shape, BF16)] + cast_shapes,
        scratch_shapes=[pltpu.VMEM((2 * _DIFF_QGROUPS, d, KV_BLOCK), BF16),
                        pltpu.VMEM((2 * _DIFF_QGROUPS, V_ROWS, KV_BLOCK), F32),
                        pltpu.VMEM((2, 2 * _DIFF_QGROUPS, KV_BLOCK, KV_BLOCK), F32)],
        compiler_params=_params(("arbitrary", "arbitrary", "arbitrary")),
        name="diffattn",
    )(qb_t, kb, vb_t, lambda_qk, subln_w.reshape(1, d), *weights_to_cast)


_ROW_TM = 512
_FFN_TH = 512


def _merge_kernel(ya_ref, yb_ref, ga_ref, gb_ref, wa_ref, wb_ref, o_ref):
    a = jnp.dot(ya_ref[...], wa_ref[...], preferred_element_type=F32)
    b = jnp.dot(yb_ref[...], wb_ref[...], preferred_element_type=F32)
    o_ref[...] = (_sigmoid(ga_ref[...].astype(F32)) * a + _sigmoid(gb_ref[...].astype(F32)) * b).astype(BF16)


def _merge(ya, yb, ga, gb, wa, wb):
    t = ya.shape[0]
    tm = _ROW_TM
    row = lambda w: pl.BlockSpec((tm, w), lambda i: (i, 0))
    full = lambda a: pl.BlockSpec(a.shape, lambda i: (0, 0))
    return pl.pallas_call(
        _merge_kernel,
        grid=(t // tm,),
        in_specs=[row(WIDTH), row(WIDTH), row(D_MODEL), row(D_MODEL), full(wa), full(wb)],
        out_specs=row(D_MODEL),
        out_shape=jax.ShapeDtypeStruct((t, D_MODEL), BF16),
        compiler_params=_params(("parallel",)),
        name="merge",
    )(ya, yb, ga, gb, wa, wb)


def _layer_norm(y, g, b):
    mu = jnp.mean(y, axis=1, keepdims=True)
    c = y - mu
    var = jnp.mean(c * c, axis=1, keepdims=True)
    return c * lax.rsqrt(var + LN_EPS) * g + b


def _row_halves(ref):
    half = ref.shape[0] // 2
    return slice(0, half), slice(half, 2 * half)


def _out_ln_kernel(m_ref, w_ref, x_ref, g_ref, b_ref, h_ref, hb_ref, *, alpha):
    halves = _row_halves(m_ref)
    ys = [alpha * x_ref[rows, :] + jnp.dot(m_ref[rows, :], w_ref[...], preferred_element_type=F32)
          for rows in halves]
    for rows, y in zip(halves, ys):
        h = _layer_norm(y, g_ref[...], b_ref[...])
        h_ref[rows, :] = h
        hb_ref[rows, :] = h.astype(BF16)


def _out_ln(m, w_out, x2, g, b, alpha):
    t = m.shape[0]
    tm = _ROW_TM
    row = pl.BlockSpec((tm, D_MODEL), lambda i: (i, 0))
    vec = pl.BlockSpec((1, D_MODEL), lambda i: (0, 0))
    return pl.pallas_call(
        functools.partial(_out_ln_kernel, alpha=alpha),
        grid=(t // tm,),
        in_specs=[row, pl.BlockSpec(w_out.shape, lambda i: (0, 0)), row, vec, vec],
        out_specs=(row, row),
        out_shape=(jax.ShapeDtypeStruct((t, D_MODEL), F32), jax.ShapeDtypeStruct((t, D_MODEL), BF16)),
        compiler_params=_params(("parallel",)),
        name="out_ln",
    )(m, w_out, x2, g.reshape(1, D_MODEL), b.reshape(1, D_MODEL))


def _ffn_kernel(hb_ref, wg_ref, wu_ref, wo_ref, h_ref, g_ref, b_ref, o_ref, *, alpha):
    j = pl.program_id(1)

    @pl.when(j == 0)
    def _():
        o_ref[...] = alpha * h_ref[...]

    hb = hb_ref[...]
    gate = jnp.dot(hb, wg_ref[...], preferred_element_type=F32)
    up = jnp.dot(hb, wu_ref[...], preferred_element_type=F32)
    act = (gate * _sigmoid(gate) * up).astype(BF16)
    o_ref[...] += jnp.dot(act, wo_ref[...], preferred_element_type=F32)

    @pl.when(j == pl.num_programs(1) - 1)
    def _():
        o_ref[...] = _layer_norm(o_ref[...], g_ref[...], b_ref[...])


def _ffn_ln(hb, h, w_ffn_in, w_ffn_out, g, b, alpha):
    t = h.shape[0]
    tm, th = _ROW_TM, _FFN_TH
    nj = FFN_HIDDEN // th
    row = pl.BlockSpec((tm, D_MODEL), lambda i, j: (i, 0))
    vec = pl.BlockSpec((1, D_MODEL), lambda i, j: (0, 0))
    return pl.pallas_call(
        functools.partial(_ffn_kernel, alpha=alpha),
        grid=(t // tm, nj),
        in_specs=[row,
                  pl.BlockSpec((D_MODEL, th), lambda i, j: (0, j)),
                  pl.BlockSpec((D_MODEL, th), lambda i, j: (0, nj + j)),
                  pl.BlockSpec((th, D_MODEL), lambda i, j: (j, 0)),
                  row, vec, vec],
        out_specs=row,
        out_shape=jax.ShapeDtypeStruct((t, D_MODEL), F32),
        compiler_params=_params(("parallel", "arbitrary")),
        name="ffn_ln",
    )(hb, w_ffn_in, w_ffn_in, w_ffn_out, h, g.reshape(1, D_MODEL), b.reshape(1, D_MODEL))


def kernel(x, w_in, lambda_qk, diff_subln_w, w_branch_a, w_branch_b, w_out,
           ln1_g, ln1_b, w_ffn_in, w_ffn_out, ln2_g, ln2_b):
    batch, seq, _ = x.shape
    depth = w_in.shape[0]
    alpha = (2.0 * depth) ** 0.25
    tables = _rope_tables(seq)
    h = x.reshape(batch * seq, D_MODEL)
    for l in range(depth):
        lam_init = 0.8 - 0.6 * math.exp(-0.3 * l)
        qa_hi, qa_lo, ka, ksum, va_t, qb_t, kb, vb_t, ga, gb = _in_proj(h, w_in[l], tables, seq)
        ya, wa, wb, wo, w_down = _moba(qa_hi, qa_lo, ka, va_t, ksum, batch, seq,
                                       (w_branch_a[l], w_branch_b[l], w_out[l], w_ffn_out[l]))
        yb, w_up = _diff(qb_t, kb, vb_t, lambda_qk[l], diff_subln_w[l], batch, seq, lam_init, (w_ffn_in[l],))
        m = _merge(ya, yb, ga, gb, wa, wb)
        h, hb = _out_ln(m, wo, h, ln1_g[l], ln1_b[l], alpha)
        h = _ffn_ln(hb, h, w_up, w_down, ln2_g[l], ln2_b[l], alpha)
    return h.reshape(batch, seq, D_MODEL)
```

```python
import functools
import math

import jax
import jax.numpy as jnp
import numpy as np
from jax import lax
from jax.experimental import pallas as pl
from jax.experimental.pallas import tpu as pltpu

D_MODEL = 2048
HEADS = 8
HEAD_DIM = 128
WIDTH = HEADS * HEAD_DIM
MOBA_BLOCK = 256
MOBA_TOPK = 3
DIFF_QK_DIM = 64
ROPE_THETA = 10000.0
FFN_HIDDEN = 5632
LN_EPS = 1e-5
RMS_EPS = 1e-5

LANES = 128
SUBLANES = 8
VMEM_LIMIT = 56 * 1024 * 1024
MASKED = -1e30
LOG2E = math.log2(math.e)
KV_BLOCK = 256
V_ROWS = HEAD_DIM + 16

BF16 = jnp.bfloat16
F32 = jnp.float32


def _params(sem):
    return pltpu.CompilerParams(dimension_semantics=sem, vmem_limit_bytes=VMEM_LIMIT)


_N_ROPE_TABLES = 5


def _rope_tables(seq):
    step = 64
    assert seq % step == 0
    half_a, half_b = HEAD_DIM // 2, DIFF_QK_DIM // 2
    lane = np.arange(_N_ROPE_TABLES * LANES)
    table, l = lane // LANES, lane % LANES
    freq = np.where(table < 2, l % half_a, 2 * (l % half_b))
    upper = np.where(table < 2, l % HEAD_DIM >= half_a, l % DIFF_QK_DIM >= half_b)
    use_sin = (table == 1) | (table >= 3)
    coef = np.select([table == 1, table == 3, table == 4],
                     [np.where(upper, 1.0, -1.0),
                      np.where(upper, 0.0, -1.0),
                      np.where(upper, 1.0, 0.0)],
                     default=1.0).astype(np.float32)
    inv = ROPE_THETA ** (-jnp.arange(half_a, dtype=F32) * 2.0 / HEAD_DIM)
    inv_lane = inv[freq][None, :]
    coarse = jnp.arange(seq // step, dtype=F32)[:, None] * float(step) * inv_lane
    fine = jnp.arange(step, dtype=F32)[:, None] * inv_lane
    cc, sc, cf, sf = lax.optimization_barrier((jnp.cos(coarse), jnp.sin(coarse), jnp.cos(fine), jnp.sin(fine)))
    cos_full = cc[:, None, :] * cf[None, :, :] - sc[:, None, :] * sf[None, :, :]
    sin_full = sc[:, None, :] * cf[None, :, :] + cc[:, None, :] * sf[None, :, :]
    return (jnp.where(use_sin, sin_full, cos_full) * coef).reshape(seq, _N_ROPE_TABLES * LANES)


_IN_TM = 512
_IN_COLS = 6 * HEAD_DIM + 2 * 2 * HEAD_DIM


def _sigmoid(t):
    return 0.5 * jnp.tanh(0.5 * t) + 0.5


def _in_proj_kernel(x_ref, wqa, wka, wva, wqb, wkb, wvb, wga, wgb, rope_ref,
                    qa_hi_ref, qa_lo_ref, ka_ref, ksum_ref, va_ref, qb_ref, kb_ref, vb_ref, ga_ref, gb_ref,
                    w_sc):
    cos_a, sin_a, cos_b, sin_b_lo, sin_b_hi = (
        rope_ref.at[:, j * LANES:(j + 1) * LANES] for j in range(_N_ROPE_TABLES))
    @pl.when(pl.program_id(1) == 0)
    def _():
        off = 0
        for w in (wqa, wka, wva, wqb, wkb, wvb, wga, wgb):
            n = w.shape[1]
            w_sc[:, off:off + n] = w[...].astype(BF16)
            off += n

    z = jnp.dot(x_ref[...].astype(BF16), w_sc[...], preferred_element_type=F32)
    d = HEAD_DIM
    blk = KV_BLOCK
    zqa, zka, zva, zqb, zkb, zvb = (z[:, j * d:(j + 1) * d] for j in range(6))
    zga, zgb = z[:, 6 * d:8 * d], z[:, 8 * d:10 * d]

    def rope_a(t):
        return t * cos_a[...] + pltpu.roll(t, d // 2, 1) * sin_a[...]

    def rope_b(t):
        return (t * cos_b[...] + pltpu.roll(t, d - DIFF_QK_DIM // 2, 1) * sin_b_lo[...]
                + pltpu.roll(t, DIFF_QK_DIM // 2, 1) * sin_b_hi[...])

    def store_transposed(ref, t, dtype):
        for j in range(_IN_TM // blk):
            ref[0, j, 0:d] = t[j * blk:(j + 1) * blk, :].T.astype(dtype)
            if ref.shape[2] > d:
                ref[0, j, d:] = jnp.ones((ref.shape[2] - d, blk), dtype)

    qa = rope_a(zqa)
    store_transposed(qa_hi_ref, qa, BF16)
    store_transposed(qa_lo_ref, qa - qa.astype(BF16).astype(F32), BF16)
    ka = rope_a(zka) * (d ** -0.5 * LOG2E)
    ka_ref[...] = ka.astype(BF16)
    for j in range(_IN_TM // MOBA_BLOCK):
        part = ka[j * MOBA_BLOCK:(j + 1) * MOBA_BLOCK].reshape(MOBA_BLOCK // SUBLANES, SUBLANES, d)
        ksum_ref[j * SUBLANES:(j + 1) * SUBLANES, :] = jnp.sum(part, axis=0)
    store_transposed(qb_ref, rope_b(zqb) * (DIFF_QK_DIM ** -0.5 * LOG2E), BF16)
    kb_ref[...] = rope_b(zkb).astype(BF16)
    store_transposed(va_ref, zva, BF16)
    store_transposed(vb_ref, zvb, BF16)
    ga_ref[...] = zga.astype(BF16)
    gb_ref[...] = zgb.astype(BF16)


def _in_proj(x2, w_in, tables, seq):
    t = x2.shape[0]
    tm = _IN_TM
    n_seq_tiles = seq // tm
    grid = (HEADS, t // tm)
    d = HEAD_DIM

    def wspec(group):
        return pl.BlockSpec((D_MODEL, d), lambda h, i, g=group: (0, g * HEADS + h))

    def gspec(base):
        return pl.BlockSpec((D_MODEL, 2 * d), lambda h, i, b=base: (0, b + h))

    tspec = pl.BlockSpec((tm, _N_ROPE_TABLES * LANES), lambda h, i: (i % n_seq_tiles, 0))
    ospec = pl.BlockSpec((tm, d), lambda h, i: (i, h))
    tr_spec = pl.BlockSpec((1, tm // KV_BLOCK, d, KV_BLOCK), lambda h, i: (h, i, 0, 0))
    gate_ospec = pl.BlockSpec((tm, 2 * d), lambda h, i: (i, h))
    ksum_rows = tm // MOBA_BLOCK * SUBLANES
    tr_shape = (HEADS, t // KV_BLOCK, d, KV_BLOCK)
    v_spec = pl.BlockSpec((1, tm // KV_BLOCK, V_ROWS, KV_BLOCK), lambda h, i: (h, i, 0, 0))
    v_shape = (HEADS, t // KV_BLOCK, V_ROWS, KV_BLOCK)
    out_shape = (
        jax.ShapeDtypeStruct(tr_shape, BF16),
        jax.ShapeDtypeStruct(tr_shape, BF16),
        jax.ShapeDtypeStruct((t, WIDTH), BF16),
        jax.ShapeDtypeStruct((t // MOBA_BLOCK * SUBLANES, WIDTH), F32),
        jax.ShapeDtypeStruct(v_shape, BF16),
        jax.ShapeDtypeStruct(tr_shape, BF16),
        jax.ShapeDtypeStruct((t, WIDTH), BF16),
        jax.ShapeDtypeStruct(v_shape, BF16),
        jax.ShapeDtypeStruct((t, D_MODEL), BF16),
        jax.ShapeDtypeStruct((t, D_MODEL), BF16),
    )
    out_specs = (tr_spec, tr_spec, ospec, pl.BlockSpec((ksum_rows, d), lambda h, i: (i, h)),
                 v_spec, tr_spec, ospec, v_spec, gate_ospec, gate_ospec)
    gate_a_base = 6 * WIDTH // (2 * d)
    gate_b_base = gate_a_base + D_MODEL // (2 * d)
    in_specs = [pl.BlockSpec((tm, D_MODEL), lambda h, i: (i, 0))]
    in_specs += [wspec(g) for g in range(6)]
    in_specs += [gspec(gate_a_base), gspec(gate_b_base)]
    in_specs += [tspec]
    return pl.pallas_call(
        _in_proj_kernel,
        grid=grid,
        in_specs=in_specs,
        out_specs=out_specs,
        out_shape=out_shape,
        scratch_shapes=[pltpu.VMEM((D_MODEL, _IN_COLS), BF16)],
        compiler_params=_params(("arbitrary", "arbitrary")),
        name="in_proj",
    )(x2, *([w_in] * 8), tables)


def _pipeline_stage(n, buf, cur, masks, nxt, load_k, load_vt, rhs_ref, acc_ref, s_ref, ms):
    vt_blk = load_vt(n)
    k_next = load_k(n + 1) if nxt else None

    def issue_scores(c):
        s_ref[1 - buf, c] = jnp.dot(k_next, rhs_ref[c], preferred_element_type=F32)

    lead = 2
    for c in nxt[:lead]:
        issue_scores(c)
    rest = nxt[lead:]
    for j, (c, mask) in enumerate(zip(cur, masks)):
        s = s_ref[buf, c]
        if mask is not None:
            s = jnp.where(mask, s, MASKED)
        m_new = jnp.maximum(ms[c], jnp.max(s, axis=0, keepdims=True))
        alpha = jnp.exp2(ms[c] - m_new)
        p = jnp.exp2(s - m_new).astype(BF16)
        acc_ref[c] = alpha * acc_ref[c] + jnp.dot(vt_blk, p, preferred_element_type=F32)
        ms[c] = m_new
        if j < len(rest):
            issue_scores(rest[j])


def _flash_columns(i, n_sub, n_qgroups, blocks_per_iter, load_k, load_vt, rhs_ref, acc_ref, s_ref):
    assert blocks_per_iter % 2 == 0
    assert n_qgroups % blocks_per_iter == 0
    n_groups = n_sub * n_qgroups
    blk = KV_BLOCK
    everyone = list(range(n_groups))
    init = tuple(jnp.full((1, blk), -jnp.inf, F32) for _ in range(n_groups))
    for c in everyone:
        acc_ref[c] = jnp.zeros(acc_ref.shape[1:], F32)
    k_first = load_k(0)
    for c in everyone:
        s_ref[0, c] = jnp.dot(k_first, rhs_ref[c], preferred_element_type=F32)

    stage = functools.partial(_pipeline_stage, load_k=load_k, load_vt=load_vt,
                              rhs_ref=rhs_ref, acc_ref=acc_ref, s_ref=s_ref)

    def body(it, carry):
        ms = list(carry)
        for j in range(blocks_per_iter):
            stage(blocks_per_iter * it + j, j % 2, everyone, [None] * n_groups, everyone, ms=ms)
        return tuple(ms)

    n_past = n_qgroups * i
    ms = list(lax.fori_loop(0, (n_qgroups // blocks_per_iter) * i, body, init))
    row = lax.broadcasted_iota(jnp.int32, (blk, blk), 0)
    col = lax.broadcasted_iota(jnp.int32, (blk, blk), 1)
    causal = row <= col
    for t in range(n_qgroups):
        cur = [c for c in everyone if c % n_qgroups >= t]
        nxt = [c for c in everyone if c % n_qgroups >= t + 1]
        masks = [causal if c % n_qgroups == t else None for c in cur]
        stage(n_past + t, t % 2, cur, masks, nxt, ms=ms)


def _normalised(acc):
    return acc[0:HEAD_DIM] / acc[HEAD_DIM:HEAD_DIM + 1]


BF16_ROW_TILE = 16


def _cast_specs(weights, grid):
    n_steps = grid[0] * grid[1] * grid[2]
    in_specs, out_specs, out_shapes = [], [], []
    for w in weights:
        rows, cols = w.shape
        share = 1
        while rows * share % (n_steps * BF16_ROW_TILE):
            share *= 2
            assert share <= n_steps, (rows, n_steps)
        spec = pl.BlockSpec((rows * share // n_steps, cols),
                            lambda b, h, i, s=share: (((b * grid[1] + h) * grid[2] + i) // s, 0))
        in_specs.append(spec)
        out_specs.append(spec)
        out_shapes.append(jax.ShapeDtypeStruct(w.shape, BF16))
    return in_specs, out_specs, out_shapes


def _cast_blocks(src_refs, dst_refs):
    for src, dst in zip(src_refs, dst_refs):
        dst[...] = src[...].astype(BF16)


_MOBA_QGROUPS = 4


def _split_bf16(a):
    hi = a.astype(BF16)
    lo = (a - hi.astype(F32)).astype(BF16)
    return hi, lo


def _moba_kernel(*refs, n_blocks, n_cast):
    q_hi_ref, q_lo_ref, k_ref, vt_ref, ksum_ref = refs[:5]
    o_ref = refs[5 + n_cast]
    kaug_sc, rhs_sc, acc_sc, s_sc = refs[6 + 2 * n_cast:]
    _cast_blocks(refs[5:5 + n_cast], refs[6 + n_cast:6 + 2 * n_cast])
    i = pl.program_id(2)
    blk = MOBA_BLOCK
    d = HEAD_DIM
    seq = n_blocks * blk

    @pl.when(i == 0)
    def _():
        kaug_sc[:, 0:d] = k_ref[...]
        row_blk = jnp.right_shift(lax.broadcasted_iota(jnp.int32, (seq, LANES), 0), blk.bit_length() - 1)
        lane = lax.broadcasted_iota(jnp.int32, (seq, LANES), 1)
        kaug_sc[:, d:2 * d] = jnp.where(row_blk == lane, 1.0, 0.0).astype(BF16)

    kmean = jnp.sum(ksum_ref[...].reshape(n_blocks, SUBLANES, d), axis=1) * (1.0 / blk)
    m_hi, m_lo = _split_bf16(kmean)
    row = lax.broadcasted_iota(jnp.int32, (n_blocks, blk), 0).astype(F32)

    for g in range(_MOBA_QGROUPS):
        q_hi, q_lo = q_hi_ref[0, g], q_lo_ref[0, g]
        q_blk = (_MOBA_QGROUPS * i + g).astype(F32)
        gate = (jnp.dot(m_hi, q_hi, preferred_element_type=F32)
                + jnp.dot(m_lo, q_hi, preferred_element_type=F32)
                + jnp.dot(m_hi, q_lo, preferred_element_type=F32))
        avail = row < q_blk
        keep = row == q_blk
        val = jnp.where(avail, gate, -jnp.inf)
        for _ in range(MOBA_TOPK):
            best = jnp.max(val, axis=0, keepdims=True)
            cand = (val == best) & avail
            first = jnp.min(jnp.where(cand, row, float(n_blocks)), axis=0, keepdims=True)
            pick = row == first
            keep = keep | pick
            avail = avail & jnp.logical_not(pick)
            val = jnp.where(pick, -jnp.inf, val)
        bias = jnp.where(keep, 0.0, MASKED)
        bias = jnp.concatenate([bias, jnp.zeros((LANES - n_blocks, blk), F32)], axis=0)
        rhs_sc[g] = jnp.concatenate([q_hi, bias.astype(BF16)], axis=0)

    def load_k(n):
        return kaug_sc[pl.ds(pl.multiple_of(n * blk, blk), blk), :]

    def load_vt(n):
        return vt_ref[0, n]

    _flash_columns(i, 1, _MOBA_QGROUPS, 4, load_k, load_vt, rhs_sc, acc_sc, s_sc)
    for g in range(_MOBA_QGROUPS):
        o_ref[g * blk:(g + 1) * blk, :] = _normalised(acc_sc[g]).astype(BF16).T


def _moba(qa_hi, qa_lo, ka, va_t, ksum, batch, seq, weights_to_cast):
    n_blocks = seq // MOBA_BLOCK
    d = HEAD_DIM
    tq = _MOBA_QGROUPS * MOBA_BLOCK
    n_tiles = seq // tq
    grid = (batch, HEADS, n_tiles)
    cast_in, cast_out, cast_shapes = _cast_specs(weights_to_cast, grid)
    q_spec = pl.BlockSpec((1, _MOBA_QGROUPS, d, MOBA_BLOCK), lambda b, h, i: (h, b * n_tiles + i, 0, 0))
    return pl.pallas_call(
        functools.partial(_moba_kernel, n_blocks=n_blocks, n_cast=len(weights_to_cast)),
        grid=grid,
        in_specs=[q_spec, q_spec,
                  pl.BlockSpec((seq, d), lambda b, h, i: (b, h)),
                  pl.BlockSpec((1, n_blocks, V_ROWS, MOBA_BLOCK), lambda b, h, i: (h, b, 0, 0)),
                  pl.BlockSpec((n_blocks * SUBLANES, d), lambda b, h, i: (b, h))] + cast_in,
        out_specs=[pl.BlockSpec((tq, d), lambda b, h, i: (b * n_tiles + i, h))] + cast_out,
        out_shape=[jax.ShapeDtypeStruct(ka.shape, BF16)] + cast_shapes,
        scratch_shapes=[pltpu.VMEM((seq, 2 * d), BF16),
                        pltpu.VMEM((_MOBA_QGROUPS, 2 * d, MOBA_BLOCK), BF16),
                        pltpu.VMEM((_MOBA_QGROUPS, V_ROWS, MOBA_BLOCK), F32),
                        pltpu.VMEM((2, _MOBA_QGROUPS, KV_BLOCK, MOBA_BLOCK), F32)],
        compiler_params=_params(("arbitrary", "arbitrary", "arbitrary")),
        name="moba",
    )(qa_hi, qa_lo, ka, va_t, ksum, *weights_to_cast)


_DIFF_QGROUPS = 4


def _diff_kernel(*refs, lam_init, n_cast):
    q_ref, k_ref, vt_ref, lam_ref, subln_ref = refs[:5]
    o_ref = refs[5 + n_cast]
    rhs_sc, acc_sc, s_sc = refs[6 + 2 * n_cast:]
    _cast_blocks(refs[5:5 + n_cast], refs[6 + n_cast:6 + 2 * n_cast])
    i = pl.program_id(2)
    blk = KV_BLOCK
    d = HEAD_DIM
    nq = _DIFF_QGROUPS

    feat = lax.broadcasted_iota(jnp.int32, (d, blk), 0)
    for sub in range(2):
        own = (feat < DIFF_QK_DIM) if sub == 0 else (feat >= DIFF_QK_DIM)
        for g in range(nq):
            q_t = q_ref[0, g]
            rhs_sc[sub * nq + g] = jnp.where(own, q_t, jnp.zeros_like(q_t))

    def load_k(n):
        return k_ref[pl.ds(pl.multiple_of(n * blk, blk), blk), :]

    def load_vt(n):
        return vt_ref[0, n]

    _flash_columns(i, 2, nq, 4, load_k, load_vt, rhs_sc, acc_sc, s_sc)

    lq = lam_ref[...]
    lam = (jnp.exp(jnp.sum(lq[0:1, :] * lq[1:2, :], axis=1, keepdims=True))
           - jnp.exp(jnp.sum(lq[2:3, :] * lq[3:4, :], axis=1, keepdims=True)) + lam_init)
    for g in range(nq):
        o = _normalised(acc_sc[g]) - lam * _normalised(acc_sc[nq + g])
        o = o * lax.rsqrt(jnp.mean(o * o, axis=0, keepdims=True) + RMS_EPS)
        o_ref[g * blk:(g + 1) * blk, :] = (o.T * subln_ref[...] * (1.0 - lam_init)).astype(BF16)


def _diff(qb_t, kb, vb_t, lambda_qk, subln_w, batch, seq, lam_init, weights_to_cast):
    d = HEAD_DIM
    n_blocks = seq // KV_BLOCK
    tq = _DIFF_QGROUPS * KV_BLOCK
    n_tiles = seq // tq
    grid = (batch, HEADS, n_tiles)
    cast_in, cast_out, cast_shapes = _cast_specs(weights_to_cast, grid)
    return pl.pallas_call(
        functools.partial(_diff_kernel, lam_init=lam_init, n_cast=len(weights_to_cast)),
        grid=grid,
        in_specs=[pl.BlockSpec((1, _DIFF_QGROUPS, d, KV_BLOCK), lambda b, h, i: (h, b * n_tiles + i, 0, 0)),
                  pl.BlockSpec((seq, d), lambda b, h, i: (b, h)),
                  pl.BlockSpec((1, n_blocks, V_ROWS, KV_BLOCK), lambda b, h, i: (h, b, 0, 0)),
                  pl.BlockSpec(lambda_qk.shape, lambda b, h, i: (0, 0)),
                  pl.BlockSpec((1, d), lambda b, h, i: (0, 0))] + cast_in,
        out_specs=[pl.BlockSpec((tq, d), lambda b, h, i: (b * n_tiles + i, h))] + cast_out,
        out_shape=[jax.ShapeDtypeStruct(kb.shape, BF16)] + cast_shapes,
        scratch_shapes=[pltpu.VMEM((2 * _DIFF_QGROUPS, d, KV_BLOCK), BF16),
                        pltpu.VMEM((2 * _DIFF_QGROUPS, V_ROWS, KV_BLOCK), F32),
                        pltpu.VMEM((2, 2 * _DIFF_QGROUPS, KV_BLOCK, KV_BLOCK), F32)],
        compiler_params=_params(("arbitrary", "arbitrary", "arbitrary")),
        name="diffattn",
    )(qb_t, kb, vb_t, lambda_qk, subln_w.reshape(1, d), *weights_to_cast)


_ROW_TM = 512
_FFN_TH = 512


def _merge_kernel(ya_ref, yb_ref, ga_ref, gb_ref, wa_ref, wb_ref, o_ref):
    a = jnp.dot(ya_ref[...], wa_ref[...], preferred_element_type=F32)
    b = jnp.dot(yb_ref[...], wb_ref[...], preferred_element_type=F32)
    o_ref[...] = (_sigmoid(ga_ref[...].astype(F32)) * a + _sigmoid(gb_ref[...].astype(F32)) * b).astype(BF16)


def _merge(ya, yb, ga, gb, wa, wb):
    t = ya.shape[0]
    tm = _ROW_TM
    row = lambda w: pl.BlockSpec((tm, w), lambda i: (i, 0))
    full = lambda a: pl.BlockSpec(a.shape, lambda i: (0, 0))
    return pl.pallas_call(
        _merge_kernel,
        grid=(t // tm,),
        in_specs=[row(WIDTH), row(WIDTH), row(D_MODEL), row(D_MODEL), full(wa), full(wb)],
        out_specs=row(D_MODEL),
        out_shape=jax.ShapeDtypeStruct((t, D_MODEL), BF16),
        compiler_params=_params(("parallel",)),
        name="merge",
    )(ya, yb, ga, gb, wa, wb)


def _layer_norm(y, g, b):
    mu = jnp.mean(y, axis=1, keepdims=True)
    c = y - mu
    var = jnp.mean(c * c, axis=1, keepdims=True)
    return c * lax.rsqrt(var + LN_EPS) * g + b


def _row_halves(ref):
    half = ref.shape[0] // 2
    return slice(0, half), slice(half, 2 * half)


def _out_ln_kernel(m_ref, w_ref, x_ref, g_ref, b_ref, h_ref, hb_ref, *, alpha):
    halves = _row_halves(m_ref)
    ys = [alpha * x_ref[rows, :] + jnp.dot(m_ref[rows, :], w_ref[...], preferred_element_type=F32)
          for rows in halves]
    for rows, y in zip(halves, ys):
        h = _layer_norm(y, g_ref[...], b_ref[...])
        h_ref[rows, :] = h
        hb_ref[rows, :] = h.astype(BF16)


def _out_ln(m, w_out, x2, g, b, alpha):
    t = m.shape[0]
    tm = _ROW_TM
    row = pl.BlockSpec((tm, D_MODEL), lambda i: (i, 0))
    vec = pl.BlockSpec((1, D_MODEL), lambda i: (0, 0))
    return pl.pallas_call(
        functools.partial(_out_ln_kernel, alpha=alpha),
        grid=(t // tm,),
        in_specs=[row, pl.BlockSpec(w_out.shape, lambda i: (0, 0)), row, vec, vec],
        out_specs=(row, row),
        out_shape=(jax.ShapeDtypeStruct((t, D_MODEL), F32), jax.ShapeDtypeStruct((t, D_MODEL), BF16)),
        compiler_params=_params(("parallel",)),
        name="out_ln",
    )(m, w_out, x2, g.reshape(1, D_MODEL), b.reshape(1, D_MODEL))


def _ffn_kernel(hb_ref, wg_ref, wu_ref, wo_ref, h_ref, g_ref, b_ref, o_ref, *, alpha):
    j = pl.program_id(1)

    @pl.when(j == 0)
    def _():
        o_ref[...] = alpha * h_ref[...]

    hb = hb_ref[...]
    gate = jnp.dot(hb, wg_ref[...], preferred_element_type=F32)
    up = jnp.dot(hb, wu_ref[...], preferred_element_type=F32)
    act = (gate * _sigmoid(gate) * up).astype(BF16)
    o_ref[...] += jnp.dot(act, wo_ref[...], preferred_element_type=F32)

    @pl.when(j == pl.num_programs(1) - 1)
    def _():
        o_ref[...] = _layer_norm(o_ref[...], g_ref[...], b_ref[...])


def _ffn_ln(hb, h, w_ffn_in, w_ffn_out, g, b, alpha):
    t = h.shape[0]
    tm, th = _ROW_TM, _FFN_TH
    nj = FFN_HIDDEN // th
    row = pl.BlockSpec((tm, D_MODEL), lambda i, j: (i, 0))
    vec = pl.BlockSpec((1, D_MODEL), lambda i, j: (0, 0))
    return pl.pallas_call(
        functools.partial(_ffn_kernel, alpha=alpha),
        grid=(t // tm, nj),
        in_specs=[row,
                  pl.BlockSpec((D_MODEL, th), lambda i, j: (0, j)),
                  pl.BlockSpec((D_MODEL, th), lambda i, j: (0, nj + j)),
                  pl.BlockSpec((th, D_MODEL), lambda i, j: (j, 0)),
                  row, vec, vec],
        out_specs=row,
        out_shape=jax.ShapeDtypeStruct((t, D_MODEL), F32),
        compiler_params=_params(("parallel", "arbitrary")),
        name="ffn_ln",
    )(hb, w_ffn_in, w_ffn_in, w_ffn_out, h, g.reshape(1, D_MODEL), b.reshape(1, D_MODEL))


def kernel(x, w_in, lambda_qk, diff_subln_w, w_branch_a, w_branch_b, w_out,
           ln1_g, ln1_b, w_ffn_in, w_ffn_out, ln2_g, ln2_b):
    batch, seq, _ = x.shape
    depth = w_in.shape[0]
    alpha = (2.0 * depth) ** 0.25
    tables = _rope_tables(seq)
    h = x.reshape(batch * seq, D_MODEL)
    for l in range(depth):
        lam_init = 0.8 - 0.6 * math.exp(-0.3 * l)
        qa_hi, qa_lo, ka, ksum, va_t, qb_t, kb, vb_t, ga, gb = _in_proj(h, w_in[l], tables, seq)
        ya, wa, wb, wo, w_down = _moba(qa_hi, qa_lo, ka, va_t, ksum, batch, seq,
                                       (w_branch_a[l], w_branch_b[l], w_out[l], w_ffn_out[l]))
        yb, w_up = _diff(qb_t, kb, vb_t, lambda_qk[l], diff_subln_w[l], batch, seq, lam_init, (w_ffn_in[l],))
        m = _merge(ya, yb, ga, gb, wa, wb)
        h, hb = _out_ln(m, wo, h, ln1_g[l], ln1_b[l], alpha)
        h = _ffn_ln(hb, h, w_up, w_down, ln2_g[l], ln2_b[l], alpha)
    return h.reshape(batch, seq, D_MODEL)
```

```python
import functools
import math

import jax
import jax.numpy as jnp
import numpy as np
from jax import lax
from jax.experimental import pallas as pl
from jax.experimental.pallas import tpu as pltpu

D_MODEL = 2048
HEADS = 8
HEAD_DIM = 128
WIDTH = HEADS * HEAD_DIM
MOBA_BLOCK = 256
MOBA_TOPK = 3
DIFF_QK_DIM = 64
ROPE_THETA = 10000.0
FFN_HIDDEN = 5632
LN_EPS = 1e-5
RMS_EPS = 1e-5

LANES = 128
SUBLANES = 8
VMEM_LIMIT = 56 * 1024 * 1024
MASKED = -1e30
LOG2E = math.log2(math.e)
KV_BLOCK = 256
V_ROWS = HEAD_DIM + 16

BF16 = jnp.bfloat16
F32 = jnp.float32


def _params(sem):
    return pltpu.CompilerParams(dimension_semantics=sem, vmem_limit_bytes=VMEM_LIMIT)


_N_ROPE_TABLES = 5
_ROPE_STEP = 64


def _rope_tables(seq):
    step = _ROPE_STEP
    assert seq % step == 0
    half_a, half_b = HEAD_DIM // 2, DIFF_QK_DIM // 2
    lane = np.arange(_N_ROPE_TABLES * LANES)
    table, l = lane // LANES, lane % LANES
    freq = np.where(table < 2, l % half_a, 2 * (l % half_b))
    upper = np.where(table < 2, l % HEAD_DIM >= half_a, l % DIFF_QK_DIM >= half_b)
    use_sin = (table == 1) | (table >= 3)
    coef = np.select([table == 1, table == 3, table == 4],
                     [np.where(upper, 1.0, -1.0),
                      np.where(upper, 0.0, -1.0),
                      np.where(upper, 1.0, 0.0)],
                     default=1.0).astype(np.float32)
    inv = ROPE_THETA ** (-jnp.arange(half_a, dtype=F32) * 2.0 / HEAD_DIM)
    inv_lane = inv[freq][None, :]
    coarse = jnp.arange(seq // step, dtype=F32)[:, None] * float(step) * inv_lane
    fine = jnp.arange(step, dtype=F32)[:, None] * inv_lane
    cc, sc = jnp.cos(coarse), jnp.sin(coarse)
    a_coarse = jnp.where(use_sin, sc, cc) * coef
    b_coarse = jnp.where(use_sin, cc, -sc) * coef
    return a_coarse, b_coarse, jnp.cos(fine), jnp.sin(fine)


_IN_TM = 512
_IN_COLS = 6 * HEAD_DIM + 2 * 2 * HEAD_DIM


def _sigmoid(t):
    return 0.5 * jnp.tanh(0.5 * t) + 0.5


def _in_proj_kernel(x_ref, wqa, wka, wva, wqb, wkb, wvb, wga, wgb,
                    rope_a_ref, rope_b_ref, rope_cos_ref, rope_sin_ref,
                    qa_hi_ref, qa_lo_ref, ka_ref, ksum_ref, va_ref, qb_ref, kb_ref, vb_ref, ga_ref, gb_ref,
                    w_sc):
    def rope_table(j):
        lanes = slice(j * LANES, (j + 1) * LANES)
        fine_cos, fine_sin = rope_cos_ref[:, lanes], rope_sin_ref[:, lanes]
        return jnp.concatenate([rope_a_ref[r:r + 1, lanes] * fine_cos + rope_b_ref[r:r + 1, lanes] * fine_sin
                                for r in range(_IN_TM // _ROPE_STEP)], axis=0)

    @pl.when(pl.program_id(1) == 0)
    def _():
        off = 0
        for w in (wqa, wka, wva, wqb, wkb, wvb, wga, wgb):
            n = w.shape[1]
            w_sc[:, off:off + n] = w[...].astype(BF16)
            off += n

    z = jnp.dot(x_ref[...].astype(BF16), w_sc[...], preferred_element_type=F32)
    d = HEAD_DIM
    blk = KV_BLOCK
    zqa, zka, zva, zqb, zkb, zvb = (z[:, j * d:(j + 1) * d] for j in range(6))
    zga, zgb = z[:, 6 * d:8 * d], z[:, 8 * d:10 * d]

    cos_a, sin_a, cos_b, sin_b_lo, sin_b_hi = (rope_table(j) for j in range(_N_ROPE_TABLES))

    def rope_a(t):
        return t * cos_a + pltpu.roll(t, d // 2, 1) * sin_a

    def rope_b(t):
        return (t * cos_b + pltpu.roll(t, d - DIFF_QK_DIM // 2, 1) * sin_b_lo
                + pltpu.roll(t, DIFF_QK_DIM // 2, 1) * sin_b_hi)

    def store_transposed(ref, t, dtype):
        for j in range(_IN_TM // blk):
            ref[0, j, 0:d] = t[j * blk:(j + 1) * blk, :].T.astype(dtype)
            if ref.shape[2] > d:
                ref[0, j, d:] = jnp.ones((ref.shape[2] - d, blk), dtype)

    qa = rope_a(zqa)
    store_transposed(qa_hi_ref, qa, BF16)
    store_transposed(qa_lo_ref, qa - qa.astype(BF16).astype(F32), BF16)
    ka = rope_a(zka) * (d ** -0.5 * LOG2E)
    ka_ref[...] = ka.astype(BF16)
    for j in range(_IN_TM // MOBA_BLOCK):
        part = ka[j * MOBA_BLOCK:(j + 1) * MOBA_BLOCK].reshape(MOBA_BLOCK // SUBLANES, SUBLANES, d)
        ksum_ref[j * SUBLANES:(j + 1) * SUBLANES, :] = jnp.sum(part, axis=0)
    store_transposed(qb_ref, rope_b(zqb) * (DIFF_QK_DIM ** -0.5 * LOG2E), BF16)
    kb_ref[...] = rope_b(zkb).astype(BF16)
    store_transposed(va_ref, zva, BF16)
    store_transposed(vb_ref, zvb, BF16)
    ga_ref[...] = zga.astype(BF16)
    gb_ref[...] = zgb.astype(BF16)


def _in_proj(x2, w_in, tables, seq):
    t = x2.shape[0]
    tm = _IN_TM
    n_seq_tiles = seq // tm
    grid = (HEADS, t // tm)
    d = HEAD_DIM

    def wspec(group):
        return pl.BlockSpec((D_MODEL, d), lambda h, i, g=group: (0, g * HEADS + h))

    def gspec(base):
        return pl.BlockSpec((D_MODEL, 2 * d), lambda h, i, b=base: (0, b + h))

    rope_lanes = _N_ROPE_TABLES * LANES
    coarse_spec = pl.BlockSpec((tm // _ROPE_STEP, rope_lanes), lambda h, i: (i % n_seq_tiles, 0))
    fine_spec = pl.BlockSpec((_ROPE_STEP, rope_lanes), lambda h, i: (0, 0))
    ospec = pl.BlockSpec((tm, d), lambda h, i: (i, h))
    tr_spec = pl.BlockSpec((1, tm // KV_BLOCK, d, KV_BLOCK), lambda h, i: (h, i, 0, 0))
    gate_ospec = pl.BlockSpec((tm, 2 * d), lambda h, i: (i, h))
    ksum_rows = tm // MOBA_BLOCK * SUBLANES
    tr_shape = (HEADS, t // KV_BLOCK, d, KV_BLOCK)
    v_spec = pl.BlockSpec((1, tm // KV_BLOCK, V_ROWS, KV_BLOCK), lambda h, i: (h, i, 0, 0))
    v_shape = (HEADS, t // KV_BLOCK, V_ROWS, KV_BLOCK)
    out_shape = (
        jax.ShapeDtypeStruct(tr_shape, BF16),
        jax.ShapeDtypeStruct(tr_shape, BF16),
        jax.ShapeDtypeStruct((t, WIDTH), BF16),
        jax.ShapeDtypeStruct((t // MOBA_BLOCK * SUBLANES, WIDTH), F32),
        jax.ShapeDtypeStruct(v_shape, BF16),
        jax.ShapeDtypeStruct(tr_shape, BF16),
        jax.ShapeDtypeStruct((t, WIDTH), BF16),
        jax.ShapeDtypeStruct(v_shape, BF16),
        jax.ShapeDtypeStruct((t, D_MODEL), BF16),
        jax.ShapeDtypeStruct((t, D_MODEL), BF16),
    )
    out_specs = (tr_spec, tr_spec, ospec, pl.BlockSpec((ksum_rows, d), lambda h, i: (i, h)),
                 v_spec, tr_spec, ospec, v_spec, gate_ospec, gate_ospec)
    gate_a_base = 6 * WIDTH // (2 * d)
    gate_b_base = gate_a_base + D_MODEL // (2 * d)
    in_specs = [pl.BlockSpec((tm, D_MODEL), lambda h, i: (i, 0))]
    in_specs += [wspec(g) for g in range(6)]
    in_specs += [gspec(gate_a_base), gspec(gate_b_base)]
    in_specs += [coarse_spec, coarse_spec, fine_spec, fine_spec]
    return pl.pallas_call(
        _in_proj_kernel,
        grid=grid,
        in_specs=in_specs,
        out_specs=out_specs,
        out_shape=out_shape,
        scratch_shapes=[pltpu.VMEM((D_MODEL, _IN_COLS), BF16)],
        compiler_params=_params(("arbitrary", "arbitrary")),
        name="in_proj",
    )(x2, *([w_in] * 8), *tables)


def _pipeline_stage(n, buf, cur, masks, nxt, load_k, load_vt, rhs_ref, acc_ref, s_ref, ms):
    vt_blk = load_vt(n)
    k_next = load_k(n + 1) if nxt else None

    def issue_scores(c):
        s_ref[1 - buf, c] = jnp.dot(k_next, rhs_ref[c], preferred_element_type=F32)

    lead = 2
    for c in nxt[:lead]:
        issue_scores(c)
    rest = nxt[lead:]
    for j, (c, mask) in enumerate(zip(cur, masks)):
        s = s_ref[buf, c]
        if mask is not None:
            s = jnp.where(mask, s, MASKED)
        m_new = jnp.maximum(ms[c], jnp.max(s, axis=0, keepdims=True))
        alpha = jnp.exp2(ms[c] - m_new)
        p = jnp.exp2(s - m_new).astype(BF16)
        acc_ref[c] = alpha * acc_ref[c] + jnp.dot(vt_blk, p, preferred_element_type=F32)
        ms[c] = m_new
        if j < len(rest):
            issue_scores(rest[j])


def _flash_columns(i, n_sub, n_qgroups, blocks_per_iter, load_k, load_vt, rhs_ref, acc_ref, s_ref):
    assert blocks_per_iter % 2 == 0
    assert n_qgroups % blocks_per_iter == 0
    n_groups = n_sub * n_qgroups
    blk = KV_BLOCK
    everyone = list(range(n_groups))
    init = tuple(jnp.full((1, blk), -jnp.inf, F32) for _ in range(n_groups))
    for c in everyone:
        acc_ref[c] = jnp.zeros(acc_ref.shape[1:], F32)
    k_first = load_k(0)
    for c in everyone:
        s_ref[0, c] = jnp.dot(k_first, rhs_ref[c], preferred_element_type=F32)

    stage = functools.partial(_pipeline_stage, load_k=load_k, load_vt=load_vt,
                              rhs_ref=rhs_ref, acc_ref=acc_ref, s_ref=s_ref)

    def body(it, carry):
        ms = list(carry)
        for j in range(blocks_per_iter):
            stage(blocks_per_iter * it + j, j % 2, everyone, [None] * n_groups, everyone, ms=ms)
        return tuple(ms)

    n_past = n_qgroups * i
    ms = list(lax.fori_loop(0, (n_qgroups // blocks_per_iter) * i, body, init))
    row = lax.broadcasted_iota(jnp.int32, (blk, blk), 0)
    col = lax.broadcasted_iota(jnp.int32, (blk, blk), 1)
    causal = row <= col
    for t in range(n_qgroups):
        cur = [c for c in everyone if c % n_qgroups >= t]
        nxt = [c for c in everyone if c % n_qgroups >= t + 1]
        masks = [causal if c % n_qgroups == t else None for c in cur]
        stage(n_past + t, t % 2, cur, masks, nxt, ms=ms)


def _normalised(acc):
    return acc[0:HEAD_DIM] / acc[HEAD_DIM:HEAD_DIM + 1]


BF16_ROW_TILE = 16


def _cast_specs(weights, grid):
    n_steps = grid[0] * grid[1] * grid[2]
    in_specs, out_specs, out_shapes = [], [], []
    for w in weights:
        rows, cols = w.shape
        share = 1
        while rows * share % (n_steps * BF16_ROW_TILE):
            share *= 2
            assert share <= n_steps, (rows, n_steps)
        spec = pl.BlockSpec((rows * share // n_steps, cols),
                            lambda b, h, i, s=share: (((b * grid[1] + h) * grid[2] + i) // s, 0))
        in_specs.append(spec)
        out_specs.append(spec)
        out_shapes.append(jax.ShapeDtypeStruct(w.shape, BF16))
    return in_specs, out_specs, out_shapes


def _cast_blocks(src_refs, dst_refs):
    for src, dst in zip(src_refs, dst_refs):
        dst[...] = src[...].astype(BF16)


_MOBA_QGROUPS = 4


def _split_bf16(a):
    hi = a.astype(BF16)
    lo = (a - hi.astype(F32)).astype(BF16)
    return hi, lo


def _moba_kernel(*refs, n_blocks, n_cast):
    q_hi_ref, q_lo_ref, k_ref, vt_ref, ksum_ref = refs[:5]
    o_ref = refs[5 + n_cast]
    kaug_sc, rhs_sc, acc_sc, s_sc = refs[6 + 2 * n_cast:]
    _cast_blocks(refs[5:5 + n_cast], refs[6 + n_cast:6 + 2 * n_cast])
    i = pl.program_id(2)
    blk = MOBA_BLOCK
    d = HEAD_DIM
    seq = n_blocks * blk

    @pl.when(i == 0)
    def _():
        kaug_sc[:, 0:d] = k_ref[...]
        row_blk = jnp.right_shift(lax.broadcasted_iota(jnp.int32, (seq, LANES), 0), blk.bit_length() - 1)
        lane = lax.broadcasted_iota(jnp.int32, (seq, LANES), 1)
        kaug_sc[:, d:2 * d] = jnp.where(row_blk == lane, 1.0, 0.0).astype(BF16)

    kmean = jnp.sum(ksum_ref[...].reshape(n_blocks, SUBLANES, d), axis=1) * (1.0 / blk)
    m_hi, m_lo = _split_bf16(kmean)
    row = lax.broadcasted_iota(jnp.int32, (n_blocks, blk), 0).astype(F32)

    for g in range(_MOBA_QGROUPS):
        q_hi, q_lo = q_hi_ref[0, g], q_lo_ref[0, g]
        q_blk = (_MOBA_QGROUPS * i + g).astype(F32)
        gate = (jnp.dot(m_hi, q_hi, preferred_element_type=F32)
                + jnp.dot(m_lo, q_hi, preferred_element_type=F32)
                + jnp.dot(m_hi, q_lo, preferred_element_type=F32))
        avail = row < q_blk
        keep = row == q_blk
        val = jnp.where(avail, gate, -jnp.inf)
        for _ in range(MOBA_TOPK):
            best = jnp.max(val, axis=0, keepdims=True)
            cand = (val == best) & avail
            first = jnp.min(jnp.where(cand, row, float(n_blocks)), axis=0, keepdims=True)
            pick = row == first
            keep = keep | pick
            avail = avail & jnp.logical_not(pick)
            val = jnp.where(pick, -jnp.inf, val)
        bias = jnp.where(keep, 0.0, MASKED)
        bias = jnp.concatenate([bias, jnp.zeros((LANES - n_blocks, blk), F32)], axis=0)
        rhs_sc[g] = jnp.concatenate([q_hi, bias.astype(BF16)], axis=0)

    def load_k(n):
        return kaug_sc[pl.ds(pl.multiple_of(n * blk, blk), blk), :]

    def load_vt(n):
        return vt_ref[0, n]

    _flash_columns(i, 1, _MOBA_QGROUPS, 4, load_k, load_vt, rhs_sc, acc_sc, s_sc)
    for g in range(_MOBA_QGROUPS):
        o_ref[g * blk:(g + 1) * blk, :] = _normalised(acc_sc[g]).astype(BF16).T


def _moba(qa_hi, qa_lo, ka, va_t, ksum, batch, seq, weights_to_cast):
    n_blocks = seq // MOBA_BLOCK
    d = HEAD_DIM
    tq = _MOBA_QGROUPS * MOBA_BLOCK
    n_tiles = seq // tq
    grid = (batch, HEADS, n_tiles)
    cast_in, cast_out, cast_shapes = _cast_specs(weights_to_cast, grid)
    q_spec = pl.BlockSpec((1, _MOBA_QGROUPS, d, MOBA_BLOCK), lambda b, h, i: (h, b * n_tiles + i, 0, 0))
    return pl.pallas_call(
        functools.partial(_moba_kernel, n_blocks=n_blocks, n_cast=len(weights_to_cast)),
        grid=grid,
        in_specs=[q_spec, q_spec,
                  pl.BlockSpec((seq, d), lambda b, h, i: (b, h)),
                  pl.BlockSpec((1, n_blocks, V_ROWS, MOBA_BLOCK), lambda b, h, i: (h, b, 0, 0)),
                  pl.BlockSpec((n_blocks * SUBLANES, d), lambda b, h, i: (b, h))] + cast_in,
        out_specs=[pl.BlockSpec((tq, d), lambda b, h, i: (b * n_tiles + i, h))] + cast_out,
        out_shape=[jax.ShapeDtypeStruct(ka.shape, BF16)] + cast_shapes,
        scratch_shapes=[pltpu.VMEM((seq, 2 * d), BF16),
                        pltpu.VMEM((_MOBA_QGROUPS, 2 * d, MOBA_BLOCK), BF16),
                        pltpu.VMEM((_MOBA_QGROUPS, V_ROWS, MOBA_BLOCK), F32),
                        pltpu.VMEM((2, _MOBA_QGROUPS, KV_BLOCK, MOBA_BLOCK), F32)],
        compiler_params=_params(("arbitrary", "arbitrary", "arbitrary")),
        name="moba",
    )(qa_hi, qa_lo, ka, va_t, ksum, *weights_to_cast)


_DIFF_QGROUPS = 4


def _diff_kernel(*refs, lam_init, n_cast):
    q_ref, k_ref, vt_ref, lam_ref, subln_ref = refs[:5]
    o_ref = refs[5 + n_cast]
    rhs_sc, acc_sc, s_sc = refs[6 + 2 * n_cast:]
    _cast_blocks(refs[5:5 + n_cast], refs[6 + n_cast:6 + 2 * n_cast])
    i = pl.program_id(2)
    blk = KV_BLOCK
    d = HEAD_DIM
    nq = _DIFF_QGROUPS

    feat = lax.broadcasted_iota(jnp.int32, (d, blk), 0)
    for sub in range(2):
        own = (feat < DIFF_QK_DIM) if sub == 0 else (feat >= DIFF_QK_DIM)
        for g in range(nq):
            q_t = q_ref[0, g]
            rhs_sc[sub * nq + g] = jnp.where(own, q_t, jnp.zeros_like(q_t))

    def load_k(n):
        return k_ref[pl.ds(pl.multiple_of(n * blk, blk), blk), :]

    def load_vt(n):
        return vt_ref[0, n]

    _flash_columns(i, 2, nq, 4, load_k, load_vt, rhs_sc, acc_sc, s_sc)

    lq = lam_ref[...]
    lam = (jnp.exp(jnp.sum(lq[0:1, :] * lq[1:2, :], axis=1, keepdims=True))
           - jnp.exp(jnp.sum(lq[2:3, :] * lq[3:4, :], axis=1, keepdims=True)) + lam_init)
    for g in range(nq):
        o = _normalised(acc_sc[g]) - lam * _normalised(acc_sc[nq + g])
        o = o * lax.rsqrt(jnp.mean(o * o, axis=0, keepdims=True) + RMS_EPS)
        o_ref[g * blk:(g + 1) * blk, :] = (o.T * subln_ref[...] * (1.0 - lam_init)).astype(BF16)


def _diff(qb_t, kb, vb_t, lambda_qk, subln_w, batch, seq, lam_init, weights_to_cast):
    d = HEAD_DIM
    n_blocks = seq // KV_BLOCK
    tq = _DIFF_QGROUPS * KV_BLOCK
    n_tiles = seq // tq
    grid = (batch, HEADS, n_tiles)
    cast_in, cast_out, cast_shapes = _cast_specs(weights_to_cast, grid)
    return pl.pallas_call(
        functools.partial(_diff_kernel, lam_init=lam_init, n_cast=len(weights_to_cast)),
        grid=grid,
        in_specs=[pl.BlockSpec((1, _DIFF_QGROUPS, d, KV_BLOCK), lambda b, h, i: (h, b * n_tiles + i, 0, 0)),
                  pl.BlockSpec((seq, d), lambda b, h, i: (b, h)),
                  pl.BlockSpec((1, n_blocks, V_ROWS, KV_BLOCK), lambda b, h, i: (h, b, 0, 0)),
                  pl.BlockSpec(lambda_qk.shape, lambda b, h, i: (0, 0)),
                  pl.BlockSpec((1, d), lambda b, h, i: (0, 0))] + cast_in,
        out_specs=[pl.BlockSpec((tq, d), lambda b, h, i: (b * n_tiles + i, h))] + cast_out,
        out_shape=[jax.ShapeDtypeStruct(kb.shape, BF16)] + cast_shapes,
        scratch_shapes=[pltpu.VMEM((2 * _DIFF_QGROUPS, d, KV_BLOCK), BF16),
                        pltpu.VMEM((2 * _DIFF_QGROUPS, V_ROWS, KV_BLOCK), F32),
                        pltpu.VMEM((2, 2 * _DIFF_QGROUPS, KV_BLOCK, KV_BLOCK), F32)],
        compiler_params=_params(("arbitrary", "arbitrary", "arbitrary")),
        name="diffattn",
    )(qb_t, kb, vb_t, lambda_qk, subln_w.reshape(1, d), *weights_to_cast)


_ROW_TM = 512
_FFN_TH = 512


def _merge_kernel(ya_ref, yb_ref, ga_ref, gb_ref, wa_ref, wb_ref, o_ref):
    a = jnp.dot(ya_ref[...], wa_ref[...], preferred_element_type=F32)
    b = jnp.dot(yb_ref[...], wb_ref[...], preferred_element_type=F32)
    o_ref[...] = (_sigmoid(ga_ref[...].astype(F32)) * a + _sigmoid(gb_ref[...].astype(F32)) * b).astype(BF16)


def _merge(ya, yb, ga, gb, wa, wb):
    t = ya.shape[0]
    tm = _ROW_TM
    row = lambda w: pl.BlockSpec((tm, w), lambda i: (i, 0))
    full = lambda a: pl.BlockSpec(a.shape, lambda i: (0, 0))
    return pl.pallas_call(
        _merge_kernel,
        grid=(t // tm,),
        in_specs=[row(WIDTH), row(WIDTH), row(D_MODEL), row(D_MODEL), full(wa), full(wb)],
        out_specs=row(D_MODEL),
        out_shape=jax.ShapeDtypeStruct((t, D_MODEL), BF16),
        compiler_params=_params(("parallel",)),
        name="merge",
    )(ya, yb, ga, gb, wa, wb)


def _layer_norm(y, g, b):
    mu = jnp.mean(y, axis=1, keepdims=True)
    c = y - mu
    var = jnp.mean(c * c, axis=1, keepdims=True)
    return c * lax.rsqrt(var + LN_EPS) * g + b


def _row_halves(ref):
    half = ref.shape[0] // 2
    return slice(0, half), slice(half, 2 * half)


def _out_ln_kernel(m_ref, w_ref, x_ref, g_ref, b_ref, h_ref, hb_ref, *, alpha):
    halves = _row_halves(m_ref)
    ys = [alpha * x_ref[rows, :] + jnp.dot(m_ref[rows, :], w_ref[...], preferred_element_type=F32)
          for rows in halves]
    for rows, y in zip(halves, ys):
        h = _layer_norm(y, g_ref[...], b_ref[...])
        h_ref[rows, :] = h
        hb_ref[rows, :] = h.astype(BF16)


def _out_ln(m, w_out, x2, g, b, alpha):
    t = m.shape[0]
    tm = _ROW_TM
    row = pl.BlockSpec((tm, D_MODEL), lambda i: (i, 0))
    vec = pl.BlockSpec((1, D_MODEL), lambda i: (0, 0))
    return pl.pallas_call(
        functools.partial(_out_ln_kernel, alpha=alpha),
        grid=(t // tm,),
        in_specs=[row, pl.BlockSpec(w_out.shape, lambda i: (0, 0)), row, vec, vec],
        out_specs=(row, row),
        out_shape=(jax.ShapeDtypeStruct((t, D_MODEL), F32), jax.ShapeDtypeStruct((t, D_MODEL), BF16)),
        compiler_params=_params(("parallel",)),
        name="out_ln",
    )(m, w_out, x2, g.reshape(1, D_MODEL), b.reshape(1, D_MODEL))


def _ffn_kernel(hb_ref, wg_ref, wu_ref, wo_ref, h_ref, g_ref, b_ref, o_ref, *, alpha):
    j = pl.program_id(1)

    @pl.when(j == 0)
    def _():
        o_ref[...] = alpha * h_ref[...]

    hb = hb_ref[...]
    gate = jnp.dot(hb, wg_ref[...], preferred_element_type=F32)
    up = jnp.dot(hb, wu_ref[...], preferred_element_type=F32)
    act = (gate * _sigmoid(gate) * up).astype(BF16)
    o_ref[...] += jnp.dot(act, wo_ref[...], preferred_element_type=F32)

    @pl.when(j == pl.num_programs(1) - 1)
    def _():
        o_ref[...] = _layer_norm(o_ref[...], g_ref[...], b_ref[...])


def _ffn_ln(hb, h, w_ffn_in, w_ffn_out, g, b, alpha):
    t = h.shape[0]
    tm, th = _ROW_TM, _FFN_TH
    nj = FFN_HIDDEN // th
    row = pl.BlockSpec((tm, D_MODEL), lambda i, j: (i, 0))
    vec = pl.BlockSpec((1, D_MODEL), lambda i, j: (0, 0))
    return pl.pallas_call(
        functools.partial(_ffn_kernel, alpha=alpha),
        grid=(t // tm, nj),
        in_specs=[row,
                  pl.BlockSpec((D_MODEL, th), lambda i, j: (0, j)),
                  pl.BlockSpec((D_MODEL, th), lambda i, j: (0, nj + j)),
                  pl.BlockSpec((th, D_MODEL), lambda i, j: (j, 0)),
                  row, vec, vec],
        out_specs=row,
        out_shape=jax.ShapeDtypeStruct((t, D_MODEL), F32),
        compiler_params=_params(("parallel", "arbitrary")),
        name="ffn_ln",
    )(hb, w_ffn_in, w_ffn_in, w_ffn_out, h, g.reshape(1, D_MODEL), b.reshape(1, D_MODEL))


def kernel(x, w_in, lambda_qk, diff_subln_w, w_branch_a, w_branch_b, w_out,
           ln1_g, ln1_b, w_ffn_in, w_ffn_out, ln2_g, ln2_b):
    batch, seq, _ = x.shape
    depth = w_in.shape[0]
    alpha = (2.0 * depth) ** 0.25
    tables = _rope_tables(seq)
    h = x.reshape(batch * seq, D_MODEL)
    for l in range(depth):
        lam_init = 0.8 - 0.6 * math.exp(-0.3 * l)
        qa_hi, qa_lo, ka, ksum, va_t, qb_t, kb, vb_t, ga, gb = _in_proj(h, w_in[l], tables, seq)
        ya, wa, wb, wo, w_down = _moba(qa_hi, qa_lo, ka, va_t, ksum, batch, seq,
                                       (w_branch_a[l], w_branch_b[l], w_out[l], w_ffn_out[l]))
        yb, w_up = _diff(qb_t, kb, vb_t, lambda_qk[l], diff_subln_w[l], batch, seq, lam_init, (w_ffn_in[l],))
        m = _merge(ya, yb, ga, gb, wa, wb)
        h, hb = _out_ln(m, wo, h, ln1_g[l], ln1_b[l], alpha)
        h = _ffn_ln(hb, h, w_up, w_down, ln2_g[l], ln2_b[l], alpha)
    return h.reshape(batch, seq, D_MODEL)
```

```python
import functools
import math

import jax
import jax.numpy as jnp
import numpy as np
from jax import lax
from jax.experimental import pallas as pl
from jax.experimental.pallas import tpu as pltpu

D_MODEL = 2048
HEADS = 8
HEAD_DIM = 128
WIDTH = HEADS * HEAD_DIM
MOBA_BLOCK = 256
MOBA_TOPK = 3
DIFF_QK_DIM = 64
ROPE_THETA = 10000.0
FFN_HIDDEN = 5632
LN_EPS = 1e-5
RMS_EPS = 1e-5

LANES = 128
SUBLANES = 8
VMEM_LIMIT = 56 * 1024 * 1024
MASKED = -1e30
LOG2E = math.log2(math.e)
KV_BLOCK = 256
V_ROWS = HEAD_DIM + 16

BF16 = jnp.bfloat16
F32 = jnp.float32


def _params(sem):
    return pltpu.CompilerParams(dimension_semantics=sem, vmem_limit_bytes=VMEM_LIMIT)


_N_ROPE_TABLES = 5
_ROPE_STEP = 64


def _rope_tables(seq):
    step = _ROPE_STEP
    assert seq % step == 0
    half_a, half_b = HEAD_DIM // 2, DIFF_QK_DIM // 2
    lane = np.arange(_N_ROPE_TABLES * LANES)
    table, l = lane // LANES, lane % LANES
    freq = np.where(table < 2, l % half_a, 2 * (l % half_b))
    upper = np.where(table < 2, l % HEAD_DIM >= half_a, l % DIFF_QK_DIM >= half_b)
    use_sin = (table == 1) | (table >= 3)
    coef = np.select([table == 1, table == 3, table == 4],
                     [np.where(upper, 1.0, -1.0),
                      np.where(upper, 0.0, -1.0),
                      np.where(upper, 1.0, 0.0)],
                     default=1.0).astype(np.float32)
    inv_lane = (ROPE_THETA ** (-jnp.asarray(freq, F32) * 2.0 / HEAD_DIM))[None, :]
    coarse = jnp.arange(seq // step, dtype=F32)[:, None] * float(step) * inv_lane
    fine = jnp.arange(step, dtype=F32)[:, None] * inv_lane
    cc, sc = jnp.cos(coarse), jnp.sin(coarse)
    a_coarse = jnp.where(use_sin, sc, cc) * coef
    b_coarse = jnp.where(use_sin, cc, -sc) * coef
    return a_coarse, b_coarse, jnp.cos(fine), jnp.sin(fine)


_IN_TM = 512
_IN_COLS = 6 * HEAD_DIM + 2 * 2 * HEAD_DIM


def _sigmoid(t):
    return 0.5 * jnp.tanh(0.5 * t) + 0.5


def _in_proj_kernel(x_ref, wqa, wka, wva, wqb, wkb, wvb, wga, wgb,
                    rope_a_ref, rope_b_ref, rope_cos_ref, rope_sin_ref,
                    qa_hi_ref, qa_lo_ref, ka_ref, ksum_ref, va_ref, qb_ref, kb_ref, vb_ref, ga_ref, gb_ref,
                    w_sc):
    def rope_table(j):
        lanes = slice(j * LANES, (j + 1) * LANES)
        fine_cos, fine_sin = rope_cos_ref[:, lanes], rope_sin_ref[:, lanes]
        return jnp.concatenate([rope_a_ref[r:r + 1, lanes] * fine_cos + rope_b_ref[r:r + 1, lanes] * fine_sin
                                for r in range(_IN_TM // _ROPE_STEP)], axis=0)

    @pl.when(pl.program_id(1) == 0)
    def _():
        off = 0
        for w in (wqa, wka, wva, wqb, wkb, wvb, wga, wgb):
            n = w.shape[1]
            w_sc[:, off:off + n] = w[...].astype(BF16)
            off += n

    z = jnp.dot(x_ref[...].astype(BF16), w_sc[...], preferred_element_type=F32)
    d = HEAD_DIM
    blk = KV_BLOCK
    zqa, zka, zva, zqb, zkb, zvb = (z[:, j * d:(j + 1) * d] for j in range(6))
    zga, zgb = z[:, 6 * d:8 * d], z[:, 8 * d:10 * d]

    cos_a, sin_a, cos_b, sin_b_lo, sin_b_hi = (rope_table(j) for j in range(_N_ROPE_TABLES))

    def rope_a(t):
        return t * cos_a + pltpu.roll(t, d // 2, 1) * sin_a

    def rope_b(t):
        return (t * cos_b + pltpu.roll(t, d - DIFF_QK_DIM // 2, 1) * sin_b_lo
                + pltpu.roll(t, DIFF_QK_DIM // 2, 1) * sin_b_hi)

    def store_transposed(ref, t, dtype):
        for j in range(_IN_TM // blk):
            ref[0, j, 0:d] = t[j * blk:(j + 1) * blk, :].T.astype(dtype)
            if ref.shape[2] > d:
                ref[0, j, d:] = jnp.ones((ref.shape[2] - d, blk), dtype)

    qa = rope_a(zqa)
    store_transposed(qa_hi_ref, qa, BF16)
    store_transposed(qa_lo_ref, qa - qa.astype(BF16).astype(F32), BF16)
    ka = rope_a(zka) * (d ** -0.5 * LOG2E)
    ka_ref[...] = ka.astype(BF16)
    for j in range(_IN_TM // MOBA_BLOCK):
        part = ka[j * MOBA_BLOCK:(j + 1) * MOBA_BLOCK].reshape(MOBA_BLOCK // SUBLANES, SUBLANES, d)
        ksum_ref[j * SUBLANES:(j + 1) * SUBLANES, :] = jnp.sum(part, axis=0)
    store_transposed(qb_ref, rope_b(zqb) * (DIFF_QK_DIM ** -0.5 * LOG2E), BF16)
    kb_ref[...] = rope_b(zkb).astype(BF16)
    store_transposed(va_ref, zva, BF16)
    store_transposed(vb_ref, zvb, BF16)
    ga_ref[...] = zga.astype(BF16)
    gb_ref[...] = zgb.astype(BF16)


def _in_proj(x2, w_in, tables, seq):
    t = x2.shape[0]
    tm = _IN_TM
    n_seq_tiles = seq // tm
    grid = (HEADS, t // tm)
    d = HEAD_DIM

    def wspec(group):
        return pl.BlockSpec((D_MODEL, d), lambda h, i, g=group: (0, g * HEADS + h))

    def gspec(base):
        return pl.BlockSpec((D_MODEL, 2 * d), lambda h, i, b=base: (0, b + h))

    rope_lanes = _N_ROPE_TABLES * LANES
    coarse_spec = pl.BlockSpec((tm // _ROPE_STEP, rope_lanes), lambda h, i: (i % n_seq_tiles, 0))
    fine_spec = pl.BlockSpec((_ROPE_STEP, rope_lanes), lambda h, i: (0, 0))
    ospec = pl.BlockSpec((tm, d), lambda h, i: (i, h))
    tr_spec = pl.BlockSpec((1, tm // KV_BLOCK, d, KV_BLOCK), lambda h, i: (h, i, 0, 0))
    gate_ospec = pl.BlockSpec((tm, 2 * d), lambda h, i: (i, h))
    ksum_rows = tm // MOBA_BLOCK * SUBLANES
    tr_shape = (HEADS, t // KV_BLOCK, d, KV_BLOCK)
    v_spec = pl.BlockSpec((1, tm // KV_BLOCK, V_ROWS, KV_BLOCK), lambda h, i: (h, i, 0, 0))
    v_shape = (HEADS, t // KV_BLOCK, V_ROWS, KV_BLOCK)
    out_shape = (
        jax.ShapeDtypeStruct(tr_shape, BF16),
        jax.ShapeDtypeStruct(tr_shape, BF16),
        jax.ShapeDtypeStruct((t, WIDTH), BF16),
        jax.ShapeDtypeStruct((t // MOBA_BLOCK * SUBLANES, WIDTH), F32),
        jax.ShapeDtypeStruct(v_shape, BF16),
        jax.ShapeDtypeStruct(tr_shape, BF16),
        jax.ShapeDtypeStruct((t, WIDTH), BF16),
        jax.ShapeDtypeStruct(v_shape, BF16),
        jax.ShapeDtypeStruct((t, D_MODEL), BF16),
        jax.ShapeDtypeStruct((t, D_MODEL), BF16),
    )
    out_specs = (tr_spec, tr_spec, ospec, pl.BlockSpec((ksum_rows, d), lambda h, i: (i, h)),
                 v_spec, tr_spec, ospec, v_spec, gate_ospec, gate_ospec)
    gate_a_base = 6 * WIDTH // (2 * d)
    gate_b_base = gate_a_base + D_MODEL // (2 * d)
    in_specs = [pl.BlockSpec((tm, D_MODEL), lambda h, i: (i, 0))]
    in_specs += [wspec(g) for g in range(6)]
    in_specs += [gspec(gate_a_base), gspec(gate_b_base)]
    in_specs += [coarse_spec, coarse_spec, fine_spec, fine_spec]
    return pl.pallas_call(
        _in_proj_kernel,
        grid=grid,
        in_specs=in_specs,
        out_specs=out_specs,
        out_shape=out_shape,
        scratch_shapes=[pltpu.VMEM((D_MODEL, _IN_COLS), BF16)],
        compiler_params=_params(("arbitrary", "arbitrary")),
        name="in_proj",
    )(x2, *([w_in] * 8), *tables)


def _pipeline_stage(n, buf, cur, masks, nxt, load_k, load_vt, rhs_ref, acc_ref, s_ref, ms):
    vt_blk = load_vt(n)
    k_next = load_k(n + 1) if nxt else None

    def issue_scores(c):
        s_ref[1 - buf, c] = jnp.dot(k_next, rhs_ref[c], preferred_element_type=F32)

    lead = 2
    for c in nxt[:lead]:
        issue_scores(c)
    rest = nxt[lead:]
    for j, (c, mask) in enumerate(zip(cur, masks)):
        s = s_ref[buf, c]
        if mask is not None:
            s = jnp.where(mask, s, MASKED)
        m_new = jnp.maximum(ms[c], jnp.max(s, axis=0, keepdims=True))
        alpha = jnp.exp2(ms[c] - m_new)
        p = jnp.exp2(s - m_new).astype(BF16)
        acc_ref[c] = alpha * acc_ref[c] + jnp.dot(vt_blk, p, preferred_element_type=F32)
        ms[c] = m_new
        if j < len(rest):
            issue_scores(rest[j])


def _flash_columns(i, n_sub, n_qgroups, blocks_per_iter, load_k, load_vt, rhs_ref, acc_ref, s_ref):
    assert blocks_per_iter % 2 == 0
    assert n_qgroups % blocks_per_iter == 0
    n_groups = n_sub * n_qgroups
    blk = KV_BLOCK
    everyone = list(range(n_groups))
    init = tuple(jnp.full((1, blk), -jnp.inf, F32) for _ in range(n_groups))
    for c in everyone:
        acc_ref[c] = jnp.zeros(acc_ref.shape[1:], F32)
    k_first = load_k(0)
    for c in everyone:
        s_ref[0, c] = jnp.dot(k_first, rhs_ref[c], preferred_element_type=F32)

    stage = functools.partial(_pipeline_stage, load_k=load_k, load_vt=load_vt,
                              rhs_ref=rhs_ref, acc_ref=acc_ref, s_ref=s_ref)

    def body(it, carry):
        ms = list(carry)
        for j in range(blocks_per_iter):
            stage(blocks_per_iter * it + j, j % 2, everyone, [None] * n_groups, everyone, ms=ms)
        return tuple(ms)

    n_past = n_qgroups * i
    ms = list(lax.fori_loop(0, (n_qgroups // blocks_per_iter) * i, body, init))
    row = lax.broadcasted_iota(jnp.int32, (blk, blk), 0)
    col = lax.broadcasted_iota(jnp.int32, (blk, blk), 1)
    causal = row <= col
    for t in range(n_qgroups):
        cur = [c for c in everyone if c % n_qgroups >= t]
        nxt = [c for c in everyone if c % n_qgroups >= t + 1]
        masks = [causal if c % n_qgroups == t else None for c in cur]
        stage(n_past + t, t % 2, cur, masks, nxt, ms=ms)


def _normalised(acc):
    return acc[0:HEAD_DIM] / acc[HEAD_DIM:HEAD_DIM + 1]


BF16_ROW_TILE = 16


def _cast_specs(weights, grid):
    n_steps = grid[0] * grid[1] * grid[2]
    in_specs, out_specs, out_shapes = [], [], []
    for w in weights:
        rows, cols = w.shape
        share = 1
        while rows * share % (n_steps * BF16_ROW_TILE):
            share *= 2
            assert share <= n_steps, (rows, n_steps)
        spec = pl.BlockSpec((rows * share // n_steps, cols),
                            lambda b, h, i, s=share: (((b * grid[1] + h) * grid[2] + i) // s, 0))
        in_specs.append(spec)
        out_specs.append(spec)
        out_shapes.append(jax.ShapeDtypeStruct(w.shape, BF16))
    return in_specs, out_specs, out_shapes


def _cast_blocks(src_refs, dst_refs):
    for src, dst in zip(src_refs, dst_refs):
        dst[...] = src[...].astype(BF16)


_MOBA_QGROUPS = 4


def _split_bf16(a):
    hi = a.astype(BF16)
    lo = (a - hi.astype(F32)).astype(BF16)
    return hi, lo


def _moba_kernel(*refs, n_blocks, n_cast):
    q_hi_ref, q_lo_ref, k_ref, vt_ref, ksum_ref = refs[:5]
    o_ref = refs[5 + n_cast]
    kaug_sc, rhs_sc, acc_sc, s_sc = refs[6 + 2 * n_cast:]
    _cast_blocks(refs[5:5 + n_cast], refs[6 + n_cast:6 + 2 * n_cast])
    i = pl.program_id(2)
    blk = MOBA_BLOCK
    d = HEAD_DIM
    seq = n_blocks * blk

    @pl.when(i == 0)
    def _():
        kaug_sc[:, 0:d] = k_ref[...]
        row_blk = jnp.right_shift(lax.broadcasted_iota(jnp.int32, (seq, LANES), 0), blk.bit_length() - 1)
        lane = lax.broadcasted_iota(jnp.int32, (seq, LANES), 1)
        kaug_sc[:, d:2 * d] = jnp.where(row_blk == lane, 1.0, 0.0).astype(BF16)

    kmean = jnp.sum(ksum_ref[...].reshape(n_blocks, SUBLANES, d), axis=1) * (1.0 / blk)
    m_hi, m_lo = _split_bf16(kmean)
    row = lax.broadcasted_iota(jnp.int32, (n_blocks, blk), 0).astype(F32)

    for g in range(_MOBA_QGROUPS):
        q_hi, q_lo = q_hi_ref[0, g], q_lo_ref[0, g]
        q_blk = (_MOBA_QGROUPS * i + g).astype(F32)
        gate = (jnp.dot(m_hi, q_hi, preferred_element_type=F32)
                + jnp.dot(m_lo, q_hi, preferred_element_type=F32)
                + jnp.dot(m_hi, q_lo, preferred_element_type=F32))
        avail = row < q_blk
        keep = row == q_blk
        val = jnp.where(avail, gate, -jnp.inf)
        for _ in range(MOBA_TOPK):
            best = jnp.max(val, axis=0, keepdims=True)
            cand = (val == best) & avail
            first = jnp.min(jnp.where(cand, row, float(n_blocks)), axis=0, keepdims=True)
            pick = row == first
            keep = keep | pick
            avail = avail & jnp.logical_not(pick)
            val = jnp.where(pick, -jnp.inf, val)
        bias = jnp.where(keep, 0.0, MASKED)
        bias = jnp.concatenate([bias, jnp.zeros((LANES - n_blocks, blk), F32)], axis=0)
        rhs_sc[g] = jnp.concatenate([q_hi, bias.astype(BF16)], axis=0)

    def load_k(n):
        return kaug_sc[pl.ds(pl.multiple_of(n * blk, blk), blk), :]

    def load_vt(n):
        return vt_ref[0, n]

    _flash_columns(i, 1, _MOBA_QGROUPS, 4, load_k, load_vt, rhs_sc, acc_sc, s_sc)
    for g in range(_MOBA_QGROUPS):
        o_ref[g * blk:(g + 1) * blk, :] = _normalised(acc_sc[g]).astype(BF16).T


def _moba(qa_hi, qa_lo, ka, va_t, ksum, batch, seq, weights_to_cast):
    n_blocks = seq // MOBA_BLOCK
    d = HEAD_DIM
    tq = _MOBA_QGROUPS * MOBA_BLOCK
    n_tiles = seq // tq
    grid = (batch, HEADS, n_tiles)
    cast_in, cast_out, cast_shapes = _cast_specs(weights_to_cast, grid)
    q_spec = pl.BlockSpec((1, _MOBA_QGROUPS, d, MOBA_BLOCK), lambda b, h, i: (h, b * n_tiles + i, 0, 0))
    return pl.pallas_call(
        functools.partial(_moba_kernel, n_blocks=n_blocks, n_cast=len(weights_to_cast)),
        grid=grid,
        in_specs=[q_spec, q_spec,
                  pl.BlockSpec((seq, d), lambda b, h, i: (b, h)),
                  pl.BlockSpec((1, n_blocks, V_ROWS, MOBA_BLOCK), lambda b, h, i: (h, b, 0, 0)),
                  pl.BlockSpec((n_blocks * SUBLANES, d), lambda b, h, i: (b, h))] + cast_in,
        out_specs=[pl.BlockSpec((tq, d), lambda b, h, i: (b * n_tiles + i, h))] + cast_out,
        out_shape=[jax.ShapeDtypeStruct(ka.shape, BF16)] + cast_shapes,
        scratch_shapes=[pltpu.VMEM((seq, 2 * d), BF16),
                        pltpu.VMEM((_MOBA_QGROUPS, 2 * d, MOBA_BLOCK), BF16),
                        pltpu.VMEM((_MOBA_QGROUPS, V_ROWS, MOBA_BLOCK), F32),
                        pltpu.VMEM((2, _MOBA_QGROUPS, KV_BLOCK, MOBA_BLOCK), F32)],
        compiler_params=_params(("arbitrary", "arbitrary", "arbitrary")),
        name="moba",
    )(qa_hi, qa_lo, ka, va_t, ksum, *weights_to_cast)


_DIFF_QGROUPS = 4


def _diff_kernel(*refs, lam_init, n_cast):
    q_ref, k_ref, vt_ref, lam_ref, subln_ref = refs[:5]
    o_ref = refs[5 + n_cast]
    rhs_sc, acc_sc, s_sc = refs[6 + 2 * n_cast:]
    _cast_blocks(refs[5:5 + n_cast], refs[6 + n_cast:6 + 2 * n_cast])
    i = pl.program_id(2)
    blk = KV_BLOCK
    d = HEAD_DIM
    nq = _DIFF_QGROUPS

    feat = lax.broadcasted_iota(jnp.int32, (d, blk), 0)
    for sub in range(2):
        own = (feat < DIFF_QK_DIM) if sub == 0 else (feat >= DIFF_QK_DIM)
        for g in range(nq):
            q_t = q_ref[0, g]
            rhs_sc[sub * nq + g] = jnp.where(own, q_t, jnp.zeros_like(q_t))

    def load_k(n):
        return k_ref[pl.ds(pl.multiple_of(n * blk, blk), blk), :]

    def load_vt(n):
        return vt_ref[0, n]

    _flash_columns(i, 2, nq, 4, load_k, load_vt, rhs_sc, acc_sc, s_sc)

    lq = lam_ref[...]
    lam = (jnp.exp(jnp.sum(lq[0:1, :] * lq[1:2, :], axis=1, keepdims=True))
           - jnp.exp(jnp.sum(lq[2:3, :] * lq[3:4, :], axis=1, keepdims=True)) + lam_init)
    for g in range(nq):
        o = _normalised(acc_sc[g]) - lam * _normalised(acc_sc[nq + g])
        o = o * lax.rsqrt(jnp.mean(o * o, axis=0, keepdims=True) + RMS_EPS)
        o_ref[g * blk:(g + 1) * blk, :] = (o.T * subln_ref[...] * (1.0 - lam_init)).astype(BF16)


def _diff(qb_t, kb, vb_t, lambda_qk, subln_w, batch, seq, lam_init, weights_to_cast):
    d = HEAD_DIM
    n_blocks = seq // KV_BLOCK
    tq = _DIFF_QGROUPS * KV_BLOCK
    n_tiles = seq // tq
    grid = (batch, HEADS, n_tiles)
    cast_in, cast_out, cast_shapes = _cast_specs(weights_to_cast, grid)
    return pl.pallas_call(
        functools.partial(_diff_kernel, lam_init=lam_init, n_cast=len(weights_to_cast)),
        grid=grid,
        in_specs=[pl.BlockSpec((1, _DIFF_QGROUPS, d, KV_BLOCK), lambda b, h, i: (h, b * n_tiles + i, 0, 0)),
                  pl.BlockSpec((seq, d), lambda b, h, i: (b, h)),
                  pl.BlockSpec((1, n_blocks, V_ROWS, KV_BLOCK), lambda b, h, i: (h, b, 0, 0)),
                  pl.BlockSpec(lambda_qk.shape, lambda b, h, i: (0, 0)),
                  pl.BlockSpec((1, d), lambda b, h, i: (0, 0))] + cast_in,
        out_specs=[pl.BlockSpec((tq, d), lambda b, h, i: (b * n_tiles + i, h))] + cast_out,
        out_shape=[jax.ShapeDtypeStruct(kb.shape, BF16)] + cast_shapes,
        scratch_shapes=[pltpu.VMEM((2 * _DIFF_QGROUPS, d, KV_BLOCK), BF16),
                        pltpu.VMEM((2 * _DIFF_QGROUPS, V_ROWS, KV_BLOCK), F32),
                        pltpu.VMEM((2, 2 * _DIFF_QGROUPS, KV_BLOCK, KV_BLOCK), F32)],
        compiler_params=_params(("arbitrary", "arbitrary", "arbitrary")),
        name="diffattn",
    )(qb_t, kb, vb_t, lambda_qk, subln_w.reshape(1, d), *weights_to_cast)


_ROW_TM = 512
_FFN_TH = 512


def _merge_kernel(ya_ref, yb_ref, ga_ref, gb_ref, wa_ref, wb_ref, o_ref):
    a = jnp.dot(ya_ref[...], wa_ref[...], preferred_element_type=F32)
    b = jnp.dot(yb_ref[...], wb_ref[...], preferred_element_type=F32)
    o_ref[...] = (_sigmoid(ga_ref[...].astype(F32)) * a + _sigmoid(gb_ref[...].astype(F32)) * b).astype(BF16)


def _merge(ya, yb, ga, gb, wa, wb):
    t = ya.shape[0]
    tm = _ROW_TM
    row = lambda w: pl.BlockSpec((tm, w), lambda i: (i, 0))
    full = lambda a: pl.BlockSpec(a.shape, lambda i: (0, 0))
    return pl.pallas_call(
        _merge_kernel,
        grid=(t // tm,),
        in_specs=[row(WIDTH), row(WIDTH), row(D_MODEL), row(D_MODEL), full(wa), full(wb)],
        out_specs=row(D_MODEL),
        out_shape=jax.ShapeDtypeStruct((t, D_MODEL), BF16),
        compiler_params=_params(("parallel",)),
        name="merge",
    )(ya, yb, ga, gb, wa, wb)


def _layer_norm(y, g, b):
    mu = jnp.mean(y, axis=1, keepdims=True)
    c = y - mu
    var = jnp.mean(c * c, axis=1, keepdims=True)
    return c * lax.rsqrt(var + LN_EPS) * g + b


def _row_halves(ref):
    half = ref.shape[0] // 2
    return slice(0, half), slice(half, 2 * half)


def _out_ln_kernel(m_ref, w_ref, x_ref, g_ref, b_ref, h_ref, hb_ref, *, alpha):
    halves = _row_halves(m_ref)
    ys = [alpha * x_ref[rows, :] + jnp.dot(m_ref[rows, :], w_ref[...], preferred_element_type=F32)
          for rows in halves]
    for rows, y in zip(halves, ys):
        h = _layer_norm(y, g_ref[...], b_ref[...])
        h_ref[rows, :] = h
        hb_ref[rows, :] = h.astype(BF16)


def _out_ln(m, w_out, x2, g, b, alpha):
    t = m.shape[0]
    tm = _ROW_TM
    row = pl.BlockSpec((tm, D_MODEL), lambda i: (i, 0))
    vec = pl.BlockSpec((1, D_MODEL), lambda i: (0, 0))
    return pl.pallas_call(
        functools.partial(_out_ln_kernel, alpha=alpha),
        grid=(t // tm,),
        in_specs=[row, pl.BlockSpec(w_out.shape, lambda i: (0, 0)), row, vec, vec],
        out_specs=(row, row),
        out_shape=(jax.ShapeDtypeStruct((t, D_MODEL), F32), jax.ShapeDtypeStruct((t, D_MODEL), BF16)),
        compiler_params=_params(("parallel",)),
        name="out_ln",
    )(m, w_out, x2, g.reshape(1, D_MODEL), b.reshape(1, D_MODEL))


def _ffn_kernel(hb_ref, wg_ref, wu_ref, wo_ref, h_ref, g_ref, b_ref, o_ref, *, alpha):
    j = pl.program_id(1)

    @pl.when(j == 0)
    def _():
        o_ref[...] = alpha * h_ref[...]

    hb = hb_ref[...]
    gate = jnp.dot(hb, wg_ref[...], preferred_element_type=F32)
    up = jnp.dot(hb, wu_ref[...], preferred_element_type=F32)
    act = (gate * _sigmoid(gate) * up).astype(BF16)
    o_ref[...] += jnp.dot(act, wo_ref[...], preferred_element_type=F32)

    @pl.when(j == pl.num_programs(1) - 1)
    def _():
        o_ref[...] = _layer_norm(o_ref[...], g_ref[...], b_ref[...])


def _ffn_ln(hb, h, w_ffn_in, w_ffn_out, g, b, alpha):
    t = h.shape[0]
    tm, th = _ROW_TM, _FFN_TH
    nj = FFN_HIDDEN // th
    row = pl.BlockSpec((tm, D_MODEL), lambda i, j: (i, 0))
    vec = pl.BlockSpec((1, D_MODEL), lambda i, j: (0, 0))
    return pl.pallas_call(
        functools.partial(_ffn_kernel, alpha=alpha),
        grid=(t // tm, nj),
        in_specs=[row,
                  pl.BlockSpec((D_MODEL, th), lambda i, j: (0, j)),
                  pl.BlockSpec((D_MODEL, th), lambda i, j: (0, nj + j)),
                  pl.BlockSpec((th, D_MODEL), lambda i, j: (j, 0)),
                  row, vec, vec],
        out_specs=row,
        out_shape=jax.ShapeDtypeStruct((t, D_MODEL), F32),
        compiler_params=_params(("parallel", "arbitrary")),
        name="ffn_ln",
    )(hb, w_ffn_in, w_ffn_in, w_ffn_out, h, g.reshape(1, D_MODEL), b.reshape(1, D_MODEL))


def kernel(x, w_in, lambda_qk, diff_subln_w, w_branch_a, w_branch_b, w_out,
           ln1_g, ln1_b, w_ffn_in, w_ffn_out, ln2_g, ln2_b):
    batch, seq, _ = x.shape
    depth = w_in.shape[0]
    alpha = (2.0 * depth) ** 0.25
    tables = _rope_tables(seq)
    h = x.reshape(batch * seq, D_MODEL)
    for l in range(depth):
        lam_init = 0.8 - 0.6 * math.exp(-0.3 * l)
        qa_hi, qa_lo, ka, ksum, va_t, qb_t, kb, vb_t, ga, gb = _in_proj(h, w_in[l], tables, seq)
        ya, wa, wb, wo, w_down = _moba(qa_hi, qa_lo, ka, va_t, ksum, batch, seq,
                                       (w_branch_a[l], w_branch_b[l], w_out[l], w_ffn_out[l]))
        yb, w_up = _diff(qb_t, kb, vb_t, lambda_qk[l], diff_subln_w[l], batch, seq, lam_init, (w_ffn_in[l],))
        m = _merge(ya, yb, ga, gb, wa, wb)
        h, hb = _out_ln(m, wo, h, ln1_g[l], ln1_b[l], alpha)
        h = _ffn_ln(hb, h, w_up, w_down, ln2_g[l], ln2_b[l], alpha)
    return h.reshape(batch, seq, D_MODEL)
```

```python
import functools
import math

import jax
import jax.numpy as jnp
import numpy as np
from jax import lax
from jax.experimental import pallas as pl
from jax.experimental.pallas import tpu as pltpu

D_MODEL = 2048
HEADS = 8
HEAD_DIM = 128
WIDTH = HEADS * HEAD_DIM
MOBA_BLOCK = 256
MOBA_TOPK = 3
DIFF_QK_DIM = 64
ROPE_THETA = 10000.0
FFN_HIDDEN = 5632
LN_EPS = 1e-5
RMS_EPS = 1e-5

LANES = 128
SUBLANES = 8
VMEM_LIMIT = 56 * 1024 * 1024
MASKED = -1e30
LOG2E = math.log2(math.e)
KV_BLOCK = 256
V_ROWS = HEAD_DIM + 16

BF16 = jnp.bfloat16
F32 = jnp.float32


def _params(sem):
    return pltpu.CompilerParams(dimension_semantics=sem, vmem_limit_bytes=VMEM_LIMIT)


_N_ROPE_TABLES = 5
_ROPE_STEP = 64


def _rope_tables(seq):
    step = _ROPE_STEP
    assert seq % step == 0
    half_a, half_b = HEAD_DIM // 2, DIFF_QK_DIM // 2
    lane = np.arange(_N_ROPE_TABLES * LANES)
    table, l = lane // LANES, lane % LANES
    freq = np.where(table < 2, l % half_a, 2 * (l % half_b))
    upper = np.where(table < 2, l % HEAD_DIM >= half_a, l % DIFF_QK_DIM >= half_b)
    use_sin = (table == 1) | (table >= 3)
    coef = np.select([table == 1, table == 3, table == 4],
                     [np.where(upper, 1.0, -1.0),
                      np.where(upper, 0.0, -1.0),
                      np.where(upper, 1.0, 0.0)],
                     default=1.0).astype(np.float32)
    inv_lane = (ROPE_THETA ** (-jnp.asarray(freq, F32) * 2.0 / HEAD_DIM))[None, :]
    coarse = jnp.arange(seq // step, dtype=F32)[:, None] * float(step) * inv_lane
    fine = jnp.arange(step, dtype=F32)[:, None] * inv_lane
    cc, sc = jnp.cos(coarse), jnp.sin(coarse)
    a_coarse = jnp.where(use_sin, sc, cc) * coef
    b_coarse = jnp.where(use_sin, cc, -sc) * coef
    return a_coarse, b_coarse, jnp.cos(fine), jnp.sin(fine)


_IN_TM = 512
_IN_COLS = 6 * HEAD_DIM + 2 * 2 * HEAD_DIM


def _sigmoid(t):
    return 0.5 * jnp.tanh(0.5 * t) + 0.5


def _in_proj_kernel(x_ref, wqa, wka, wva, wqb, wkb, wvb, wga, wgb,
                    rope_a_ref, rope_b_ref, rope_cos_ref, rope_sin_ref,
                    qa_hi_ref, qa_lo_ref, ka_ref, ksum_ref, va_ref, qb_ref, kb_ref, vb_ref, ga_ref, gb_ref,
                    w_sc):
    def rope_table(j):
        lanes = slice(j * LANES, (j + 1) * LANES)
        fine_cos, fine_sin = rope_cos_ref[:, lanes], rope_sin_ref[:, lanes]
        return jnp.concatenate([rope_a_ref[r:r + 1, lanes] * fine_cos + rope_b_ref[r:r + 1, lanes] * fine_sin
                                for r in range(_IN_TM // _ROPE_STEP)], axis=0)

    @pl.when(pl.program_id(1) == 0)
    def _():
        off = 0
        for w in (wqa, wka, wva, wqb, wkb, wvb, wga, wgb):
            n = w.shape[1]
            w_sc[:, off:off + n] = w[...].astype(BF16)
            off += n

    z = jnp.dot(x_ref[...].astype(BF16), w_sc[...], preferred_element_type=F32)
    d = HEAD_DIM
    blk = KV_BLOCK
    zqa, zka, zva, zqb, zkb, zvb = (z[:, j * d:(j + 1) * d] for j in range(6))
    zga, zgb = z[:, 6 * d:8 * d], z[:, 8 * d:10 * d]

    cos_a, sin_a, cos_b, sin_b_lo, sin_b_hi = (rope_table(j) for j in range(_N_ROPE_TABLES))

    def rope_a(t):
        return t * cos_a + pltpu.roll(t, d // 2, 1) * sin_a

    def rope_b(t):
        return (t * cos_b + pltpu.roll(t, d - DIFF_QK_DIM // 2, 1) * sin_b_lo
                + pltpu.roll(t, DIFF_QK_DIM // 2, 1) * sin_b_hi)

    def store_transposed(ref, t, dtype):
        for j in range(_IN_TM // blk):
            ref[0, j, 0:d] = t[j * blk:(j + 1) * blk, :].T.astype(dtype)
            if ref.shape[2] > d:
                ref[0, j, d:] = jnp.ones((ref.shape[2] - d, blk), dtype)

    qa = rope_a(zqa)
    store_transposed(qa_hi_ref, qa, BF16)
    store_transposed(qa_lo_ref, qa - qa.astype(BF16).astype(F32), BF16)
    ka = rope_a(zka) * (d ** -0.5 * LOG2E)
    ka_ref[...] = ka.astype(BF16)
    for j in range(_IN_TM // MOBA_BLOCK):
        part = ka[j * MOBA_BLOCK:(j + 1) * MOBA_BLOCK].reshape(MOBA_BLOCK // SUBLANES, SUBLANES, d)
        ksum_ref[j * SUBLANES:(j + 1) * SUBLANES, :] = jnp.sum(part, axis=0)
    store_transposed(qb_ref, rope_b(zqb) * (DIFF_QK_DIM ** -0.5 * LOG2E), BF16)
    kb_ref[...] = rope_b(zkb).astype(BF16)
    store_transposed(va_ref, zva, BF16)
    store_transposed(vb_ref, zvb, BF16)
    ga_ref[...] = zga.astype(BF16)
    gb_ref[...] = zgb.astype(BF16)


def _in_proj(x2, w_in, tables, seq):
    t = x2.shape[0]
    tm = _IN_TM
    n_seq_tiles = seq // tm
    grid = (HEADS, t // tm)
    d = HEAD_DIM

    def wspec(group):
        return pl.BlockSpec((D_MODEL, d), lambda h, i, g=group: (0, g * HEADS + h))

    def gspec(base):
        return pl.BlockSpec((D_MODEL, 2 * d), lambda h, i, b=base: (0, b + h))

    rope_lanes = _N_ROPE_TABLES * LANES
    coarse_spec = pl.BlockSpec((tm // _ROPE_STEP, rope_lanes), lambda h, i: (i % n_seq_tiles, 0))
    fine_spec = pl.BlockSpec((_ROPE_STEP, rope_lanes), lambda h, i: (0, 0))
    ospec = pl.BlockSpec((tm, d), lambda h, i: (i, h))
    tr_spec = pl.BlockSpec((1, tm // KV_BLOCK, d, KV_BLOCK), lambda h, i: (h, i, 0, 0))
    gate_ospec = pl.BlockSpec((tm, 2 * d), lambda h, i: (i, h))
    ksum_rows = tm // MOBA_BLOCK * SUBLANES
    tr_shape = (HEADS, t // KV_BLOCK, d, KV_BLOCK)
    v_spec = pl.BlockSpec((1, tm // KV_BLOCK, V_ROWS, KV_BLOCK), lambda h, i: (h, i, 0, 0))
    v_shape = (HEADS, t // KV_BLOCK, V_ROWS, KV_BLOCK)
    out_shape = (
        jax.ShapeDtypeStruct(tr_shape, BF16),
        jax.ShapeDtypeStruct(tr_shape, BF16),
        jax.ShapeDtypeStruct((t, WIDTH), BF16),
        jax.ShapeDtypeStruct((t // MOBA_BLOCK * SUBLANES, WIDTH), F32),
        jax.ShapeDtypeStruct(v_shape, BF16),
        jax.ShapeDtypeStruct(tr_shape, BF16),
        jax.ShapeDtypeStruct((t, WIDTH), BF16),
        jax.ShapeDtypeStruct(v_shape, BF16),
        jax.ShapeDtypeStruct((t, D_MODEL), BF16),
        jax.ShapeDtypeStruct((t, D_MODEL), BF16),
    )
    out_specs = (tr_spec, tr_spec, ospec, pl.BlockSpec((ksum_rows, d), lambda h, i: (i, h)),
                 v_spec, tr_spec, ospec, v_spec, gate_ospec, gate_ospec)
    gate_a_base = 6 * WIDTH // (2 * d)
    gate_b_base = gate_a_base + D_MODEL // (2 * d)
    in_specs = [pl.BlockSpec((tm, D_MODEL), lambda h, i: (i, 0))]
    in_specs += [wspec(g) for g in range(6)]
    in_specs += [gspec(gate_a_base), gspec(gate_b_base)]
    in_specs += [coarse_spec, coarse_spec, fine_spec, fine_spec]
    return pl.pallas_call(
        _in_proj_kernel,
        grid=grid,
        in_specs=in_specs,
        out_specs=out_specs,
        out_shape=out_shape,
        scratch_shapes=[pltpu.VMEM((D_MODEL, _IN_COLS), BF16)],
        compiler_params=_params(("arbitrary", "arbitrary")),
        name="in_proj",
    )(x2, *([w_in] * 8), *tables)


class _Operands:
    def __init__(self, n_sub, n_qgroups, src_of_sub, load_k, load_vt, load_rhs, bias_row=None):
        self.n_sub, self.n_qgroups = n_sub, n_qgroups
        self.src_of_sub, self.load_k, self.load_vt = src_of_sub, load_k, load_vt
        self.load_rhs, self.bias_row = load_rhs, bias_row

    def src(self, c):
        return self.src_of_sub[c // self.n_qgroups]


def _pipeline_stage(n, buf, cur, masks, nxt, ops, acc_ref, s_ref, ms):
    vt_blk = {src: ops.load_vt(n, src) for src in sorted({ops.src(c) for c in cur})}
    k_next = {src: ops.load_k(n + 1, src) for src in sorted({ops.src(c) for c in nxt})}

    def issue_scores(c):
        s_ref[1 - buf, c] = jnp.dot(k_next[ops.src(c)], ops.load_rhs(c), preferred_element_type=F32)

    lead = 2
    for c in nxt[:lead]:
        issue_scores(c)
    rest = nxt[lead:]
    for j, (c, mask) in enumerate(zip(cur, masks)):
        s = s_ref[buf, c]
        if mask is not None:
            s = jnp.where(mask, s, MASKED)
        top = jnp.max(s, axis=0, keepdims=True)
        bias = ops.bias_row(n, c) if (ops.bias_row is not None and mask is None) else None
        if bias is None:
            m_new = jnp.maximum(ms[c], top)
            shift = m_new
        else:
            seen = bias == 0.0
            m_new = jnp.maximum(ms[c], jnp.where(seen, top, MASKED))
            shift = jnp.where(seen, m_new, -MASKED)
        alpha = jnp.exp2(ms[c] - m_new)
        p = jnp.exp2(s - shift).astype(BF16)
        acc_ref[c] = alpha * acc_ref[c] + jnp.dot(vt_blk[ops.src(c)], p, preferred_element_type=F32)
        ms[c] = m_new
        if j < len(rest):
            issue_scores(rest[j])


def _flash_columns(i, ops, blocks_per_iter, acc_ref, s_ref):
    n_qgroups = ops.n_qgroups
    assert blocks_per_iter % 2 == 0
    assert n_qgroups % blocks_per_iter == 0
    n_groups = ops.n_sub * n_qgroups
    blk = KV_BLOCK
    everyone = list(range(n_groups))
    init = tuple(jnp.full((1, blk), MASKED, F32) for _ in range(n_groups))
    for c in everyone:
        acc_ref[c] = jnp.zeros(acc_ref.shape[1:], F32)
    k_first = {src: ops.load_k(0, src) for src in sorted(set(ops.src_of_sub))}
    for c in everyone:
        s_ref[0, c] = jnp.dot(k_first[ops.src(c)], ops.load_rhs(c), preferred_element_type=F32)

    stage = functools.partial(_pipeline_stage, ops=ops, acc_ref=acc_ref, s_ref=s_ref)

    def body(it, carry):
        ms = list(carry)
        for j in range(blocks_per_iter):
            stage(blocks_per_iter * it + j, j % 2, everyone, [None] * n_groups, everyone, ms=ms)
        return tuple(ms)

    n_past = n_qgroups * i
    ms = list(lax.fori_loop(0, (n_qgroups // blocks_per_iter) * i, body, init))
    row = lax.broadcasted_iota(jnp.int32, (blk, blk), 0)
    col = lax.broadcasted_iota(jnp.int32, (blk, blk), 1)
    causal = row <= col
    for t in range(n_qgroups):
        cur = [c for c in everyone if c % n_qgroups >= t]
        nxt = [c for c in everyone if c % n_qgroups >= t + 1]
        masks = [causal if c % n_qgroups == t else None for c in cur]
        stage(n_past + t, t % 2, cur, masks, nxt, ms=ms)


def _normalised(acc):
    return acc[0:HEAD_DIM] / acc[HEAD_DIM:HEAD_DIM + 1]


BF16_ROW_TILE = 16


def _cast_specs(weights, grid):
    n_steps = grid[0] * grid[1] * grid[2]
    in_specs, out_specs, out_shapes = [], [], []
    for w in weights:
        rows, cols = w.shape
        share = 1
        while rows * share % (n_steps * BF16_ROW_TILE):
            share *= 2
            assert share <= n_steps, (rows, n_steps)
        spec = pl.BlockSpec((rows * share // n_steps, cols),
                            lambda b, h, i, s=share: (((b * grid[1] + h) * grid[2] + i) // s, 0))
        in_specs.append(spec)
        out_specs.append(spec)
        out_shapes.append(jax.ShapeDtypeStruct(w.shape, BF16))
    return in_specs, out_specs, out_shapes


def _cast_blocks(src_refs, dst_refs):
    for src, dst in zip(src_refs, dst_refs):
        dst[...] = src[...].astype(BF16)


_MOBA_QGROUPS = 4
_MOBA_HEADS_PER_STEP = 2


def _split_bf16(a):
    hi = a.astype(BF16)
    lo = (a - hi.astype(F32)).astype(BF16)
    return hi, lo


def _moba_kernel(*refs, n_blocks, n_cast):
    q_hi_ref, q_lo_ref, k_ref, vt_ref, ksum_ref = refs[:5]
    o_ref = refs[5 + n_cast]
    bias_sc, acc_sc, s_sc = refs[6 + 2 * n_cast:]
    _cast_blocks(refs[5:5 + n_cast], refs[6 + n_cast:6 + 2 * n_cast])
    i = pl.program_id(2)
    blk = MOBA_BLOCK
    d = HEAD_DIM
    nq = _MOBA_QGROUPS
    row = lax.broadcasted_iota(jnp.int32, (n_blocks, blk), 0).astype(F32)

    for hd in range(_MOBA_HEADS_PER_STEP):
        ksum = ksum_ref[:, hd * d:(hd + 1) * d]
        kmean = jnp.sum(ksum.reshape(n_blocks, SUBLANES, d), axis=1) * (1.0 / blk)
        m_hi, m_lo = _split_bf16(kmean)
        for g in range(nq):
            q_hi, q_lo = q_hi_ref[hd, g], q_lo_ref[hd, g]
            q_blk = (nq * i + g).astype(F32)
            gate = (jnp.dot(m_hi, q_hi, preferred_element_type=F32)
                    + jnp.dot(m_lo, q_hi, preferred_element_type=F32)
                    + jnp.dot(m_hi, q_lo, preferred_element_type=F32))
            avail = row < q_blk
            keep = row == q_blk
            val = jnp.where(avail, gate, -jnp.inf)
            for _ in range(MOBA_TOPK):
                best = jnp.max(val, axis=0, keepdims=True)
                cand = (val == best) & avail
                first = jnp.min(jnp.where(cand, row, float(n_blocks)), axis=0, keepdims=True)
                pick = row == first
                keep = keep | pick
                avail = avail & jnp.logical_not(pick)
                val = jnp.where(pick, -jnp.inf, val)
            bias = jnp.where(keep, 0.0, MASKED)
            for n in range(n_blocks):
                bias_sc[hd * nq + g, n] = bias[n:n + 1, :]

    ops = _Operands(
        n_sub=_MOBA_HEADS_PER_STEP, n_qgroups=nq, src_of_sub=list(range(_MOBA_HEADS_PER_STEP)),
        load_k=lambda n, hd: k_ref[pl.ds(pl.multiple_of(n * blk, blk), blk), hd * d:(hd + 1) * d],
        load_vt=lambda n, hd: vt_ref[hd, n],
        load_rhs=lambda c: q_hi_ref[c // nq, c % nq],
        bias_row=lambda n, c: bias_sc[c, n])
    _flash_columns(i, ops, 4, acc_sc, s_sc)
    for hd in range(_MOBA_HEADS_PER_STEP):
        for g in range(nq):
            o_ref[g * blk:(g + 1) * blk, hd * d:(hd + 1) * d] = _normalised(acc_sc[hd * nq + g]).astype(BF16).T


def _moba(qa_hi, qa_lo, ka, va_t, ksum, batch, seq, weights_to_cast):
    n_blocks = seq // MOBA_BLOCK
    d = HEAD_DIM
    hps = _MOBA_HEADS_PER_STEP
    n_groups = hps * _MOBA_QGROUPS
    tq = _MOBA_QGROUPS * MOBA_BLOCK
    n_tiles = seq // tq
    grid = (batch, HEADS // hps, n_tiles)
    cast_in, cast_out, cast_shapes = _cast_specs(weights_to_cast, grid)
    q_spec = pl.BlockSpec((hps, _MOBA_QGROUPS, d, MOBA_BLOCK), lambda b, h, i: (h, b * n_tiles + i, 0, 0))
    return pl.pallas_call(
        functools.partial(_moba_kernel, n_blocks=n_blocks, n_cast=len(weights_to_cast)),
        grid=grid,
        in_specs=[q_spec, q_spec,
                  pl.BlockSpec((seq, hps * d), lambda b, h, i: (b, h)),
                  pl.BlockSpec((hps, n_blocks, V_ROWS, MOBA_BLOCK), lambda b, h, i: (h, b, 0, 0)),
                  pl.BlockSpec((n_blocks * SUBLANES, hps * d), lambda b, h, i: (b, h))] + cast_in,
        out_specs=[pl.BlockSpec((tq, hps * d), lambda b, h, i: (b * n_tiles + i, h))] + cast_out,
        out_shape=[jax.ShapeDtypeStruct(ka.shape, BF16)] + cast_shapes,
        scratch_shapes=[pltpu.VMEM((n_groups, n_blocks, 1, MOBA_BLOCK), F32),
                        pltpu.VMEM((n_groups, V_ROWS, MOBA_BLOCK), F32),
                        pltpu.VMEM((2, n_groups, KV_BLOCK, MOBA_BLOCK), F32)],
        compiler_params=_params(("arbitrary", "arbitrary", "arbitrary")),
        name="moba",
    )(qa_hi, qa_lo, ka, va_t, ksum, *weights_to_cast)


_DIFF_QGROUPS = 4


def _diff_kernel(*refs, lam_init, n_cast):
    q_ref, k_ref, vt_ref, lam_ref, subln_ref = refs[:5]
    o_ref = refs[5 + n_cast]
    rhs_sc, acc_sc, s_sc = refs[6 + 2 * n_cast:]
    _cast_blocks(refs[5:5 + n_cast], refs[6 + n_cast:6 + 2 * n_cast])
    i = pl.program_id(2)
    blk = KV_BLOCK
    d = HEAD_DIM
    nq = _DIFF_QGROUPS

    feat = lax.broadcasted_iota(jnp.int32, (d, blk), 0)
    for sub in range(2):
        own = (feat < DIFF_QK_DIM) if sub == 0 else (feat >= DIFF_QK_DIM)
        for g in range(nq):
            q_t = q_ref[0, g]
            rhs_sc[sub * nq + g] = jnp.where(own, q_t, jnp.zeros_like(q_t))

    ops = _Operands(
        n_sub=2, n_qgroups=nq, src_of_sub=[0, 0],
        load_k=lambda n, src: k_ref[pl.ds(pl.multiple_of(n * blk, blk), blk), :],
        load_vt=lambda n, src: vt_ref[0, n],
        load_rhs=lambda c: rhs_sc[c])
    _flash_columns(i, ops, 4, acc_sc, s_sc)

    lq = lam_ref[...]
    lam = (jnp.exp(jnp.sum(lq[0:1, :] * lq[1:2, :], axis=1, keepdims=True))
           - jnp.exp(jnp.sum(lq[2:3, :] * lq[3:4, :], axis=1, keepdims=True)) + lam_init)
    for g in range(nq):
        o = _normalised(acc_sc[g]) - lam * _normalised(acc_sc[nq + g])
        o = o * lax.rsqrt(jnp.mean(o * o, axis=0, keepdims=True) + RMS_EPS)
        o_ref[g * blk:(g + 1) * blk, :] = (o.T * subln_ref[...] * (1.0 - lam_init)).astype(BF16)


def _diff(qb_t, kb, vb_t, lambda_qk, subln_w, batch, seq, lam_init, weights_to_cast):
    d = HEAD_DIM
    n_blocks = seq // KV_BLOCK
    tq = _DIFF_QGROUPS * KV_BLOCK
    n_tiles = seq // tq
    grid = (batch, HEADS, n_tiles)
    cast_in, cast_out, cast_shapes = _cast_specs(weights_to_cast, grid)
    return pl.pallas_call(
        functools.partial(_diff_kernel, lam_init=lam_init, n_cast=len(weights_to_cast)),
        grid=grid,
        in_specs=[pl.BlockSpec((1, _DIFF_QGROUPS, d, KV_BLOCK), lambda b, h, i: (h, b * n_tiles + i, 0, 0)),
                  pl.BlockSpec((seq, d), lambda b, h, i: (b, h)),
                  pl.BlockSpec((1, n_blocks, V_ROWS, KV_BLOCK), lambda b, h, i: (h, b, 0, 0)),
                  pl.BlockSpec(lambda_qk.shape, lambda b, h, i: (0, 0)),
                  pl.BlockSpec((1, d), lambda b, h, i: (0, 0))] + cast_in,
        out_specs=[pl.BlockSpec((tq, d), lambda b, h, i: (b * n_tiles + i, h))] + cast_out,
        out_shape=[jax.ShapeDtypeStruct(kb.shape, BF16)] + cast_shapes,
        scratch_shapes=[pltpu.VMEM((2 * _DIFF_QGROUPS, d, KV_BLOCK), BF16),
                        pltpu.VMEM((2 * _DIFF_QGROUPS, V_ROWS, KV_BLOCK), F32),
                        pltpu.VMEM((2, 2 * _DIFF_QGROUPS, KV_BLOCK, KV_BLOCK), F32)],
        compiler_params=_params(("arbitrary", "arbitrary", "arbitrary")),
        name="diffattn",
    )(qb_t, kb, vb_t, lambda_qk, subln_w.reshape(1, d), *weights_to_cast)


_ROW_TM = 512
_FFN_TH = 512


def _merge_kernel(ya_ref, yb_ref, ga_ref, gb_ref, wa_ref, wb_ref, o_ref):
    a = jnp.dot(ya_ref[...], wa_ref[...], preferred_element_type=F32)
    b = jnp.dot(yb_ref[...], wb_ref[...], preferred_element_type=F32)
    o_ref[...] = (_sigmoid(ga_ref[...].astype(F32)) * a + _sigmoid(gb_ref[...].astype(F32)) * b).astype(BF16)


def _merge(ya, yb, ga, gb, wa, wb):
    t = ya.shape[0]
    tm = _ROW_TM
    row = lambda w: pl.BlockSpec((tm, w), lambda i: (i, 0))
    full = lambda a: pl.BlockSpec(a.shape, lambda i: (0, 0))
    return pl.pallas_call(
        _merge_kernel,
        grid=(t // tm,),
        in_specs=[row(WIDTH), row(WIDTH), row(D_MODEL), row(D_MODEL), full(wa), full(wb)],
        out_specs=row(D_MODEL),
        out_shape=jax.ShapeDtypeStruct((t, D_MODEL), BF16),
        compiler_params=_params(("parallel",)),
        name="merge",
    )(ya, yb, ga, gb, wa, wb)


def _layer_norm(y, g, b):
    mu = jnp.mean(y, axis=1, keepdims=True)
    c = y - mu
    var = jnp.mean(c * c, axis=1, keepdims=True)
    return c * lax.rsqrt(var + LN_EPS) * g + b


def _row_halves(ref):
    half = ref.shape[0] // 2
    return slice(0, half), slice(half, 2 * half)


def _out_ln_kernel(m_ref, w_ref, x_ref, g_ref, b_ref, h_ref, hb_ref, *, alpha):
    halves = _row_halves(m_ref)
    ys = [alpha * x_ref[rows, :] + jnp.dot(m_ref[rows, :], w_ref[...], preferred_element_type=F32)
          for rows in halves]
    for rows, y in zip(halves, ys):
        h = _layer_norm(y, g_ref[...], b_ref[...])
        h_ref[rows, :] = h
        hb_ref[rows, :] = h.astype(BF16)


def _out_ln(m, w_out, x2, g, b, alpha):
    t = m.shape[0]
    tm = _ROW_TM
    row = pl.BlockSpec((tm, D_MODEL), lambda i: (i, 0))
    vec = pl.BlockSpec((1, D_MODEL), lambda i: (0, 0))
    return pl.pallas_call(
        functools.partial(_out_ln_kernel, alpha=alpha),
        grid=(t // tm,),
        in_specs=[row, pl.BlockSpec(w_out.shape, lambda i: (0, 0)), row, vec, vec],
        out_specs=(row, row),
        out_shape=(jax.ShapeDtypeStruct((t, D_MODEL), F32), jax.ShapeDtypeStruct((t, D_MODEL), BF16)),
        compiler_params=_params(("parallel",)),
        name="out_ln",
    )(m, w_out, x2, g.reshape(1, D_MODEL), b.reshape(1, D_MODEL))


def _ffn_kernel(hb_ref, wg_ref, wu_ref, wo_ref, h_ref, g_ref, b_ref, o_ref, *, alpha):
    j = pl.program_id(1)

    @pl.when(j == 0)
    def _():
        o_ref[...] = alpha * h_ref[...]

    hb = hb_ref[...]
    gate = jnp.dot(hb, wg_ref[...], preferred_element_type=F32)
    up = jnp.dot(hb, wu_ref[...], preferred_element_type=F32)
    act = (gate * _sigmoid(gate) * up).astype(BF16)
    o_ref[...] += jnp.dot(act, wo_ref[...], preferred_element_type=F32)

    @pl.when(j == pl.num_programs(1) - 1)
    def _():
        o_ref[...] = _layer_norm(o_ref[...], g_ref[...], b_ref[...])


def _ffn_ln(hb, h, w_ffn_in, w_ffn_out, g, b, alpha):
    t = h.shape[0]
    tm, th = _ROW_TM, _FFN_TH
    nj = FFN_HIDDEN // th
    row = pl.BlockSpec((tm, D_MODEL), lambda i, j: (i, 0))
    vec = pl.BlockSpec((1, D_MODEL), lambda i, j: (0, 0))
    return pl.pallas_call(
        functools.partial(_ffn_kernel, alpha=alpha),
        grid=(t // tm, nj),
        in_specs=[row,
                  pl.BlockSpec((D_MODEL, th), lambda i, j: (0, j)),
                  pl.BlockSpec((D_MODEL, th), lambda i, j: (0, nj + j)),
                  pl.BlockSpec((th, D_MODEL), lambda i, j: (j, 0)),
                  row, vec, vec],
        out_specs=row,
        out_shape=jax.ShapeDtypeStruct((t, D_MODEL), F32),
        compiler_params=_params(("parallel", "arbitrary")),
        name="ffn_ln",
    )(hb, w_ffn_in, w_ffn_in, w_ffn_out, h, g.reshape(1, D_MODEL), b.reshape(1, D_MODEL))


def kernel(x, w_in, lambda_qk, diff_subln_w, w_branch_a, w_branch_b, w_out,
           ln1_g, ln1_b, w_ffn_in, w_ffn_out, ln2_g, ln2_b):
    batch, seq, _ = x.shape
    depth = w_in.shape[0]
    alpha = (2.0 * depth) ** 0.25
    tables = _rope_tables(seq)
    h = x.reshape(batch * seq, D_MODEL)
    for l in range(depth):
        lam_init = 0.8 - 0.6 * math.exp(-0.3 * l)
        qa_hi, qa_lo, ka, ksum, va_t, qb_t, kb, vb_t, ga, gb = _in_proj(h, w_in[l], tables, seq)
        ya, wa, wb, wo, w_down = _moba(qa_hi, qa_lo, ka, va_t, ksum, batch, seq,
                                       (w_branch_a[l], w_branch_b[l], w_out[l], w_ffn_out[l]))
        yb, w_up = _diff(qb_t, kb, vb_t, lambda_qk[l], diff_subln_w[l], batch, seq, lam_init, (w_ffn_in[l],))
        m = _merge(ya, yb, ga, gb, wa, wb)
        h, hb = _out_ln(m, wo, h, ln1_g[l], ln1_b[l], alpha)
        h = _ffn_ln(hb, h, w_up, w_down, ln2_g[l], ln2_b[l], alpha)
    return h.reshape(batch, seq, D_MODEL)
```

```python
import functools
import math

import jax
import jax.numpy as jnp
import numpy as np
from jax import lax
from jax.experimental import pallas as pl
from jax.experimental.pallas import tpu as pltpu

D_MODEL = 2048
HEADS = 8
HEAD_DIM = 128
WIDTH = HEADS * HEAD_DIM
MOBA_BLOCK = 256
MOBA_TOPK = 3
DIFF_QK_DIM = 64
ROPE_THETA = 10000.0
FFN_HIDDEN = 5632
LN_EPS = 1e-5
RMS_EPS = 1e-5

LANES = 128
SUBLANES = 8
VMEM_LIMIT = 56 * 1024 * 1024
MASKED = -1e30
LOG2E = math.log2(math.e)
KV_BLOCK = 256
V_ROWS = HEAD_DIM + 16

BF16 = jnp.bfloat16
F32 = jnp.float32


def _params(sem):
    return pltpu.CompilerParams(dimension_semantics=sem, vmem_limit_bytes=VMEM_LIMIT)


_N_ROPE_TABLES = 5
_ROPE_STEP = 64


def _rope_tables(seq):
    step = _ROPE_STEP
    assert seq % step == 0
    half_a, half_b = HEAD_DIM // 2, DIFF_QK_DIM // 2
    lane = np.arange(_N_ROPE_TABLES * LANES)
    table, l = lane // LANES, lane % LANES
    freq = np.where(table < 2, l % half_a, 2 * (l % half_b))
    upper = np.where(table < 2, l % HEAD_DIM >= half_a, l % DIFF_QK_DIM >= half_b)
    use_sin = (table == 1) | (table >= 3)
    coef = np.select([table == 1, table == 3, table == 4],
                     [np.where(upper, 1.0, -1.0),
                      np.where(upper, 0.0, -1.0),
                      np.where(upper, 1.0, 0.0)],
                     default=1.0).astype(np.float32)
    inv_lane = (ROPE_THETA ** (-jnp.asarray(freq, F32) * 2.0 / HEAD_DIM))[None, :]
    coarse = jnp.arange(seq // step, dtype=F32)[:, None] * float(step) * inv_lane
    fine = jnp.arange(step, dtype=F32)[:, None] * inv_lane
    cc, sc = jnp.cos(coarse), jnp.sin(coarse)
    a_coarse = jnp.where(use_sin, sc, cc) * coef
    b_coarse = jnp.where(use_sin, cc, -sc) * coef
    return a_coarse, b_coarse, jnp.cos(fine), jnp.sin(fine)


_IN_TM = 512
_IN_COLS = 6 * HEAD_DIM + 2 * 2 * HEAD_DIM


def _sigmoid(t):
    return 0.5 * jnp.tanh(0.5 * t) + 0.5


def _in_proj_kernel(x_ref, wqa, wka, wva, wqb, wkb, wvb, wga, wgb,
                    rope_a_ref, rope_b_ref, rope_cos_ref, rope_sin_ref,
                    qa_hi_ref, qa_lo_ref, ka_ref, ksum_ref, va_ref, qb_ref, kb_ref, vb_ref, ga_ref, gb_ref,
                    w_sc):
    def rope_table(j):
        lanes = slice(j * LANES, (j + 1) * LANES)
        fine_cos, fine_sin = rope_cos_ref[:, lanes], rope_sin_ref[:, lanes]
        return jnp.concatenate([rope_a_ref[r:r + 1, lanes] * fine_cos + rope_b_ref[r:r + 1, lanes] * fine_sin
                                for r in range(_IN_TM // _ROPE_STEP)], axis=0)

    @pl.when(pl.program_id(1) == 0)
    def _():
        off = 0
        for w in (wqa, wka, wva, wqb, wkb, wvb, wga, wgb):
            n = w.shape[1]
            w_sc[:, off:off + n] = w[...].astype(BF16)
            off += n

    z = jnp.dot(x_ref[...].astype(BF16), w_sc[...], preferred_element_type=F32)
    d = HEAD_DIM
    blk = KV_BLOCK
    zqa, zka, zva, zqb, zkb, zvb = (z[:, j * d:(j + 1) * d] for j in range(6))
    zga, zgb = z[:, 6 * d:8 * d], z[:, 8 * d:10 * d]

    cos_a, sin_a, cos_b, sin_b_lo, sin_b_hi = (rope_table(j) for j in range(_N_ROPE_TABLES))

    def rope_a(t):
        return t * cos_a + pltpu.roll(t, d // 2, 1) * sin_a

    def rope_b(t):
        return (t * cos_b + pltpu.roll(t, d - DIFF_QK_DIM // 2, 1) * sin_b_lo
                + pltpu.roll(t, DIFF_QK_DIM // 2, 1) * sin_b_hi)

    def store_transposed(ref, t, dtype):
        for j in range(_IN_TM // blk):
            ref[0, j, 0:d] = t[j * blk:(j + 1) * blk, :].T.astype(dtype)
            if ref.shape[2] > d:
                ref[0, j, d:] = jnp.ones((ref.shape[2] - d, blk), dtype)

    qa = rope_a(zqa)
    store_transposed(qa_hi_ref, qa, BF16)
    store_transposed(qa_lo_ref, qa - qa.astype(BF16).astype(F32), BF16)
    ka = rope_a(zka) * (d ** -0.5 * LOG2E)
    ka_ref[...] = ka.astype(BF16)
    for j in range(_IN_TM // MOBA_BLOCK):
        part = ka[j * MOBA_BLOCK:(j + 1) * MOBA_BLOCK].reshape(MOBA_BLOCK // SUBLANES, SUBLANES, d)
        ksum_ref[j * SUBLANES:(j + 1) * SUBLANES, :] = jnp.sum(part, axis=0)
    store_transposed(qb_ref, rope_b(zqb) * (DIFF_QK_DIM ** -0.5 * LOG2E), BF16)
    kb_ref[...] = rope_b(zkb).astype(BF16)
    store_transposed(va_ref, zva, BF16)
    store_transposed(vb_ref, zvb, BF16)
    ga_ref[...] = zga.astype(BF16)
    gb_ref[...] = zgb.astype(BF16)


def _in_proj(x2, w_in, tables, seq):
    t = x2.shape[0]
    tm = _IN_TM
    n_seq_tiles = seq // tm
    grid = (HEADS, t // tm)
    d = HEAD_DIM

    def wspec(group):
        return pl.BlockSpec((D_MODEL, d), lambda h, i, g=group: (0, g * HEADS + h))

    def gspec(base):
        return pl.BlockSpec((D_MODEL, 2 * d), lambda h, i, b=base: (0, b + h))

    rope_lanes = _N_ROPE_TABLES * LANES
    coarse_spec = pl.BlockSpec((tm // _ROPE_STEP, rope_lanes), lambda h, i: (i % n_seq_tiles, 0))
    fine_spec = pl.BlockSpec((_ROPE_STEP, rope_lanes), lambda h, i: (0, 0))
    ospec = pl.BlockSpec((tm, d), lambda h, i: (i, h))
    tr_spec = pl.BlockSpec((1, tm // KV_BLOCK, d, KV_BLOCK), lambda h, i: (h, i, 0, 0))
    gate_ospec = pl.BlockSpec((tm, 2 * d), lambda h, i: (i, h))
    ksum_rows = tm // MOBA_BLOCK * SUBLANES
    tr_shape = (HEADS, t // KV_BLOCK, d, KV_BLOCK)
    v_spec = pl.BlockSpec((1, tm // KV_BLOCK, V_ROWS, KV_BLOCK), lambda h, i: (h, i, 0, 0))
    v_shape = (HEADS, t // KV_BLOCK, V_ROWS, KV_BLOCK)
    out_shape = (
        jax.ShapeDtypeStruct(tr_shape, BF16),
        jax.ShapeDtypeStruct(tr_shape, BF16),
        jax.ShapeDtypeStruct((t, WIDTH), BF16),
        jax.ShapeDtypeStruct((t // MOBA_BLOCK * SUBLANES, WIDTH), F32),
        jax.ShapeDtypeStruct(v_shape, BF16),
        jax.ShapeDtypeStruct(tr_shape, BF16),
        jax.ShapeDtypeStruct((t, WIDTH), BF16),
        jax.ShapeDtypeStruct(v_shape, BF16),
        jax.ShapeDtypeStruct((t, D_MODEL), BF16),
        jax.ShapeDtypeStruct((t, D_MODEL), BF16),
    )
    out_specs = (tr_spec, tr_spec, ospec, pl.BlockSpec((ksum_rows, d), lambda h, i: (i, h)),
                 v_spec, tr_spec, ospec, v_spec, gate_ospec, gate_ospec)
    gate_a_base = 6 * WIDTH // (2 * d)
    gate_b_base = gate_a_base + D_MODEL // (2 * d)
    in_specs = [pl.BlockSpec((tm, D_MODEL), lambda h, i: (i, 0))]
    in_specs += [wspec(g) for g in range(6)]
    in_specs += [gspec(gate_a_base), gspec(gate_b_base)]
    in_specs += [coarse_spec, coarse_spec, fine_spec, fine_spec]
    return pl.pallas_call(
        _in_proj_kernel,
        grid=grid,
        in_specs=in_specs,
        out_specs=out_specs,
        out_shape=out_shape,
        scratch_shapes=[pltpu.VMEM((D_MODEL, _IN_COLS), BF16)],
        compiler_params=_params(("arbitrary", "arbitrary")),
        name="in_proj",
    )(x2, *([w_in] * 8), *tables)


class _Operands:
    def __init__(self, n_sub, n_qgroups, src_of_sub, load_k, load_vt, load_rhs, bias_row=None):
        self.n_sub, self.n_qgroups = n_sub, n_qgroups
        self.src_of_sub, self.load_k, self.load_vt = src_of_sub, load_k, load_vt
        self.load_rhs, self.bias_row = load_rhs, bias_row

    def src(self, c):
        return self.src_of_sub[c // self.n_qgroups]


def _pipeline_stage(n, buf, cur, masks, nxt, ops, acc_ref, s_ref, ms):
    vt_blk = {src: ops.load_vt(n, src) for src in sorted({ops.src(c) for c in cur})}
    k_next = {src: ops.load_k(n + 1, src) for src in sorted({ops.src(c) for c in nxt})}

    def issue_scores(c):
        s_ref[1 - buf, c] = jnp.dot(k_next[ops.src(c)], ops.load_rhs(c), preferred_element_type=F32)

    lead = 2
    for c in nxt[:lead]:
        issue_scores(c)
    rest = nxt[lead:]
    for j, (c, mask) in enumerate(zip(cur, masks)):
        s = s_ref[buf, c]
        if mask is not None:
            s = jnp.where(mask, s, MASKED)
        top = jnp.max(s, axis=0, keepdims=True)
        bias = ops.bias_row(n, c) if (ops.bias_row is not None and mask is None) else None
        if bias is None:
            m_new = jnp.maximum(ms[c], top)
            shift = m_new
        else:
            seen = bias == 0.0
            m_new = jnp.maximum(ms[c], jnp.where(seen, top, MASKED))
            shift = jnp.where(seen, m_new, -MASKED)
        alpha = jnp.exp2(ms[c] - m_new)
        p = jnp.exp2(s - shift).astype(BF16)
        acc_ref[c] = alpha * acc_ref[c] + jnp.dot(vt_blk[ops.src(c)], p, preferred_element_type=F32)
        ms[c] = m_new
        if j < len(rest):
            issue_scores(rest[j])


def _flash_columns(i, ops, blocks_per_iter, acc_ref, s_ref):
    n_qgroups = ops.n_qgroups
    assert blocks_per_iter % 2 == 0
    assert n_qgroups % blocks_per_iter == 0
    n_groups = ops.n_sub * n_qgroups
    blk = KV_BLOCK
    everyone = list(range(n_groups))
    init = tuple(jnp.full((1, blk), MASKED, F32) for _ in range(n_groups))
    for c in everyone:
        acc_ref[c] = jnp.zeros(acc_ref.shape[1:], F32)
    k_first = {src: ops.load_k(0, src) for src in sorted(set(ops.src_of_sub))}
    for c in everyone:
        s_ref[0, c] = jnp.dot(k_first[ops.src(c)], ops.load_rhs(c), preferred_element_type=F32)

    stage = functools.partial(_pipeline_stage, ops=ops, acc_ref=acc_ref, s_ref=s_ref)

    def body(it, carry):
        ms = list(carry)
        for j in range(blocks_per_iter):
            stage(blocks_per_iter * it + j, j % 2, everyone, [None] * n_groups, everyone, ms=ms)
        return tuple(ms)

    n_past = n_qgroups * i
    ms = list(lax.fori_loop(0, (n_qgroups // blocks_per_iter) * i, body, init))
    row = lax.broadcasted_iota(jnp.int32, (blk, blk), 0)
    col = lax.broadcasted_iota(jnp.int32, (blk, blk), 1)
    causal = row <= col
    for t in range(n_qgroups):
        cur = [c for c in everyone if c % n_qgroups >= t]
        nxt = [c for c in everyone if c % n_qgroups >= t + 1]
        masks = [causal if c % n_qgroups == t else None for c in cur]
        stage(n_past + t, t % 2, cur, masks, nxt, ms=ms)


def _normalised(acc):
    return acc[0:HEAD_DIM] / acc[HEAD_DIM:HEAD_DIM + 1]


BF16_ROW_TILE = 16


def _cast_specs(weights, grid):
    n_steps = grid[0] * grid[1] * grid[2]
    in_specs, out_specs, out_shapes = [], [], []
    for w in weights:
        rows, cols = w.shape
        share = 1
        while rows * share % (n_steps * BF16_ROW_TILE):
            share *= 2
            assert share <= n_steps, (rows, n_steps)
        spec = pl.BlockSpec((rows * share // n_steps, cols),
                            lambda b, h, i, s=share: (((b * grid[1] + h) * grid[2] + i) // s, 0))
        in_specs.append(spec)
        out_specs.append(spec)
        out_shapes.append(jax.ShapeDtypeStruct(w.shape, BF16))
    return in_specs, out_specs, out_shapes


def _cast_blocks(src_refs, dst_refs):
    for src, dst in zip(src_refs, dst_refs):
        dst[...] = src[...].astype(BF16)


_MOBA_QGROUPS = 4
_MOBA_HEADS_PER_STEP = 2


def _split_bf16(a):
    hi = a.astype(BF16)
    lo = (a - hi.astype(F32)).astype(BF16)
    return hi, lo


def _moba_kernel(*refs, n_blocks, n_cast):
    q_hi_ref, q_lo_ref, k_ref, vt_ref, ksum_ref = refs[:5]
    o_ref = refs[5 + n_cast]
    bias_sc, acc_sc, s_sc = refs[6 + 2 * n_cast:]
    _cast_blocks(refs[5:5 + n_cast], refs[6 + n_cast:6 + 2 * n_cast])
    i = pl.program_id(2)
    blk = MOBA_BLOCK
    d = HEAD_DIM
    nq = _MOBA_QGROUPS
    row = lax.broadcasted_iota(jnp.int32, (n_blocks, blk), 0).astype(F32)

    for hd in range(_MOBA_HEADS_PER_STEP):
        ksum = ksum_ref[:, hd * d:(hd + 1) * d]
        kmean = jnp.sum(ksum.reshape(n_blocks, SUBLANES, d), axis=1) * (1.0 / blk)
        m_hi, m_lo = _split_bf16(kmean)
        for g in range(nq):
            q_hi, q_lo = q_hi_ref[hd, g], q_lo_ref[hd, g]
            q_blk = (nq * i + g).astype(F32)
            gate = (jnp.dot(m_hi, q_hi, preferred_element_type=F32)
                    + jnp.dot(m_lo, q_hi, preferred_element_type=F32)
                    + jnp.dot(m_hi, q_lo, preferred_element_type=F32))
            avail = row < q_blk
            keep = row == q_blk
            val = jnp.where(avail, gate, -jnp.inf)
            for _ in range(MOBA_TOPK):
                best = jnp.max(val, axis=0, keepdims=True)
                cand = (val == best) & avail
                first = jnp.min(jnp.where(cand, row, float(n_blocks)), axis=0, keepdims=True)
                pick = row == first
                keep = keep | pick
                avail = avail & jnp.logical_not(pick)
                val = jnp.where(pick, -jnp.inf, val)
            bias = jnp.where(keep, 0.0, MASKED)
            for n in range(n_blocks):
                bias_sc[hd * nq + g, n] = bias[n:n + 1, :]

    ops = _Operands(
        n_sub=_MOBA_HEADS_PER_STEP, n_qgroups=nq, src_of_sub=list(range(_MOBA_HEADS_PER_STEP)),
        load_k=lambda n, hd: k_ref[pl.ds(pl.multiple_of(n * blk, blk), blk), hd * d:(hd + 1) * d],
        load_vt=lambda n, hd: vt_ref[hd, n],
        load_rhs=lambda c: q_hi_ref[c // nq, c % nq],
        bias_row=lambda n, c: bias_sc[c, n])
    _flash_columns(i, ops, 4, acc_sc, s_sc)
    for hd in range(_MOBA_HEADS_PER_STEP):
        for g in range(nq):
            o_ref[g * blk:(g + 1) * blk, hd * d:(hd + 1) * d] = _normalised(acc_sc[hd * nq + g]).astype(BF16).T


def _moba(qa_hi, qa_lo, ka, va_t, ksum, batch, seq, weights_to_cast):
    n_blocks = seq // MOBA_BLOCK
    d = HEAD_DIM
    hps = _MOBA_HEADS_PER_STEP
    n_groups = hps * _MOBA_QGROUPS
    tq = _MOBA_QGROUPS * MOBA_BLOCK
    n_tiles = seq // tq
    grid = (batch, HEADS // hps, n_tiles)
    cast_in, cast_out, cast_shapes = _cast_specs(weights_to_cast, grid)
    q_spec = pl.BlockSpec((hps, _MOBA_QGROUPS, d, MOBA_BLOCK), lambda b, h, i: (h, b * n_tiles + i, 0, 0))
    return pl.pallas_call(
        functools.partial(_moba_kernel, n_blocks=n_blocks, n_cast=len(weights_to_cast)),
        grid=grid,
        in_specs=[q_spec, q_spec,
                  pl.BlockSpec((seq, hps * d), lambda b, h, i: (b, h)),
                  pl.BlockSpec((hps, n_blocks, V_ROWS, MOBA_BLOCK), lambda b, h, i: (h, b, 0, 0)),
                  pl.BlockSpec((n_blocks * SUBLANES, hps * d), lambda b, h, i: (b, h))] + cast_in,
        out_specs=[pl.BlockSpec((tq, hps * d), lambda b, h, i: (b * n_tiles + i, h))] + cast_out,
        out_shape=[jax.ShapeDtypeStruct(ka.shape, BF16)] + cast_shapes,
        scratch_shapes=[pltpu.VMEM((n_groups, n_blocks, 1, MOBA_BLOCK), F32),
                        pltpu.VMEM((n_groups, V_ROWS, MOBA_BLOCK), F32),
                        pltpu.VMEM((2, n_groups, KV_BLOCK, MOBA_BLOCK), F32)],
        compiler_params=_params(("arbitrary", "arbitrary", "arbitrary")),
        name="moba",
    )(qa_hi, qa_lo, ka, va_t, ksum, *weights_to_cast)


_DIFF_QGROUPS = 4
_DIFF_HEADS_PER_STEP = 2


def _diff_kernel(*refs, lam_init, n_cast):
    q_ref, k_ref, vt_ref, lam_ref, subln_ref = refs[:5]
    o_ref = refs[5 + n_cast]
    rhs_sc, acc_sc, s_sc = refs[6 + 2 * n_cast:]
    _cast_blocks(refs[5:5 + n_cast], refs[6 + n_cast:6 + 2 * n_cast])
    i = pl.program_id(2)
    blk = KV_BLOCK
    d = HEAD_DIM
    nq = _DIFF_QGROUPS
    hps = _DIFF_HEADS_PER_STEP

    feat = lax.broadcasted_iota(jnp.int32, (d, blk), 0)
    for hd in range(hps):
        for sub in range(2):
            own = (feat < DIFF_QK_DIM) if sub == 0 else (feat >= DIFF_QK_DIM)
            for g in range(nq):
                q_t = q_ref[hd, g]
                rhs_sc[(2 * hd + sub) * nq + g] = jnp.where(own, q_t, jnp.zeros_like(q_t))

    ops = _Operands(
        n_sub=2 * hps, n_qgroups=nq, src_of_sub=[hd for hd in range(hps) for _ in range(2)],
        load_k=lambda n, hd: k_ref[pl.ds(pl.multiple_of(n * blk, blk), blk), hd * d:(hd + 1) * d],
        load_vt=lambda n, hd: vt_ref[hd, n],
        load_rhs=lambda c: rhs_sc[c])
    _flash_columns(i, ops, 4, acc_sc, s_sc)

    lq = lam_ref[...]
    lam = (jnp.exp(jnp.sum(lq[0:1, :] * lq[1:2, :], axis=1, keepdims=True))
           - jnp.exp(jnp.sum(lq[2:3, :] * lq[3:4, :], axis=1, keepdims=True)) + lam_init)
    for hd in range(hps):
        for g in range(nq):
            first, second = acc_sc[2 * hd * nq + g], acc_sc[(2 * hd + 1) * nq + g]
            o = _normalised(first) - lam * _normalised(second)
            o = o * lax.rsqrt(jnp.mean(o * o, axis=0, keepdims=True) + RMS_EPS)
            o_ref[g * blk:(g + 1) * blk, hd * d:(hd + 1) * d] = (
                o.T * subln_ref[...] * (1.0 - lam_init)).astype(BF16)


def _diff(qb_t, kb, vb_t, lambda_qk, subln_w, batch, seq, lam_init, weights_to_cast):
    d = HEAD_DIM
    n_blocks = seq // KV_BLOCK
    hps = _DIFF_HEADS_PER_STEP
    n_groups = 2 * hps * _DIFF_QGROUPS
    tq = _DIFF_QGROUPS * KV_BLOCK
    n_tiles = seq // tq
    grid = (batch, HEADS // hps, n_tiles)
    cast_in, cast_out, cast_shapes = _cast_specs(weights_to_cast, grid)
    return pl.pallas_call(
        functools.partial(_diff_kernel, lam_init=lam_init, n_cast=len(weights_to_cast)),
        grid=grid,
        in_specs=[pl.BlockSpec((hps, _DIFF_QGROUPS, d, KV_BLOCK), lambda b, h, i: (h, b * n_tiles + i, 0, 0)),
                  pl.BlockSpec((seq, hps * d), lambda b, h, i: (b, h)),
                  pl.BlockSpec((hps, n_blocks, V_ROWS, KV_BLOCK), lambda b, h, i: (h, b, 0, 0)),
                  pl.BlockSpec(lambda_qk.shape, lambda b, h, i: (0, 0)),
                  pl.BlockSpec((1, d), lambda b, h, i: (0, 0))] + cast_in,
        out_specs=[pl.BlockSpec((tq, hps * d), lambda b, h, i: (b * n_tiles + i, h))] + cast_out,
        out_shape=[jax.ShapeDtypeStruct(kb.shape, BF16)] + cast_shapes,
        scratch_shapes=[pltpu.VMEM((n_groups, d, KV_BLOCK), BF16),
                        pltpu.VMEM((n_groups, V_ROWS, KV_BLOCK), F32),
                        pltpu.VMEM((2, n_groups, KV_BLOCK, KV_BLOCK), F32)],
        compiler_params=_params(("arbitrary", "arbitrary", "arbitrary")),
        name="diffattn",
    )(qb_t, kb, vb_t, lambda_qk, subln_w.reshape(1, d), *weights_to_cast)


_ROW_TM = 512
_FFN_TH = 512


def _merge_kernel(ya_ref, yb_ref, ga_ref, gb_ref, wa_ref, wb_ref, o_ref):
    a = jnp.dot(ya_ref[...], wa_ref[...], preferred_element_type=F32)
    b = jnp.dot(yb_ref[...], wb_ref[...], preferred_element_type=F32)
    o_ref[...] = (_sigmoid(ga_ref[...].astype(F32)) * a + _sigmoid(gb_ref[...].astype(F32)) * b).astype(BF16)


def _merge(ya, yb, ga, gb, wa, wb):
    t = ya.shape[0]
    tm = _ROW_TM
    row = lambda w: pl.BlockSpec((tm, w), lambda i: (i, 0))
    full = lambda a: pl.BlockSpec(a.shape, lambda i: (0, 0))
    return pl.pallas_call(
        _merge_kernel,
        grid=(t // tm,),
        in_specs=[row(WIDTH), row(WIDTH), row(D_MODEL), row(D_MODEL), full(wa), full(wb)],
        out_specs=row(D_MODEL),
        out_shape=jax.ShapeDtypeStruct((t, D_MODEL), BF16),
        compiler_params=_params(("parallel",)),
        name="merge",
    )(ya, yb, ga, gb, wa, wb)


def _layer_norm(y, g, b):
    mu = jnp.mean(y, axis=1, keepdims=True)
    c = y - mu
    var = jnp.mean(c * c, axis=1, keepdims=True)
    return c * lax.rsqrt(var + LN_EPS) * g + b


def _row_halves(ref):
    half = ref.shape[0] // 2
    return slice(0, half), slice(half, 2 * half)


def _out_ln_kernel(m_ref, w_ref, x_ref, g_ref, b_ref, h_ref, hb_ref, *, alpha):
    halves = _row_halves(m_ref)
    ys = [alpha * x_ref[rows, :] + jnp.dot(m_ref[rows, :], w_ref[...], preferred_element_type=F32)
          for rows in halves]
    for rows, y in zip(halves, ys):
        h = _layer_norm(y, g_ref[...], b_ref[...])
        h_ref[rows, :] = h
        hb_ref[rows, :] = h.astype(BF16)


def _out_ln(m, w_out, x2, g, b, alpha):
    t = m.shape[0]
    tm = _ROW_TM
    row = pl.BlockSpec((tm, D_MODEL), lambda i: (i, 0))
    vec = pl.BlockSpec((1, D_MODEL), lambda i: (0, 0))
    return pl.pallas_call(
        functools.partial(_out_ln_kernel, alpha=alpha),
        grid=(t // tm,),
        in_specs=[row, pl.BlockSpec(w_out.shape, lambda i: (0, 0)), row, vec, vec],
        out_specs=(row, row),
        out_shape=(jax.ShapeDtypeStruct((t, D_MODEL), F32), jax.ShapeDtypeStruct((t, D_MODEL), BF16)),
        compiler_params=_params(("parallel",)),
        name="out_ln",
    )(m, w_out, x2, g.reshape(1, D_MODEL), b.reshape(1, D_MODEL))


def _ffn_kernel(hb_ref, wg_ref, wu_ref, wo_ref, h_ref, g_ref, b_ref, o_ref, *, alpha):
    j = pl.program_id(1)

    @pl.when(j == 0)
    def _():
        o_ref[...] = alpha * h_ref[...]

    hb = hb_ref[...]
    gate = jnp.dot(hb, wg_ref[...], preferred_element_type=F32)
    up = jnp.dot(hb, wu_ref[...], preferred_element_type=F32)
    act = (gate * _sigmoid(gate) * up).astype(BF16)
    o_ref[...] += jnp.dot(act, wo_ref[...], preferred_element_type=F32)

    @pl.when(j == pl.num_programs(1) - 1)
    def _():
        o_ref[...] = _layer_norm(o_ref[...], g_ref[...], b_ref[...])


def _ffn_ln(hb, h, w_ffn_in, w_ffn_out, g, b, alpha):
    t = h.shape[0]
    tm, th = _ROW_TM, _FFN_TH
    nj = FFN_HIDDEN // th
    row = pl.BlockSpec((tm, D_MODEL), lambda i, j: (i, 0))
    vec = pl.BlockSpec((1, D_MODEL), lambda i, j: (0, 0))
    return pl.pallas_call(
        functools.partial(_ffn_kernel, alpha=alpha),
        grid=(t // tm, nj),
        in_specs=[row,
                  pl.BlockSpec((D_MODEL, th), lambda i, j: (0, j)),
                  pl.BlockSpec((D_MODEL, th), lambda i, j: (0, nj + j)),
                  pl.BlockSpec((th, D_MODEL), lambda i, j: (j, 0)),
                  row, vec, vec],
        out_specs=row,
        out_shape=jax.ShapeDtypeStruct((t, D_MODEL), F32),
        compiler_params=_params(("parallel", "arbitrary")),
        name="ffn_ln",
    )(hb, w_ffn_in, w_ffn_in, w_ffn_out, h, g.reshape(1, D_MODEL), b.reshape(1, D_MODEL))


def kernel(x, w_in, lambda_qk, diff_subln_w, w_branch_a, w_branch_b, w_out,
           ln1_g, ln1_b, w_ffn_in, w_ffn_out, ln2_g, ln2_b):
    batch, seq, _ = x.shape
    depth = w_in.shape[0]
    alpha = (2.0 * depth) ** 0.25
    tables = _rope_tables(seq)
    h = x.reshape(batch * seq, D_MODEL)
    for l in range(depth):
        lam_init = 0.8 - 0.6 * math.exp(-0.3 * l)
        qa_hi, qa_lo, ka, ksum, va_t, qb_t, kb, vb_t, ga, gb = _in_proj(h, w_in[l], tables, seq)
        ya, wa, wb, wo, w_down = _moba(qa_hi, qa_lo, ka, va_t, ksum, batch, seq,
                                       (w_branch_a[l], w_branch_b[l], w_out[l], w_ffn_out[l]))
        yb, w_up = _diff(qb_t, kb, vb_t, lambda_qk[l], diff_subln_w[l], batch, seq, lam_init, (w_ffn_in[l],))
        m = _merge(ya, yb, ga, gb, wa, wb)
        h, hb = _out_ln(m, wo, h, ln1_g[l], ln1_b[l], alpha)
        h = _ffn_ln(hb, h, w_up, w_down, ln2_g[l], ln2_b[l], alpha)
    return h.reshape(batch, seq, D_MODEL)
```

```python
import functools
import math

import jax
import jax.numpy as jnp
import numpy as np
from jax import lax
from jax.experimental import pallas as pl
from jax.experimental.pallas import tpu as pltpu

D_MODEL = 2048
HEADS = 8
HEAD_DIM = 128
WIDTH = HEADS * HEAD_DIM
MOBA_BLOCK = 256
MOBA_TOPK = 3
DIFF_QK_DIM = 64
ROPE_THETA = 10000.0
FFN_HIDDEN = 5632
LN_EPS = 1e-5
RMS_EPS = 1e-5

LANES = 128
SUBLANES = 8
VMEM_LIMIT = 56 * 1024 * 1024
MASKED = -1e30
LOG2E = math.log2(math.e)
KV_BLOCK = 256
BF16_ROW_TILE = 16
V_ROWS = HEAD_DIM + BF16_ROW_TILE
BLOCKS_PER_ITER = 4
SCORE_LEAD = 2

BF16 = jnp.bfloat16
F32 = jnp.float32


def _params(sem):
    return pltpu.CompilerParams(dimension_semantics=sem, vmem_limit_bytes=VMEM_LIMIT)


_N_ROPE_TABLES = 5
_ROPE_STEP = 64


def _rope_tables(seq):
    step = _ROPE_STEP
    assert seq % step == 0
    half_a, half_b = HEAD_DIM // 2, DIFF_QK_DIM // 2
    lane = np.arange(_N_ROPE_TABLES * LANES)
    table, l = lane // LANES, lane % LANES
    freq = np.where(table < 2, l % half_a, 2 * (l % half_b))
    upper = np.where(table < 2, l % HEAD_DIM >= half_a, l % DIFF_QK_DIM >= half_b)
    use_sin = (table == 1) | (table >= 3)
    coef = np.select([table == 1, table == 3, table == 4],
                     [np.where(upper, 1.0, -1.0),
                      np.where(upper, 0.0, -1.0),
                      np.where(upper, 1.0, 0.0)],
                     default=1.0).astype(np.float32)
    inv_lane = (ROPE_THETA ** (-jnp.asarray(freq, F32) * 2.0 / HEAD_DIM))[None, :]
    coarse = jnp.arange(seq // step, dtype=F32)[:, None] * float(step) * inv_lane
    fine = jnp.arange(step, dtype=F32)[:, None] * inv_lane
    cc, sc = jnp.cos(coarse), jnp.sin(coarse)
    a_coarse = jnp.where(use_sin, sc, cc) * coef
    b_coarse = jnp.where(use_sin, cc, -sc) * coef
    return a_coarse, b_coarse, jnp.cos(fine), jnp.sin(fine)


_IN_TM = 512
_IN_COLS = 6 * HEAD_DIM + 2 * 2 * HEAD_DIM


def _sigmoid(t):
    return 0.5 * jnp.tanh(0.5 * t) + 0.5


def _in_proj_kernel(x_ref, wqa, wka, wva, wqb, wkb, wvb, wga, wgb,
                    rope_a_ref, rope_b_ref, rope_cos_ref, rope_sin_ref,
                    qa_hi_ref, qa_lo_ref, ka_ref, ksum_ref, va_ref, qb_ref, kb_ref, vb_ref, ga_ref, gb_ref,
                    w_sc):
    def rope_table(j):
        lanes = slice(j * LANES, (j + 1) * LANES)
        fine_cos, fine_sin = rope_cos_ref[:, lanes], rope_sin_ref[:, lanes]
        return jnp.concatenate([rope_a_ref[r:r + 1, lanes] * fine_cos + rope_b_ref[r:r + 1, lanes] * fine_sin
                                for r in range(_IN_TM // _ROPE_STEP)], axis=0)

    @pl.when(pl.program_id(1) == 0)
    def _():
        off = 0
        for w in (wqa, wka, wva, wqb, wkb, wvb, wga, wgb):
            n = w.shape[1]
            w_sc[:, off:off + n] = w[...].astype(BF16)
            off += n

    z = jnp.dot(x_ref[...].astype(BF16), w_sc[...], preferred_element_type=F32)
    d = HEAD_DIM
    blk = KV_BLOCK
    zqa, zka, zva, zqb, zkb, zvb = (z[:, j * d:(j + 1) * d] for j in range(6))
    zga, zgb = z[:, 6 * d:8 * d], z[:, 8 * d:10 * d]

    cos_a, sin_a, cos_b, sin_b_lo, sin_b_hi = (rope_table(j) for j in range(_N_ROPE_TABLES))

    def rope_a(t):
        return t * cos_a + pltpu.roll(t, d // 2, 1) * sin_a

    def rope_b(t):
        return (t * cos_b + pltpu.roll(t, d - DIFF_QK_DIM // 2, 1) * sin_b_lo
                + pltpu.roll(t, DIFF_QK_DIM // 2, 1) * sin_b_hi)

    def store_transposed(ref, t, dtype):
        for j in range(_IN_TM // blk):
            ref[0, j, 0:d] = t[j * blk:(j + 1) * blk, :].T.astype(dtype)
            if ref.shape[2] > d:
                ref[0, j, d:] = jnp.ones((ref.shape[2] - d, blk), dtype)

    qa = rope_a(zqa)
    store_transposed(qa_hi_ref, qa, BF16)
    store_transposed(qa_lo_ref, qa - qa.astype(BF16).astype(F32), BF16)
    ka = rope_a(zka) * (d ** -0.5 * LOG2E)
    ka_ref[...] = ka.astype(BF16)
    for j in range(_IN_TM // MOBA_BLOCK):
        part = ka[j * MOBA_BLOCK:(j + 1) * MOBA_BLOCK].reshape(MOBA_BLOCK // SUBLANES, SUBLANES, d)
        ksum_ref[j * SUBLANES:(j + 1) * SUBLANES, :] = jnp.sum(part, axis=0)
    store_transposed(qb_ref, rope_b(zqb) * (DIFF_QK_DIM ** -0.5 * LOG2E), BF16)
    kb_ref[...] = rope_b(zkb).astype(BF16)
    store_transposed(va_ref, zva, BF16)
    store_transposed(vb_ref, zvb, BF16)
    ga_ref[...] = zga.astype(BF16)
    gb_ref[...] = zgb.astype(BF16)


def _in_proj(x2, w_in, tables, seq):
    t = x2.shape[0]
    tm = _IN_TM
    n_seq_tiles = seq // tm
    grid = (HEADS, t // tm)
    d = HEAD_DIM

    def wspec(group):
        return pl.BlockSpec((D_MODEL, d), lambda h, i, g=group: (0, g * HEADS + h))

    def gspec(base):
        return pl.BlockSpec((D_MODEL, 2 * d), lambda h, i, b=base: (0, b + h))

    rope_lanes = _N_ROPE_TABLES * LANES
    coarse_spec = pl.BlockSpec((tm // _ROPE_STEP, rope_lanes), lambda h, i: (i % n_seq_tiles, 0))
    fine_spec = pl.BlockSpec((_ROPE_STEP, rope_lanes), lambda h, i: (0, 0))
    ospec = pl.BlockSpec((tm, d), lambda h, i: (i, h))
    tr_spec = pl.BlockSpec((1, tm // KV_BLOCK, d, KV_BLOCK), lambda h, i: (h, i, 0, 0))
    gate_ospec = pl.BlockSpec((tm, 2 * d), lambda h, i: (i, h))
    ksum_rows = tm // MOBA_BLOCK * SUBLANES
    tr_shape = (HEADS, t // KV_BLOCK, d, KV_BLOCK)
    v_spec = pl.BlockSpec((1, tm // KV_BLOCK, V_ROWS, KV_BLOCK), lambda h, i: (h, i, 0, 0))
    v_shape = (HEADS, t // KV_BLOCK, V_ROWS, KV_BLOCK)
    out_shape = (
        jax.ShapeDtypeStruct(tr_shape, BF16),
        jax.ShapeDtypeStruct(tr_shape, BF16),
        jax.ShapeDtypeStruct((t, WIDTH), BF16),
        jax.ShapeDtypeStruct((t // MOBA_BLOCK * SUBLANES, WIDTH), F32),
        jax.ShapeDtypeStruct(v_shape, BF16),
        jax.ShapeDtypeStruct(tr_shape, BF16),
        jax.ShapeDtypeStruct((t, WIDTH), BF16),
        jax.ShapeDtypeStruct(v_shape, BF16),
        jax.ShapeDtypeStruct((t, D_MODEL), BF16),
        jax.ShapeDtypeStruct((t, D_MODEL), BF16),
    )
    out_specs = (tr_spec, tr_spec, ospec, pl.BlockSpec((ksum_rows, d), lambda h, i: (i, h)),
                 v_spec, tr_spec, ospec, v_spec, gate_ospec, gate_ospec)
    gate_a_base = 6 * WIDTH // (2 * d)
    gate_b_base = gate_a_base + D_MODEL // (2 * d)
    in_specs = [pl.BlockSpec((tm, D_MODEL), lambda h, i: (i, 0))]
    in_specs += [wspec(g) for g in range(6)]
    in_specs += [gspec(gate_a_base), gspec(gate_b_base)]
    in_specs += [coarse_spec, coarse_spec, fine_spec, fine_spec]
    return pl.pallas_call(
        _in_proj_kernel,
        grid=grid,
        in_specs=in_specs,
        out_specs=out_specs,
        out_shape=out_shape,
        scratch_shapes=[pltpu.VMEM((D_MODEL, _IN_COLS), BF16)],
        compiler_params=_params(("arbitrary", "arbitrary")),
        name="in_proj",
    )(x2, *([w_in] * 8), *tables)


class _Operands:
    def __init__(self, n_sub, n_qgroups, src_of_sub, load_k, load_vt, load_rhs, bias_row=None):
        self.n_sub, self.n_qgroups = n_sub, n_qgroups
        self.src_of_sub, self.load_k, self.load_vt = src_of_sub, load_k, load_vt
        self.load_rhs, self.bias_row = load_rhs, bias_row

    def src(self, c):
        return self.src_of_sub[c // self.n_qgroups]


def _pipeline_stage(n, buf, cur, masks, nxt, ops, acc_ref, s_ref, ms):
    vt_blk = {src: ops.load_vt(n, src) for src in sorted({ops.src(c) for c in cur})}
    k_next = {src: ops.load_k(n + 1, src) for src in sorted({ops.src(c) for c in nxt})}

    def issue_scores(c):
        s_ref[1 - buf, c] = jnp.dot(k_next[ops.src(c)], ops.load_rhs(c), preferred_element_type=F32)

    for c in nxt[:SCORE_LEAD]:
        issue_scores(c)
    rest = nxt[SCORE_LEAD:]
    for j, (c, mask) in enumerate(zip(cur, masks)):
        s = s_ref[buf, c]
        if mask is not None:
            s = jnp.where(mask, s, MASKED)
        top = jnp.max(s, axis=0, keepdims=True)
        bias = ops.bias_row(n, c) if (ops.bias_row is not None and mask is None) else None
        if bias is None:
            m_new = jnp.maximum(ms[c], top)
            shift = m_new
        else:
            seen = bias == 0.0
            m_new = jnp.maximum(ms[c], jnp.where(seen, top, MASKED))
            shift = jnp.where(seen, m_new, -MASKED)
        alpha = jnp.exp2(ms[c] - m_new)
        p = jnp.exp2(s - shift).astype(BF16)
        acc_ref[c] = alpha * acc_ref[c] + jnp.dot(vt_blk[ops.src(c)], p, preferred_element_type=F32)
        ms[c] = m_new
        if j < len(rest):
            issue_scores(rest[j])


def _flash_columns(i, ops, blocks_per_iter, acc_ref, s_ref):
    n_qgroups = ops.n_qgroups
    assert blocks_per_iter % 2 == 0
    assert n_qgroups % blocks_per_iter == 0
    n_groups = ops.n_sub * n_qgroups
    blk = KV_BLOCK
    everyone = list(range(n_groups))
    init = tuple(jnp.full((1, blk), MASKED, F32) for _ in range(n_groups))
    for c in everyone:
        acc_ref[c] = jnp.zeros(acc_ref.shape[1:], F32)
    k_first = {src: ops.load_k(0, src) for src in sorted(set(ops.src_of_sub))}
    for c in everyone:
        s_ref[0, c] = jnp.dot(k_first[ops.src(c)], ops.load_rhs(c), preferred_element_type=F32)

    stage = functools.partial(_pipeline_stage, ops=ops, acc_ref=acc_ref, s_ref=s_ref)

    def body(it, carry):
        ms = list(carry)
        for j in range(blocks_per_iter):
            stage(blocks_per_iter * it + j, j % 2, everyone, [None] * n_groups, everyone, ms=ms)
        return tuple(ms)

    n_past = n_qgroups * i
    ms = list(lax.fori_loop(0, (n_qgroups // blocks_per_iter) * i, body, init))
    row = lax.broadcasted_iota(jnp.int32, (blk, blk), 0)
    col = lax.broadcasted_iota(jnp.int32, (blk, blk), 1)
    causal = row <= col
    for t in range(n_qgroups):
        cur = [c for c in everyone if c % n_qgroups >= t]
        nxt = [c for c in everyone if c % n_qgroups >= t + 1]
        masks = [causal if c % n_qgroups == t else None for c in cur]
        stage(n_past + t, t % 2, cur, masks, nxt, ms=ms)


def _normalised(acc):
    return acc[0:HEAD_DIM] / acc[HEAD_DIM:HEAD_DIM + 1]


def _cast_specs(weights, grid):
    n_steps = grid[0] * grid[1] * grid[2]
    in_specs, out_specs, out_shapes = [], [], []
    for w in weights:
        rows, cols = w.shape
        share = 1
        while rows * share % (n_steps * BF16_ROW_TILE):
            share *= 2
            assert share <= n_steps, (rows, n_steps)
        spec = pl.BlockSpec((rows * share // n_steps, cols),
                            lambda b, h, i, s=share: (((b * grid[1] + h) * grid[2] + i) // s, 0))
        in_specs.append(spec)
        out_specs.append(spec)
        out_shapes.append(jax.ShapeDtypeStruct(w.shape, BF16))
    return in_specs, out_specs, out_shapes


def _cast_blocks(src_refs, dst_refs):
    for src, dst in zip(src_refs, dst_refs):
        dst[...] = src[...].astype(BF16)


_MOBA_QGROUPS = 4
_MOBA_HEADS_PER_STEP = 4


def _split_bf16(a):
    hi = a.astype(BF16)
    lo = (a - hi.astype(F32)).astype(BF16)
    return hi, lo


def _moba_kernel(*refs, n_blocks, n_cast):
    q_hi_ref, q_lo_ref, k_ref, vt_ref, ksum_ref = refs[:5]
    o_ref = refs[5 + n_cast]
    bias_sc, acc_sc, s_sc = refs[6 + 2 * n_cast:]
    _cast_blocks(refs[5:5 + n_cast], refs[6 + n_cast:6 + 2 * n_cast])
    i = pl.program_id(2)
    blk = MOBA_BLOCK
    d = HEAD_DIM
    nq = _MOBA_QGROUPS
    row = lax.broadcasted_iota(jnp.int32, (n_blocks, blk), 0).astype(F32)

    for hd in range(_MOBA_HEADS_PER_STEP):
        ksum = ksum_ref[:, hd * d:(hd + 1) * d]
        kmean = jnp.sum(ksum.reshape(n_blocks, SUBLANES, d), axis=1) * (1.0 / blk)
        m_hi, m_lo = _split_bf16(kmean)
        for g in range(nq):
            q_hi, q_lo = q_hi_ref[hd, g], q_lo_ref[hd, g]
            q_blk = (nq * i + g).astype(F32)
            gate = (jnp.dot(m_hi, q_hi, preferred_element_type=F32)
                    + jnp.dot(m_lo, q_hi, preferred_element_type=F32)
                    + jnp.dot(m_hi, q_lo, preferred_element_type=F32))
            avail = row < q_blk
            keep = row == q_blk
            val = jnp.where(avail, gate, -jnp.inf)
            for _ in range(MOBA_TOPK):
                best = jnp.max(val, axis=0, keepdims=True)
                cand = (val == best) & avail
                first = jnp.min(jnp.where(cand, row, float(n_blocks)), axis=0, keepdims=True)
                pick = row == first
                keep = keep | pick
                avail = avail & jnp.logical_not(pick)
                val = jnp.where(pick, -jnp.inf, val)
            bias = jnp.where(keep, 0.0, MASKED)
            for n in range(n_blocks):
                bias_sc[hd * nq + g, n] = bias[n:n + 1, :]

    ops = _Operands(
        n_sub=_MOBA_HEADS_PER_STEP, n_qgroups=nq, src_of_sub=list(range(_MOBA_HEADS_PER_STEP)),
        load_k=lambda n, hd: k_ref[pl.ds(pl.multiple_of(n * blk, blk), blk), hd * d:(hd + 1) * d],
        load_vt=lambda n, hd: vt_ref[hd, n],
        load_rhs=lambda c: q_hi_ref[c // nq, c % nq],
        bias_row=lambda n, c: bias_sc[c, n])
    _flash_columns(i, ops, BLOCKS_PER_ITER, acc_sc, s_sc)
    for hd in range(_MOBA_HEADS_PER_STEP):
        for g in range(nq):
            o_ref[g * blk:(g + 1) * blk, hd * d:(hd + 1) * d] = _normalised(acc_sc[hd * nq + g]).astype(BF16).T


def _moba(qa_hi, qa_lo, ka, va_t, ksum, batch, seq, weights_to_cast):
    n_blocks = seq // MOBA_BLOCK
    d = HEAD_DIM
    hps = _MOBA_HEADS_PER_STEP
    n_groups = hps * _MOBA_QGROUPS
    tq = _MOBA_QGROUPS * MOBA_BLOCK
    n_tiles = seq // tq
    grid = (batch, HEADS // hps, n_tiles)
    cast_in, cast_out, cast_shapes = _cast_specs(weights_to_cast, grid)
    q_spec = pl.BlockSpec((hps, _MOBA_QGROUPS, d, MOBA_BLOCK), lambda b, h, i: (h, b * n_tiles + i, 0, 0))
    return pl.pallas_call(
        functools.partial(_moba_kernel, n_blocks=n_blocks, n_cast=len(weights_to_cast)),
        grid=grid,
        in_specs=[q_spec, q_spec,
                  pl.BlockSpec((seq, hps * d), lambda b, h, i: (b, h)),
                  pl.BlockSpec((hps, n_blocks, V_ROWS, MOBA_BLOCK), lambda b, h, i: (h, b, 0, 0)),
                  pl.BlockSpec((n_blocks * SUBLANES, hps * d), lambda b, h, i: (b, h))] + cast_in,
        out_specs=[pl.BlockSpec((tq, hps * d), lambda b, h, i: (b * n_tiles + i, h))] + cast_out,
        out_shape=[jax.ShapeDtypeStruct(ka.shape, BF16)] + cast_shapes,
        scratch_shapes=[pltpu.VMEM((n_groups, n_blocks, 1, MOBA_BLOCK), F32),
                        pltpu.VMEM((n_groups, V_ROWS, MOBA_BLOCK), F32),
                        pltpu.VMEM((2, n_groups, KV_BLOCK, MOBA_BLOCK), F32)],
        compiler_params=_params(("arbitrary", "arbitrary", "arbitrary")),
        name="moba",
    )(qa_hi, qa_lo, ka, va_t, ksum, *weights_to_cast)


_DIFF_QGROUPS = 4
_DIFF_HEADS_PER_STEP = 2


def _diff_kernel(*refs, lam_init, n_cast):
    q_ref, k_ref, vt_ref, lam_ref, subln_ref = refs[:5]
    o_ref = refs[5 + n_cast]
    rhs_sc, acc_sc, s_sc = refs[6 + 2 * n_cast:]
    _cast_blocks(refs[5:5 + n_cast], refs[6 + n_cast:6 + 2 * n_cast])
    i = pl.program_id(2)
    blk = KV_BLOCK
    d = HEAD_DIM
    nq = _DIFF_QGROUPS
    hps = _DIFF_HEADS_PER_STEP

    feat = lax.broadcasted_iota(jnp.int32, (d, blk), 0)
    for hd in range(hps):
        for sub in range(2):
            own = (feat < DIFF_QK_DIM) if sub == 0 else (feat >= DIFF_QK_DIM)
            for g in range(nq):
                q_t = q_ref[hd, g]
                rhs_sc[(2 * hd + sub) * nq + g] = jnp.where(own, q_t, jnp.zeros_like(q_t))

    ops = _Operands(
        n_sub=2 * hps, n_qgroups=nq, src_of_sub=[hd for hd in range(hps) for _ in range(2)],
        load_k=lambda n, hd: k_ref[pl.ds(pl.multiple_of(n * blk, blk), blk), hd * d:(hd + 1) * d],
        load_vt=lambda n, hd: vt_ref[hd, n],
        load_rhs=lambda c: rhs_sc[c])
    _flash_columns(i, ops, BLOCKS_PER_ITER, acc_sc, s_sc)

    lq = lam_ref[...]
    lam = (jnp.exp(jnp.sum(lq[0:1, :] * lq[1:2, :], axis=1, keepdims=True))
           - jnp.exp(jnp.sum(lq[2:3, :] * lq[3:4, :], axis=1, keepdims=True)) + lam_init)
    for hd in range(hps):
        for g in range(nq):
            first, second = acc_sc[2 * hd * nq + g], acc_sc[(2 * hd + 1) * nq + g]
            o = _normalised(first) - lam * _normalised(second)
            o = o * lax.rsqrt(jnp.mean(o * o, axis=0, keepdims=True) + RMS_EPS)
            o_ref[g * blk:(g + 1) * blk, hd * d:(hd + 1) * d] = (
                o.T * subln_ref[...] * (1.0 - lam_init)).astype(BF16)


def _diff(qb_t, kb, vb_t, lambda_qk, subln_w, batch, seq, lam_init, weights_to_cast):
    d = HEAD_DIM
    n_blocks = seq // KV_BLOCK
    hps = _DIFF_HEADS_PER_STEP
    n_groups = 2 * hps * _DIFF_QGROUPS
    tq = _DIFF_QGROUPS * KV_BLOCK
    n_tiles = seq // tq
    grid = (batch, HEADS // hps, n_tiles)
    cast_in, cast_out, cast_shapes = _cast_specs(weights_to_cast, grid)
    return pl.pallas_call(
        functools.partial(_diff_kernel, lam_init=lam_init, n_cast=len(weights_to_cast)),
        grid=grid,
        in_specs=[pl.BlockSpec((hps, _DIFF_QGROUPS, d, KV_BLOCK), lambda b, h, i: (h, b * n_tiles + i, 0, 0)),
                  pl.BlockSpec((seq, hps * d), lambda b, h, i: (b, h)),
                  pl.BlockSpec((hps, n_blocks, V_ROWS, KV_BLOCK), lambda b, h, i: (h, b, 0, 0)),
                  pl.BlockSpec(lambda_qk.shape, lambda b, h, i: (0, 0)),
                  pl.BlockSpec((1, d), lambda b, h, i: (0, 0))] + cast_in,
        out_specs=[pl.BlockSpec((tq, hps * d), lambda b, h, i: (b * n_tiles + i, h))] + cast_out,
        out_shape=[jax.ShapeDtypeStruct(kb.shape, BF16)] + cast_shapes,
        scratch_shapes=[pltpu.VMEM((n_groups, d, KV_BLOCK), BF16),
                        pltpu.VMEM((n_groups, V_ROWS, KV_BLOCK), F32),
                        pltpu.VMEM((2, n_groups, KV_BLOCK, KV_BLOCK), F32)],
        compiler_params=_params(("arbitrary", "arbitrary", "arbitrary")),
        name="diffattn",
    )(qb_t, kb, vb_t, lambda_qk, subln_w.reshape(1, d), *weights_to_cast)


_ROW_TM = 512
_FFN_TH = 512


def _merge_kernel(ya_ref, yb_ref, ga_ref, gb_ref, wa_ref, wb_ref, o_ref):
    a = jnp.dot(ya_ref[...], wa_ref[...], preferred_element_type=F32)
    b = jnp.dot(yb_ref[...], wb_ref[...], preferred_element_type=F32)
    o_ref[...] = (_sigmoid(ga_ref[...].astype(F32)) * a + _sigmoid(gb_ref[...].astype(F32)) * b).astype(BF16)


def _merge(ya, yb, ga, gb, wa, wb):
    t = ya.shape[0]
    tm = _ROW_TM
    row = lambda w: pl.BlockSpec((tm, w), lambda i: (i, 0))
    full = lambda a: pl.BlockSpec(a.shape, lambda i: (0, 0))
    return pl.pallas_call(
        _merge_kernel,
        grid=(t // tm,),
        in_specs=[row(WIDTH), row(WIDTH), row(D_MODEL), row(D_MODEL), full(wa), full(wb)],
        out_specs=row(D_MODEL),
        out_shape=jax.ShapeDtypeStruct((t, D_MODEL), BF16),
        compiler_params=_params(("parallel",)),
        name="merge",
    )(ya, yb, ga, gb, wa, wb)


def _layer_norm(y, g, b):
    mu = jnp.mean(y, axis=1, keepdims=True)
    c = y - mu
    var = jnp.mean(c * c, axis=1, keepdims=True)
    return c * lax.rsqrt(var + LN_EPS) * g + b


def _row_halves(ref):
    half = ref.shape[0] // 2
    return slice(0, half), slice(half, 2 * half)


def _out_ln_kernel(m_ref, w_ref, x_ref, g_ref, b_ref, h_ref, hb_ref, *, alpha):
    halves = _row_halves(m_ref)
    ys = [alpha * x_ref[rows, :] + jnp.dot(m_ref[rows, :], w_ref[...], preferred_element_type=F32)
          for rows in halves]
    for rows, y in zip(halves, ys):
        h = _layer_norm(y, g_ref[...], b_ref[...])
        h_ref[rows, :] = h
        hb_ref[rows, :] = h.astype(BF16)


def _out_ln(m, w_out, x2, g, b, alpha):
    t = m.shape[0]
    tm = _ROW_TM
    row = pl.BlockSpec((tm, D_MODEL), lambda i: (i, 0))
    vec = pl.BlockSpec((1, D_MODEL), lambda i: (0, 0))
    return pl.pallas_call(
        functools.partial(_out_ln_kernel, alpha=alpha),
        grid=(t // tm,),
        in_specs=[row, pl.BlockSpec(w_out.shape, lambda i: (0, 0)), row, vec, vec],
        out_specs=(row, row),
        out_shape=(jax.ShapeDtypeStruct((t, D_MODEL), F32), jax.ShapeDtypeStruct((t, D_MODEL), BF16)),
        compiler_params=_params(("parallel",)),
        name="out_ln",
    )(m, w_out, x2, g.reshape(1, D_MODEL), b.reshape(1, D_MODEL))


def _ffn_kernel(hb_ref, wg_ref, wu_ref, wo_ref, h_ref, g_ref, b_ref, o_ref, *, alpha):
    j = pl.program_id(1)

    @pl.when(j == 0)
    def _():
        o_ref[...] = alpha * h_ref[...]

    hb = hb_ref[...]
    gate = jnp.dot(hb, wg_ref[...], preferred_element_type=F32)
    up = jnp.dot(hb, wu_ref[...], preferred_element_type=F32)
    act = (gate * _sigmoid(gate) * up).astype(BF16)
    o_ref[...] += jnp.dot(act, wo_ref[...], preferred_element_type=F32)

    @pl.when(j == pl.num_programs(1) - 1)
    def _():
        o_ref[...] = _layer_norm(o_ref[...], g_ref[...], b_ref[...])


def _ffn_ln(hb, h, w_ffn_in, w_ffn_out, g, b, alpha):
    t = h.shape[0]
    tm, th = _ROW_TM, _FFN_TH
    nj = FFN_HIDDEN // th
    row = pl.BlockSpec((tm, D_MODEL), lambda i, j: (i, 0))
    vec = pl.BlockSpec((1, D_MODEL), lambda i, j: (0, 0))
    return pl.pallas_call(
        functools.partial(_ffn_kernel, alpha=alpha),
        grid=(t // tm, nj),
        in_specs=[row,
                  pl.BlockSpec((D_MODEL, th), lambda i, j: (0, j)),
                  pl.BlockSpec((D_MODEL, th), lambda i, j: (0, nj + j)),
                  pl.BlockSpec((th, D_MODEL), lambda i, j: (j, 0)),
                  row, vec, vec],
        out_specs=row,
        out_shape=jax.ShapeDtypeStruct((t, D_MODEL), F32),
        compiler_params=_params(("parallel", "arbitrary")),
        name="ffn_ln",
    )(hb, w_ffn_in, w_ffn_in, w_ffn_out, h, g.reshape(1, D_MODEL), b.reshape(1, D_MODEL))


def kernel(x, w_in, lambda_qk, diff_subln_w, w_branch_a, w_branch_b, w_out,
           ln1_g, ln1_b, w_ffn_in, w_ffn_out, ln2_g, ln2_b):
    batch, seq, _ = x.shape
    depth = w_in.shape[0]
    alpha = (2.0 * depth) ** 0.25
    tables = _rope_tables(seq)
    h = x.reshape(batch * seq, D_MODEL)
    for l in range(depth):
        lam_init = 0.8 - 0.6 * math.exp(-0.3 * l)
        qa_hi, qa_lo, ka, ksum, va_t, qb_t, kb, vb_t, ga, gb = _in_proj(h, w_in[l], tables, seq)
        ya, wa, wb, wo, w_down = _moba(qa_hi, qa_lo, ka, va_t, ksum, batch, seq,
                                       (w_branch_a[l], w_branch_b[l], w_out[l], w_ffn_out[l]))
        yb, w_up = _diff(qb_t, kb, vb_t, lambda_qk[l], diff_subln_w[l], batch, seq, lam_init, (w_ffn_in[l],))
        m = _merge(ya, yb, ga, gb, wa, wb)
        h, hb = _out_ln(m, wo, h, ln1_g[l], ln1_b[l], alpha)
        h = _ffn_ln(hb, h, w_up, w_down, ln2_g[l], ln2_b[l], alpha)
    return h.reshape(batch, seq, D_MODEL)
```

```python
import functools
import math

import jax
import jax.numpy as jnp
import numpy as np
from jax import lax
from jax.experimental import pallas as pl
from jax.experimental.pallas import tpu as pltpu

D_MODEL = 2048
HEADS = 8
HEAD_DIM = 128
WIDTH = HEADS * HEAD_DIM
MOBA_BLOCK = 256
MOBA_TOPK = 3
DIFF_QK_DIM = 64
ROPE_THETA = 10000.0
FFN_HIDDEN = 5632
LN_EPS = 1e-5
RMS_EPS = 1e-5

LANES = 128
SUBLANES = 8
VMEM_LIMIT = 56 * 1024 * 1024
MASKED = -1e30
LOG2E = math.log2(math.e)
KV_BLOCK = 256
BF16_ROW_TILE = 16
V_ROWS = HEAD_DIM + BF16_ROW_TILE
BLOCKS_PER_ITER = 4
SCORE_LEAD = 2

BF16 = jnp.bfloat16
F32 = jnp.float32


def _params(sem):
    return pltpu.CompilerParams(dimension_semantics=sem, vmem_limit_bytes=VMEM_LIMIT)


_N_ROPE_TABLES = 5
_ROPE_STEP = 64


def _rope_tables(seq):
    step = _ROPE_STEP
    assert seq % step == 0
    half_a, half_b = HEAD_DIM // 2, DIFF_QK_DIM // 2
    lane = np.arange(_N_ROPE_TABLES * LANES)
    table, l = lane // LANES, lane % LANES
    freq = np.where(table < 2, l % half_a, 2 * (l % half_b))
    upper = np.where(table < 2, l % HEAD_DIM >= half_a, l % DIFF_QK_DIM >= half_b)
    use_sin = (table == 1) | (table >= 3)
    coef = np.select([table == 1, table == 3, table == 4],
                     [np.where(upper, 1.0, -1.0),
                      np.where(upper, 0.0, -1.0),
                      np.where(upper, 1.0, 0.0)],
                     default=1.0).astype(np.float32)
    inv_lane = (ROPE_THETA ** (-jnp.asarray(freq, F32) * 2.0 / HEAD_DIM))[None, :]
    coarse = jnp.arange(seq // step, dtype=F32)[:, None] * float(step) * inv_lane
    fine = jnp.arange(step, dtype=F32)[:, None] * inv_lane
    cc, sc = jnp.cos(coarse), jnp.sin(coarse)
    a_coarse = jnp.where(use_sin, sc, cc) * coef
    b_coarse = jnp.where(use_sin, cc, -sc) * coef
    return a_coarse, b_coarse, jnp.cos(fine), jnp.sin(fine)


_IN_TM = 512
_IN_COLS = 6 * HEAD_DIM + 2 * 2 * HEAD_DIM


def _sigmoid(t):
    return 0.5 * jnp.tanh(0.5 * t) + 0.5


_IN_PROJ_INPUTS = 13
_IN_PROJ_OUTPUTS = 10


def _in_proj_kernel(*refs, n_cast):
    x_ref, wqa, wka, wva, wqb, wkb, wvb, wga, wgb = refs[:9]
    rope_a_ref, rope_b_ref, rope_cos_ref, rope_sin_ref = refs[9:_IN_PROJ_INPUTS]
    outs = refs[_IN_PROJ_INPUTS + n_cast:_IN_PROJ_INPUTS + n_cast + _IN_PROJ_OUTPUTS]
    qa_hi_ref, qa_lo_ref, ka_ref, ksum_ref, va_ref, qb_ref, kb_ref, vb_ref, ga_ref, gb_ref = outs
    w_sc = refs[-1]
    _cast_blocks(refs[_IN_PROJ_INPUTS:_IN_PROJ_INPUTS + n_cast], refs[_IN_PROJ_INPUTS + n_cast + _IN_PROJ_OUTPUTS:-1])

    def rope_table(j):
        lanes = slice(j * LANES, (j + 1) * LANES)
        fine_cos, fine_sin = rope_cos_ref[:, lanes], rope_sin_ref[:, lanes]
        return jnp.concatenate([rope_a_ref[r:r + 1, lanes] * fine_cos + rope_b_ref[r:r + 1, lanes] * fine_sin
                                for r in range(_IN_TM // _ROPE_STEP)], axis=0)

    @pl.when(pl.program_id(1) == 0)
    def _():
        off = 0
        for w in (wqa, wka, wva, wqb, wkb, wvb, wga, wgb):
            n = w.shape[1]
            w_sc[:, off:off + n] = w[...].astype(BF16)
            off += n

    z = jnp.dot(x_ref[...].astype(BF16), w_sc[...], preferred_element_type=F32)
    d = HEAD_DIM
    blk = KV_BLOCK
    zqa, zka, zva, zqb, zkb, zvb = (z[:, j * d:(j + 1) * d] for j in range(6))
    zga, zgb = z[:, 6 * d:8 * d], z[:, 8 * d:10 * d]

    cos_a, sin_a, cos_b, sin_b_lo, sin_b_hi = (rope_table(j) for j in range(_N_ROPE_TABLES))

    def rope_a(t):
        return t * cos_a + pltpu.roll(t, d // 2, 1) * sin_a

    def rope_b(t):
        return (t * cos_b + pltpu.roll(t, d - DIFF_QK_DIM // 2, 1) * sin_b_lo
                + pltpu.roll(t, DIFF_QK_DIM // 2, 1) * sin_b_hi)

    def store_transposed(ref, t, dtype):
        for j in range(_IN_TM // blk):
            ref[0, j, 0:d] = t[j * blk:(j + 1) * blk, :].T.astype(dtype)
            if ref.shape[2] > d:
                ref[0, j, d:] = jnp.ones((ref.shape[2] - d, blk), dtype)

    qa = rope_a(zqa)
    store_transposed(qa_hi_ref, qa, BF16)
    store_transposed(qa_lo_ref, qa - qa.astype(BF16).astype(F32), BF16)
    ka = rope_a(zka) * (d ** -0.5 * LOG2E)
    ka_ref[...] = ka.astype(BF16)
    for j in range(_IN_TM // MOBA_BLOCK):
        part = ka[j * MOBA_BLOCK:(j + 1) * MOBA_BLOCK].reshape(MOBA_BLOCK // SUBLANES, SUBLANES, d)
        ksum_ref[j * SUBLANES:(j + 1) * SUBLANES, :] = jnp.sum(part, axis=0)
    store_transposed(qb_ref, rope_b(zqb) * (DIFF_QK_DIM ** -0.5 * LOG2E), BF16)
    kb_ref[...] = rope_b(zkb).astype(BF16)
    store_transposed(va_ref, zva, BF16)
    store_transposed(vb_ref, zvb, BF16)
    ga_ref[...] = zga.astype(BF16)
    gb_ref[...] = zgb.astype(BF16)


def _in_proj(x2, w_in, tables, seq, weights_to_cast):
    t = x2.shape[0]
    tm = _IN_TM
    n_seq_tiles = seq // tm
    grid = (HEADS, t // tm)
    d = HEAD_DIM

    def wspec(group):
        return pl.BlockSpec((D_MODEL, d), lambda h, i, g=group: (0, g * HEADS + h))

    def gspec(base):
        return pl.BlockSpec((D_MODEL, 2 * d), lambda h, i, b=base: (0, b + h))

    rope_lanes = _N_ROPE_TABLES * LANES
    coarse_spec = pl.BlockSpec((tm // _ROPE_STEP, rope_lanes), lambda h, i: (i % n_seq_tiles, 0))
    fine_spec = pl.BlockSpec((_ROPE_STEP, rope_lanes), lambda h, i: (0, 0))
    ospec = pl.BlockSpec((tm, d), lambda h, i: (i, h))
    tr_spec = pl.BlockSpec((1, tm // KV_BLOCK, d, KV_BLOCK), lambda h, i: (h, i, 0, 0))
    gate_ospec = pl.BlockSpec((tm, 2 * d), lambda h, i: (i, h))
    ksum_rows = tm // MOBA_BLOCK * SUBLANES
    tr_shape = (HEADS, t // KV_BLOCK, d, KV_BLOCK)
    v_spec = pl.BlockSpec((1, tm // KV_BLOCK, V_ROWS, KV_BLOCK), lambda h, i: (h, i, 0, 0))
    v_shape = (HEADS, t // KV_BLOCK, V_ROWS, KV_BLOCK)
    out_shape = (
        jax.ShapeDtypeStruct(tr_shape, BF16),
        jax.ShapeDtypeStruct(tr_shape, BF16),
        jax.ShapeDtypeStruct((t, WIDTH), BF16),
        jax.ShapeDtypeStruct((t // MOBA_BLOCK * SUBLANES, WIDTH), F32),
        jax.ShapeDtypeStruct(v_shape, BF16),
        jax.ShapeDtypeStruct(tr_shape, BF16),
        jax.ShapeDtypeStruct((t, WIDTH), BF16),
        jax.ShapeDtypeStruct(v_shape, BF16),
        jax.ShapeDtypeStruct((t, D_MODEL), BF16),
        jax.ShapeDtypeStruct((t, D_MODEL), BF16),
    )
    out_specs = (tr_spec, tr_spec, ospec, pl.BlockSpec((ksum_rows, d), lambda h, i: (i, h)),
                 v_spec, tr_spec, ospec, v_spec, gate_ospec, gate_ospec)
    gate_a_base = 6 * WIDTH // (2 * d)
    gate_b_base = gate_a_base + D_MODEL // (2 * d)
    in_specs = [pl.BlockSpec((tm, D_MODEL), lambda h, i: (i, 0))]
    in_specs += [wspec(g) for g in range(6)]
    in_specs += [gspec(gate_a_base), gspec(gate_b_base)]
    in_specs += [coarse_spec, coarse_spec, fine_spec, fine_spec]
    assert len(in_specs) == _IN_PROJ_INPUTS and len(out_specs) == _IN_PROJ_OUTPUTS
    cast_in, cast_out, cast_shapes = _cast_specs(weights_to_cast, grid)
    return pl.pallas_call(
        functools.partial(_in_proj_kernel, n_cast=len(weights_to_cast)),
        grid=grid,
        in_specs=in_specs + cast_in,
        out_specs=list(out_specs) + cast_out,
        out_shape=list(out_shape) + cast_shapes,
        scratch_shapes=[pltpu.VMEM((D_MODEL, _IN_COLS), BF16)],
        compiler_params=_params(("arbitrary", "arbitrary")),
        name="in_proj",
    )(x2, *([w_in] * 8), *tables, *weights_to_cast)


class _Operands:
    def __init__(self, n_sub, n_qgroups, src_of_sub, load_k, load_vt, load_rhs, bias_row=None):
        self.n_sub, self.n_qgroups = n_sub, n_qgroups
        self.src_of_sub, self.load_k, self.load_vt = src_of_sub, load_k, load_vt
        self.load_rhs, self.bias_row = load_rhs, bias_row

    def src(self, c):
        return self.src_of_sub[c // self.n_qgroups]


def _pipeline_stage(n, buf, cur, masks, nxt, ops, acc_ref, s_ref, ms):
    vt_blk = {src: ops.load_vt(n, src) for src in sorted({ops.src(c) for c in cur})}
    k_next = {src: ops.load_k(n + 1, src) for src in sorted({ops.src(c) for c in nxt})}

    def issue_scores(c):
        s_ref[1 - buf, c] = jnp.dot(k_next[ops.src(c)], ops.load_rhs(c), preferred_element_type=F32)

    for c in nxt[:SCORE_LEAD]:
        issue_scores(c)
    rest = nxt[SCORE_LEAD:]
    for j, (c, mask) in enumerate(zip(cur, masks)):
        s = s_ref[buf, c]
        if mask is not None:
            s = jnp.where(mask, s, MASKED)
        top = jnp.max(s, axis=0, keepdims=True)
        bias = ops.bias_row(n, c) if (ops.bias_row is not None and mask is None) else None
        if bias is None:
            m_new = jnp.maximum(ms[c], top)
            shift = m_new
        else:
            seen = bias == 0.0
            m_new = jnp.maximum(ms[c], jnp.where(seen, top, MASKED))
            shift = jnp.where(seen, m_new, -MASKED)
        alpha = jnp.exp2(ms[c] - m_new)
        p = jnp.exp2(s - shift).astype(BF16)
        acc_ref[c] = alpha * acc_ref[c] + jnp.dot(vt_blk[ops.src(c)], p, preferred_element_type=F32)
        ms[c] = m_new
        if j < len(rest):
            issue_scores(rest[j])


def _flash_columns(i, ops, blocks_per_iter, acc_ref, s_ref):
    n_qgroups = ops.n_qgroups
    assert blocks_per_iter % 2 == 0
    assert n_qgroups % blocks_per_iter == 0
    n_groups = ops.n_sub * n_qgroups
    blk = KV_BLOCK
    everyone = list(range(n_groups))
    init = tuple(jnp.full((1, blk), MASKED, F32) for _ in range(n_groups))
    for c in everyone:
        acc_ref[c] = jnp.zeros(acc_ref.shape[1:], F32)
    k_first = {src: ops.load_k(0, src) for src in sorted(set(ops.src_of_sub))}
    for c in everyone:
        s_ref[0, c] = jnp.dot(k_first[ops.src(c)], ops.load_rhs(c), preferred_element_type=F32)

    stage = functools.partial(_pipeline_stage, ops=ops, acc_ref=acc_ref, s_ref=s_ref)

    def body(it, carry):
        ms = list(carry)
        for j in range(blocks_per_iter):
            stage(blocks_per_iter * it + j, j % 2, everyone, [None] * n_groups, everyone, ms=ms)
        return tuple(ms)

    n_past = n_qgroups * i
    ms = list(lax.fori_loop(0, (n_qgroups // blocks_per_iter) * i, body, init))
    row = lax.broadcasted_iota(jnp.int32, (blk, blk), 0)
    col = lax.broadcasted_iota(jnp.int32, (blk, blk), 1)
    causal = row <= col
    for t in range(n_qgroups):
        cur = [c for c in everyone if c % n_qgroups >= t]
        nxt = [c for c in everyone if c % n_qgroups >= t + 1]
        masks = [causal if c % n_qgroups == t else None for c in cur]
        stage(n_past + t, t % 2, cur, masks, nxt, ms=ms)


def _normalised(acc):
    return acc[0:HEAD_DIM] / acc[HEAD_DIM:HEAD_DIM + 1]


def _cast_specs(weights, grid):
    n_steps = math.prod(grid)

    def step_of(*idx):
        lin = idx[0]
        for extent, j in zip(grid[1:], idx[1:]):
            lin = lin * extent + j
        return lin

    in_specs, out_specs, out_shapes = [], [], []
    for w in weights:
        rows, cols = w.shape
        share = 1
        while rows * share % (n_steps * BF16_ROW_TILE):
            share *= 2
            assert share <= n_steps, (rows, n_steps)
        spec = pl.BlockSpec((rows * share // n_steps, cols),
                            lambda *idx, s=share: (step_of(*idx) // s, 0))
        in_specs.append(spec)
        out_specs.append(spec)
        out_shapes.append(jax.ShapeDtypeStruct(w.shape, BF16))
    return in_specs, out_specs, out_shapes


def _cast_blocks(src_refs, dst_refs):
    for src, dst in zip(src_refs, dst_refs):
        dst[...] = src[...].astype(BF16)


_MOBA_QGROUPS = 4
_MOBA_HEADS_PER_STEP = 4


def _split_bf16(a):
    hi = a.astype(BF16)
    lo = (a - hi.astype(F32)).astype(BF16)
    return hi, lo


def _moba_kernel(*refs, n_blocks, n_cast):
    q_hi_ref, q_lo_ref, k_ref, vt_ref, ksum_ref = refs[:5]
    o_ref = refs[5 + n_cast]
    bias_sc, acc_sc, s_sc = refs[6 + 2 * n_cast:]
    _cast_blocks(refs[5:5 + n_cast], refs[6 + n_cast:6 + 2 * n_cast])
    i = pl.program_id(2)
    blk = MOBA_BLOCK
    d = HEAD_DIM
    nq = _MOBA_QGROUPS
    row = lax.broadcasted_iota(jnp.int32, (n_blocks, blk), 0).astype(F32)

    for hd in range(_MOBA_HEADS_PER_STEP):
        ksum = ksum_ref[:, hd * d:(hd + 1) * d]
        kmean = jnp.sum(ksum.reshape(n_blocks, SUBLANES, d), axis=1) * (1.0 / blk)
        m_hi, m_lo = _split_bf16(kmean)
        for g in range(nq):
            q_hi, q_lo = q_hi_ref[hd, g], q_lo_ref[hd, g]
            q_blk = (nq * i + g).astype(F32)
            gate = (jnp.dot(m_hi, q_hi, preferred_element_type=F32)
                    + jnp.dot(m_lo, q_hi, preferred_element_type=F32)
                    + jnp.dot(m_hi, q_lo, preferred_element_type=F32))
            avail = row < q_blk
            keep = row == q_blk
            val = jnp.where(avail, gate, -jnp.inf)
            for _ in range(MOBA_TOPK):
                best = jnp.max(val, axis=0, keepdims=True)
                cand = (val == best) & avail
                first = jnp.min(jnp.where(cand, row, float(n_blocks)), axis=0, keepdims=True)
                pick = row == first
                keep = keep | pick
                avail = avail & jnp.logical_not(pick)
                val = jnp.where(pick, -jnp.inf, val)
            bias = jnp.where(keep, 0.0, MASKED)
            for n in range(n_blocks):
                bias_sc[hd * nq + g, n] = bias[n:n + 1, :]

    ops = _Operands(
        n_sub=_MOBA_HEADS_PER_STEP, n_qgroups=nq, src_of_sub=list(range(_MOBA_HEADS_PER_STEP)),
        load_k=lambda n, hd: k_ref[pl.ds(pl.multiple_of(n * blk, blk), blk), hd * d:(hd + 1) * d],
        load_vt=lambda n, hd: vt_ref[hd, n],
        load_rhs=lambda c: q_hi_ref[c // nq, c % nq],
        bias_row=lambda n, c: bias_sc[c, n])
    _flash_columns(i, ops, BLOCKS_PER_ITER, acc_sc, s_sc)
    for hd in range(_MOBA_HEADS_PER_STEP):
        for g in range(nq):
            o_ref[g * blk:(g + 1) * blk, hd * d:(hd + 1) * d] = _normalised(acc_sc[hd * nq + g]).astype(BF16).T


def _moba(qa_hi, qa_lo, ka, va_t, ksum, batch, seq, weights_to_cast):
    n_blocks = seq // MOBA_BLOCK
    d = HEAD_DIM
    hps = _MOBA_HEADS_PER_STEP
    n_groups = hps * _MOBA_QGROUPS
    tq = _MOBA_QGROUPS * MOBA_BLOCK
    n_tiles = seq // tq
    grid = (batch, HEADS // hps, n_tiles)
    cast_in, cast_out, cast_shapes = _cast_specs(weights_to_cast, grid)
    q_spec = pl.BlockSpec((hps, _MOBA_QGROUPS, d, MOBA_BLOCK), lambda b, h, i: (h, b * n_tiles + i, 0, 0))
    return pl.pallas_call(
        functools.partial(_moba_kernel, n_blocks=n_blocks, n_cast=len(weights_to_cast)),
        grid=grid,
        in_specs=[q_spec, q_spec,
                  pl.BlockSpec((seq, hps * d), lambda b, h, i: (b, h)),
                  pl.BlockSpec((hps, n_blocks, V_ROWS, MOBA_BLOCK), lambda b, h, i: (h, b, 0, 0)),
                  pl.BlockSpec((n_blocks * SUBLANES, hps * d), lambda b, h, i: (b, h))] + cast_in,
        out_specs=[pl.BlockSpec((tq, hps * d), lambda b, h, i: (b * n_tiles + i, h))] + cast_out,
        out_shape=[jax.ShapeDtypeStruct(ka.shape, BF16)] + cast_shapes,
        scratch_shapes=[pltpu.VMEM((n_groups, n_blocks, 1, MOBA_BLOCK), F32),
                        pltpu.VMEM((n_groups, V_ROWS, MOBA_BLOCK), F32),
                        pltpu.VMEM((2, n_groups, KV_BLOCK, MOBA_BLOCK), F32)],
        compiler_params=_params(("arbitrary", "arbitrary", "arbitrary")),
        name="moba",
    )(qa_hi, qa_lo, ka, va_t, ksum, *weights_to_cast)


_DIFF_QGROUPS = 4
_DIFF_HEADS_PER_STEP = 4


def _diff_kernel(*refs, lam_init, n_cast):
    q_ref, k_ref, vt_ref, lam_ref, subln_ref = refs[:5]
    o_ref = refs[5 + n_cast]
    rhs_sc, acc_sc, s_sc = refs[6 + 2 * n_cast:]
    _cast_blocks(refs[5:5 + n_cast], refs[6 + n_cast:6 + 2 * n_cast])
    i = pl.program_id(2)
    blk = KV_BLOCK
    d = HEAD_DIM
    nq = _DIFF_QGROUPS
    hps = _DIFF_HEADS_PER_STEP

    feat = lax.broadcasted_iota(jnp.int32, (d, blk), 0)
    for hd in range(hps):
        for sub in range(2):
            own = (feat < DIFF_QK_DIM) if sub == 0 else (feat >= DIFF_QK_DIM)
            for g in range(nq):
                q_t = q_ref[hd, g]
                rhs_sc[(2 * hd + sub) * nq + g] = jnp.where(own, q_t, jnp.zeros_like(q_t))

    ops = _Operands(
        n_sub=2 * hps, n_qgroups=nq, src_of_sub=[hd for hd in range(hps) for _ in range(2)],
        load_k=lambda n, hd: k_ref[pl.ds(pl.multiple_of(n * blk, blk), blk), hd * d:(hd + 1) * d],
        load_vt=lambda n, hd: vt_ref[hd, n],
        load_rhs=lambda c: rhs_sc[c])
    _flash_columns(i, ops, BLOCKS_PER_ITER, acc_sc, s_sc)

    lq = lam_ref[...]
    lam = (jnp.exp(jnp.sum(lq[0:1, :] * lq[1:2, :], axis=1, keepdims=True))
           - jnp.exp(jnp.sum(lq[2:3, :] * lq[3:4, :], axis=1, keepdims=True)) + lam_init)
    for hd in range(hps):
        for g in range(nq):
            first, second = acc_sc[2 * hd * nq + g], acc_sc[(2 * hd + 1) * nq + g]
            o = _normalised(first) - lam * _normalised(second)
            o = o * lax.rsqrt(jnp.mean(o * o, axis=0, keepdims=True) + RMS_EPS)
            o_ref[g * blk:(g + 1) * blk, hd * d:(hd + 1) * d] = (
                o.T * subln_ref[...] * (1.0 - lam_init)).astype(BF16)


def _diff(qb_t, kb, vb_t, lambda_qk, subln_w, batch, seq, lam_init, weights_to_cast):
    d = HEAD_DIM
    n_blocks = seq // KV_BLOCK
    hps = _DIFF_HEADS_PER_STEP
    n_groups = 2 * hps * _DIFF_QGROUPS
    tq = _DIFF_QGROUPS * KV_BLOCK
    n_tiles = seq // tq
    grid = (batch, HEADS // hps, n_tiles)
    cast_in, cast_out, cast_shapes = _cast_specs(weights_to_cast, grid)
    return pl.pallas_call(
        functools.partial(_diff_kernel, lam_init=lam_init, n_cast=len(weights_to_cast)),
        grid=grid,
        in_specs=[pl.BlockSpec((hps, _DIFF_QGROUPS, d, KV_BLOCK), lambda b, h, i: (h, b * n_tiles + i, 0, 0)),
                  pl.BlockSpec((seq, hps * d), lambda b, h, i: (b, h)),
                  pl.BlockSpec((hps, n_blocks, V_ROWS, KV_BLOCK), lambda b, h, i: (h, b, 0, 0)),
                  pl.BlockSpec(lambda_qk.shape, lambda b, h, i: (0, 0)),
                  pl.BlockSpec((1, d), lambda b, h, i: (0, 0))] + cast_in,
        out_specs=[pl.BlockSpec((tq, hps * d), lambda b, h, i: (b * n_tiles + i, h))] + cast_out,
        out_shape=[jax.ShapeDtypeStruct(kb.shape, BF16)] + cast_shapes,
        scratch_shapes=[pltpu.VMEM((n_groups, d, KV_BLOCK), BF16),
                        pltpu.VMEM((n_groups, V_ROWS, KV_BLOCK), F32),
                        pltpu.VMEM((2, n_groups, KV_BLOCK, KV_BLOCK), F32)],
        compiler_params=_params(("arbitrary", "arbitrary", "arbitrary")),
        name="diffattn",
    )(qb_t, kb, vb_t, lambda_qk, subln_w.reshape(1, d), *weights_to_cast)


_ROW_TM = 512
_FFN_TH = 512


_MERGE_INPUTS = 6


def _merge_kernel(*refs, n_cast):
    ya_ref, yb_ref, ga_ref, gb_ref, wa_ref, wb_ref = refs[:_MERGE_INPUTS]
    o_ref = refs[_MERGE_INPUTS + n_cast]
    _cast_blocks(refs[_MERGE_INPUTS:_MERGE_INPUTS + n_cast], refs[_MERGE_INPUTS + n_cast + 1:])
    a = jnp.dot(ya_ref[...], wa_ref[...], preferred_element_type=F32)
    b = jnp.dot(yb_ref[...], wb_ref[...], preferred_element_type=F32)
    o_ref[...] = (_sigmoid(ga_ref[...].astype(F32)) * a + _sigmoid(gb_ref[...].astype(F32)) * b).astype(BF16)


def _merge(ya, yb, ga, gb, wa, wb, weights_to_cast):
    t = ya.shape[0]
    tm = _ROW_TM
    grid = (t // tm,)
    row = lambda w: pl.BlockSpec((tm, w), lambda i: (i, 0))
    full = lambda a: pl.BlockSpec(a.shape, lambda i: (0, 0))
    cast_in, cast_out, cast_shapes = _cast_specs(weights_to_cast, grid)
    return pl.pallas_call(
        functools.partial(_merge_kernel, n_cast=len(weights_to_cast)),
        grid=grid,
        in_specs=[row(WIDTH), row(WIDTH), row(D_MODEL), row(D_MODEL), full(wa), full(wb)] + cast_in,
        out_specs=[row(D_MODEL)] + cast_out,
        out_shape=[jax.ShapeDtypeStruct((t, D_MODEL), BF16)] + cast_shapes,
        compiler_params=_params(("arbitrary",)),
        name="merge",
    )(ya, yb, ga, gb, wa, wb, *weights_to_cast)


def _layer_norm(y, g, b):
    mu = jnp.mean(y, axis=1, keepdims=True)
    c = y - mu
    var = jnp.mean(c * c, axis=1, keepdims=True)
    return c * lax.rsqrt(var + LN_EPS) * g + b


def _row_halves(ref):
    half = ref.shape[0] // 2
    return slice(0, half), slice(half, 2 * half)


def _out_ln_kernel(m_ref, w_ref, x_ref, g_ref, b_ref, h_ref, hb_ref, *, alpha):
    halves = _row_halves(m_ref)
    ys = [alpha * x_ref[rows, :] + jnp.dot(m_ref[rows, :], w_ref[...], preferred_element_type=F32)
          for rows in halves]
    for rows, y in zip(halves, ys):
        h = _layer_norm(y, g_ref[...], b_ref[...])
        h_ref[rows, :] = h
        hb_ref[rows, :] = h.astype(BF16)


def _out_ln(m, w_out, x2, g, b, alpha):
    t = m.shape[0]
    tm = _ROW_TM
    row = pl.BlockSpec((tm, D_MODEL), lambda i: (i, 0))
    vec = pl.BlockSpec((1, D_MODEL), lambda i: (0, 0))
    return pl.pallas_call(
        functools.partial(_out_ln_kernel, alpha=alpha),
        grid=(t // tm,),
        in_specs=[row, pl.BlockSpec(w_out.shape, lambda i: (0, 0)), row, vec, vec],
        out_specs=(row, row),
        out_shape=(jax.ShapeDtypeStruct((t, D_MODEL), F32), jax.ShapeDtypeStruct((t, D_MODEL), BF16)),
        compiler_params=_params(("parallel",)),
        name="out_ln",
    )(m, w_out, x2, g.reshape(1, D_MODEL), b.reshape(1, D_MODEL))


def _ffn_kernel(hb_ref, wg_ref, wu_ref, wo_ref, h_ref, g_ref, b_ref, o_ref, *, alpha):
    j = pl.program_id(1)

    @pl.when(j == 0)
    def _():
        o_ref[...] = alpha * h_ref[...]

    hb = hb_ref[...]
    gate = jnp.dot(hb, wg_ref[...], preferred_element_type=F32)
    up = jnp.dot(hb, wu_ref[...], preferred_element_type=F32)
    act = (gate * _sigmoid(gate) * up).astype(BF16)
    o_ref[...] += jnp.dot(act, wo_ref[...], preferred_element_type=F32)

    @pl.when(j == pl.num_programs(1) - 1)
    def _():
        o_ref[...] = _layer_norm(o_ref[...], g_ref[...], b_ref[...])


def _ffn_ln(hb, h, w_ffn_in, w_ffn_out, g, b, alpha):
    t = h.shape[0]
    tm, th = _ROW_TM, _FFN_TH
    nj = FFN_HIDDEN // th
    row = pl.BlockSpec((tm, D_MODEL), lambda i, j: (i, 0))
    vec = pl.BlockSpec((1, D_MODEL), lambda i, j: (0, 0))
    return pl.pallas_call(
        functools.partial(_ffn_kernel, alpha=alpha),
        grid=(t // tm, nj),
        in_specs=[row,
                  pl.BlockSpec((D_MODEL, th), lambda i, j: (0, j)),
                  pl.BlockSpec((D_MODEL, th), lambda i, j: (0, nj + j)),
                  pl.BlockSpec((th, D_MODEL), lambda i, j: (j, 0)),
                  row, vec, vec],
        out_specs=row,
        out_shape=jax.ShapeDtypeStruct((t, D_MODEL), F32),
        compiler_params=_params(("parallel", "arbitrary")),
        name="ffn_ln",
    )(hb, w_ffn_in, w_ffn_in, w_ffn_out, h, g.reshape(1, D_MODEL), b.reshape(1, D_MODEL))


def kernel(x, w_in, lambda_qk, diff_subln_w, w_branch_a, w_branch_b, w_out,
           ln1_g, ln1_b, w_ffn_in, w_ffn_out, ln2_g, ln2_b):
    batch, seq, _ = x.shape
    depth = w_in.shape[0]
    alpha = (2.0 * depth) ** 0.25
    tables = _rope_tables(seq)
    h = x.reshape(batch * seq, D_MODEL)
    for l in range(depth):
        lam_init = 0.8 - 0.6 * math.exp(-0.3 * l)
        qa_hi, qa_lo, ka, ksum, va_t, qb_t, kb, vb_t, ga, gb = _in_proj(h, w_in[l], tables, seq, ())
        ya, wa, wb, wo, w_down = _moba(qa_hi, qa_lo, ka, va_t, ksum, batch, seq,
                                       (w_branch_a[l], w_branch_b[l], w_out[l], w_ffn_out[l]))
        yb, = _diff(qb_t, kb, vb_t, lambda_qk[l], diff_subln_w[l], batch, seq, lam_init, ())
        m, w_up = _merge(ya, yb, ga, gb, wa, wb, (w_ffn_in[l],))
        h, hb = _out_ln(m, wo, h, ln1_g[l], ln1_b[l], alpha)
        h = _ffn_ln(hb, h, w_up, w_down, ln2_g[l], ln2_b[l], alpha)
    return h.reshape(batch, seq, D_MODEL)
```

```python
import functools
import math

import jax
import jax.numpy as jnp
import numpy as np
from jax import lax
from jax.experimental import pallas as pl
from jax.experimental.pallas import tpu as pltpu

D_MODEL = 2048
HEADS = 8
HEAD_DIM = 128
WIDTH = HEADS * HEAD_DIM
MOBA_BLOCK = 256
MOBA_TOPK = 3
DIFF_QK_DIM = 64
ROPE_THETA = 10000.0
FFN_HIDDEN = 5632
LN_EPS = 1e-5
RMS_EPS = 1e-5

LANES = 128
SUBLANES = 8
VMEM_LIMIT = 56 * 1024 * 1024
MASKED = -1e30
LOG2E = math.log2(math.e)
KV_BLOCK = 256
BF16_ROW_TILE = 16
V_ROWS = HEAD_DIM + BF16_ROW_TILE
BLOCKS_PER_ITER = 4
SCORE_LEAD = 2

BF16 = jnp.bfloat16
F32 = jnp.float32


def _params(sem):
    return pltpu.CompilerParams(dimension_semantics=sem, vmem_limit_bytes=VMEM_LIMIT)


_N_ROPE_TABLES = 5
_ROPE_STEP = 64


def _rope_tables(seq):
    step = _ROPE_STEP
    assert seq % step == 0
    half_a, half_b = HEAD_DIM // 2, DIFF_QK_DIM // 2
    lane = np.arange(_N_ROPE_TABLES * LANES)
    table, l = lane // LANES, lane % LANES
    freq = np.where(table < 2, l % half_a, 2 * (l % half_b))
    upper = np.where(table < 2, l % HEAD_DIM >= half_a, l % DIFF_QK_DIM >= half_b)
    use_sin = (table == 1) | (table >= 3)
    coef = np.select([table == 1, table == 3, table == 4],
                     [np.where(upper, 1.0, -1.0),
                      np.where(upper, 0.0, -1.0),
                      np.where(upper, 1.0, 0.0)],
                     default=1.0).astype(np.float32)
    inv_lane = (ROPE_THETA ** (-jnp.asarray(freq, F32) * 2.0 / HEAD_DIM))[None, :]
    coarse = jnp.arange(seq // step, dtype=F32)[:, None] * float(step) * inv_lane
    fine = jnp.arange(step, dtype=F32)[:, None] * inv_lane
    cc, sc = jnp.cos(coarse), jnp.sin(coarse)
    a_coarse = jnp.where(use_sin, sc, cc) * coef
    b_coarse = jnp.where(use_sin, cc, -sc) * coef
    return a_coarse, b_coarse, jnp.cos(fine), jnp.sin(fine)


_IN_TM = 512
_IN_COLS = 6 * HEAD_DIM + 2 * 2 * HEAD_DIM


def _sigmoid(t):
    return 0.5 * jnp.tanh(0.5 * t) + 0.5


_IN_PROJ_INPUTS = 13
_IN_PROJ_OUTPUTS = 10


def _in_proj_kernel(*refs, n_cast):
    x_ref, wqa, wka, wva, wqb, wkb, wvb, wga, wgb = refs[:9]
    rope_a_ref, rope_b_ref, rope_cos_ref, rope_sin_ref = refs[9:_IN_PROJ_INPUTS]
    outs = refs[_IN_PROJ_INPUTS + n_cast:_IN_PROJ_INPUTS + n_cast + _IN_PROJ_OUTPUTS]
    qa_hi_ref, qa_lo_ref, ka_ref, ksum_ref, va_ref, qb_ref, kb_ref, vb_ref, ga_ref, gb_ref = outs
    w_sc = refs[-1]
    _cast_blocks(refs[_IN_PROJ_INPUTS:_IN_PROJ_INPUTS + n_cast], refs[_IN_PROJ_INPUTS + n_cast + _IN_PROJ_OUTPUTS:-1])

    def rope_table(j):
        lanes = slice(j * LANES, (j + 1) * LANES)
        fine_cos, fine_sin = rope_cos_ref[:, lanes], rope_sin_ref[:, lanes]
        return jnp.concatenate([rope_a_ref[r:r + 1, lanes] * fine_cos + rope_b_ref[r:r + 1, lanes] * fine_sin
                                for r in range(_IN_TM // _ROPE_STEP)], axis=0)

    @pl.when(pl.program_id(1) == 0)
    def _():
        off = 0
        for w in (wqa, wka, wva, wqb, wkb, wvb, wga, wgb):
            n = w.shape[1]
            w_sc[:, off:off + n] = w[...].astype(BF16)
            off += n

    z = jnp.dot(x_ref[...].astype(BF16), w_sc[...], preferred_element_type=F32)
    d = HEAD_DIM
    blk = KV_BLOCK
    zqa, zka, zva, zqb, zkb, zvb = (z[:, j * d:(j + 1) * d] for j in range(6))
    zga, zgb = z[:, 6 * d:8 * d], z[:, 8 * d:10 * d]

    cos_a, sin_a, cos_b, sin_b_lo, sin_b_hi = (rope_table(j) for j in range(_N_ROPE_TABLES))

    def rope_a(t):
        return t * cos_a + pltpu.roll(t, d // 2, 1) * sin_a

    def rope_b(t):
        return (t * cos_b + pltpu.roll(t, d - DIFF_QK_DIM // 2, 1) * sin_b_lo
                + pltpu.roll(t, DIFF_QK_DIM // 2, 1) * sin_b_hi)

    def store_transposed(ref, t, dtype):
        for j in range(_IN_TM // blk):
            ref[0, j, 0:d] = t[j * blk:(j + 1) * blk, :].T.astype(dtype)
            if ref.shape[2] > d:
                ref[0, j, d:] = jnp.ones((ref.shape[2] - d, blk), dtype)

    qa = rope_a(zqa)
    store_transposed(qa_hi_ref, qa, BF16)
    store_transposed(qa_lo_ref, qa - qa.astype(BF16).astype(F32), BF16)
    ka = rope_a(zka) * (d ** -0.5 * LOG2E)
    ka_ref[...] = ka.astype(BF16)
    for j in range(_IN_TM // MOBA_BLOCK):
        part = ka[j * MOBA_BLOCK:(j + 1) * MOBA_BLOCK].reshape(MOBA_BLOCK // SUBLANES, SUBLANES, d)
        ksum_ref[j * SUBLANES:(j + 1) * SUBLANES, :] = jnp.sum(part, axis=0)
    store_transposed(qb_ref, rope_b(zqb) * (DIFF_QK_DIM ** -0.5 * LOG2E), BF16)
    kb_ref[...] = rope_b(zkb).astype(BF16)
    store_transposed(va_ref, zva, BF16)
    store_transposed(vb_ref, zvb, BF16)
    ga_ref[...] = zga.astype(BF16)
    gb_ref[...] = zgb.astype(BF16)


def _in_proj(x2, w_in, tables, seq, weights_to_cast):
    t = x2.shape[0]
    tm = _IN_TM
    n_seq_tiles = seq // tm
    grid = (HEADS, t // tm)
    d = HEAD_DIM

    def wspec(group):
        return pl.BlockSpec((D_MODEL, d), lambda h, i, g=group: (0, g * HEADS + h))

    def gspec(base):
        return pl.BlockSpec((D_MODEL, 2 * d), lambda h, i, b=base: (0, b + h))

    rope_lanes = _N_ROPE_TABLES * LANES
    coarse_spec = pl.BlockSpec((tm // _ROPE_STEP, rope_lanes), lambda h, i: (i % n_seq_tiles, 0))
    fine_spec = pl.BlockSpec((_ROPE_STEP, rope_lanes), lambda h, i: (0, 0))
    ospec = pl.BlockSpec((tm, d), lambda h, i: (i, h))
    tr_spec = pl.BlockSpec((1, tm // KV_BLOCK, d, KV_BLOCK), lambda h, i: (h, i, 0, 0))
    gate_ospec = pl.BlockSpec((tm, 2 * d), lambda h, i: (i, h))
    ksum_rows = tm // MOBA_BLOCK * SUBLANES
    tr_shape = (HEADS, t // KV_BLOCK, d, KV_BLOCK)
    v_spec = pl.BlockSpec((1, tm // KV_BLOCK, V_ROWS, KV_BLOCK), lambda h, i: (h, i, 0, 0))
    v_shape = (HEADS, t // KV_BLOCK, V_ROWS, KV_BLOCK)
    out_shape = (
        jax.ShapeDtypeStruct(tr_shape, BF16),
        jax.ShapeDtypeStruct(tr_shape, BF16),
        jax.ShapeDtypeStruct((t, WIDTH), BF16),
        jax.ShapeDtypeStruct((t // MOBA_BLOCK * SUBLANES, WIDTH), F32),
        jax.ShapeDtypeStruct(v_shape, BF16),
        jax.ShapeDtypeStruct(tr_shape, BF16),
        jax.ShapeDtypeStruct((t, WIDTH), BF16),
        jax.ShapeDtypeStruct(v_shape, BF16),
        jax.ShapeDtypeStruct((t, D_MODEL), BF16),
        jax.ShapeDtypeStruct((t, D_MODEL), BF16),
    )
    out_specs = (tr_spec, tr_spec, ospec, pl.BlockSpec((ksum_rows, d), lambda h, i: (i, h)),
                 v_spec, tr_spec, ospec, v_spec, gate_ospec, gate_ospec)
    gate_a_base = 6 * WIDTH // (2 * d)
    gate_b_base = gate_a_base + D_MODEL // (2 * d)
    in_specs = [pl.BlockSpec((tm, D_MODEL), lambda h, i: (i, 0))]
    in_specs += [wspec(g) for g in range(6)]
    in_specs += [gspec(gate_a_base), gspec(gate_b_base)]
    in_specs += [coarse_spec, coarse_spec, fine_spec, fine_spec]
    assert len(in_specs) == _IN_PROJ_INPUTS and len(out_specs) == _IN_PROJ_OUTPUTS
    cast_in, cast_out, cast_shapes = _cast_specs(weights_to_cast, grid)
    return pl.pallas_call(
        functools.partial(_in_proj_kernel, n_cast=len(weights_to_cast)),
        grid=grid,
        in_specs=in_specs + cast_in,
        out_specs=list(out_specs) + cast_out,
        out_shape=list(out_shape) + cast_shapes,
        scratch_shapes=[pltpu.VMEM((D_MODEL, _IN_COLS), BF16)],
        compiler_params=_params(("arbitrary", "arbitrary")),
        name="in_proj",
    )(x2, *([w_in] * 8), *tables, *weights_to_cast)


class _Operands:
    def __init__(self, n_sub, n_qgroups, src_of_sub, load_k, load_vt, load_rhs, bias_row=None):
        self.n_sub, self.n_qgroups = n_sub, n_qgroups
        self.src_of_sub, self.load_k, self.load_vt = src_of_sub, load_k, load_vt
        self.load_rhs, self.bias_row = load_rhs, bias_row

    def src(self, c):
        return self.src_of_sub[c // self.n_qgroups]


def _pipeline_stage(n, buf, cur, masks, nxt, ops, acc_ref, s_ref, ms):
    vt_blk = {src: ops.load_vt(n, src) for src in sorted({ops.src(c) for c in cur})}
    k_next = {src: ops.load_k(n + 1, src) for src in sorted({ops.src(c) for c in nxt})}

    def issue_scores(c):
        s_ref[1 - buf, c] = jnp.dot(k_next[ops.src(c)], ops.load_rhs(c), preferred_element_type=F32)

    for c in nxt[:SCORE_LEAD]:
        issue_scores(c)
    rest = nxt[SCORE_LEAD:]
    for j, (c, mask) in enumerate(zip(cur, masks)):
        s = s_ref[buf, c]
        if mask is not None:
            s = jnp.where(mask, s, MASKED)
        top = jnp.max(s, axis=0, keepdims=True)
        bias = ops.bias_row(n, c) if (ops.bias_row is not None and mask is None) else None
        if bias is None:
            m_new = jnp.maximum(ms[c], top)
            shift = m_new
        else:
            seen = bias == 0.0
            m_new = jnp.maximum(ms[c], jnp.where(seen, top, MASKED))
            shift = jnp.where(seen, m_new, -MASKED)
        alpha = jnp.exp2(ms[c] - m_new)
        p = jnp.exp2(s - shift).astype(BF16)
        acc_ref[c] = alpha * acc_ref[c] + jnp.dot(vt_blk[ops.src(c)], p, preferred_element_type=F32)
        ms[c] = m_new
        if j < len(rest):
            issue_scores(rest[j])


def _flash_columns(i, ops, blocks_per_iter, acc_ref, s_ref):
    n_qgroups = ops.n_qgroups
    assert blocks_per_iter % 2 == 0
    assert n_qgroups % blocks_per_iter == 0
    n_groups = ops.n_sub * n_qgroups
    blk = KV_BLOCK
    everyone = list(range(n_groups))
    init = tuple(jnp.full((1, blk), MASKED, F32) for _ in range(n_groups))
    for c in everyone:
        acc_ref[c] = jnp.zeros(acc_ref.shape[1:], F32)
    k_first = {src: ops.load_k(0, src) for src in sorted(set(ops.src_of_sub))}
    for c in everyone:
        s_ref[0, c] = jnp.dot(k_first[ops.src(c)], ops.load_rhs(c), preferred_element_type=F32)

    stage = functools.partial(_pipeline_stage, ops=ops, acc_ref=acc_ref, s_ref=s_ref)

    def body(it, carry):
        ms = list(carry)
        for j in range(blocks_per_iter):
            stage(blocks_per_iter * it + j, j % 2, everyone, [None] * n_groups, everyone, ms=ms)
        return tuple(ms)

    n_past = n_qgroups * i
    ms = list(lax.fori_loop(0, (n_qgroups // blocks_per_iter) * i, body, init))
    row = lax.broadcasted_iota(jnp.int32, (blk, blk), 0)
    col = lax.broadcasted_iota(jnp.int32, (blk, blk), 1)
    causal = row <= col
    for t in range(n_qgroups):
        cur = [c for c in everyone if c % n_qgroups >= t]
        nxt = [c for c in everyone if c % n_qgroups >= t + 1]
        masks = [causal if c % n_qgroups == t else None for c in cur]
        stage(n_past + t, t % 2, cur, masks, nxt, ms=ms)


def _normalised(acc):
    return acc[0:HEAD_DIM] / acc[HEAD_DIM:HEAD_DIM + 1]


def _cast_specs(weights, grid):
    n_steps = math.prod(grid)

    def step_of(*idx):
        lin = idx[0]
        for extent, j in zip(grid[1:], idx[1:]):
            lin = lin * extent + j
        return lin

    in_specs, out_specs, out_shapes = [], [], []
    for w in weights:
        rows, cols = w.shape
        share = 1
        while rows * share % (n_steps * BF16_ROW_TILE):
            share *= 2
            assert share <= n_steps, (rows, n_steps)
        spec = pl.BlockSpec((rows * share // n_steps, cols),
                            lambda *idx, s=share: (step_of(*idx) // s, 0))
        in_specs.append(spec)
        out_specs.append(spec)
        out_shapes.append(jax.ShapeDtypeStruct(w.shape, BF16))
    return in_specs, out_specs, out_shapes


def _cast_blocks(src_refs, dst_refs):
    for src, dst in zip(src_refs, dst_refs):
        dst[...] = src[...].astype(BF16)


_MOBA_QGROUPS = 4
_MOBA_HEADS_PER_STEP = 4


def _split_bf16(a):
    hi = a.astype(BF16)
    lo = (a - hi.astype(F32)).astype(BF16)
    return hi, lo


def _moba_kernel(*refs, n_blocks, n_cast):
    q_hi_ref, q_lo_ref, k_ref, vt_ref, ksum_ref = refs[:5]
    o_ref = refs[5 + n_cast]
    bias_sc, acc_sc, s_sc = refs[6 + 2 * n_cast:]
    _cast_blocks(refs[5:5 + n_cast], refs[6 + n_cast:6 + 2 * n_cast])
    i = pl.program_id(2)
    blk = MOBA_BLOCK
    d = HEAD_DIM
    nq = _MOBA_QGROUPS
    row = lax.broadcasted_iota(jnp.int32, (n_blocks, blk), 0).astype(F32)

    for hd in range(_MOBA_HEADS_PER_STEP):
        ksum = ksum_ref[:, hd * d:(hd + 1) * d]
        kmean = jnp.sum(ksum.reshape(n_blocks, SUBLANES, d), axis=1) * (1.0 / blk)
        m_hi, m_lo = _split_bf16(kmean)
        for g in range(nq):
            q_hi, q_lo = q_hi_ref[hd, g], q_lo_ref[hd, g]
            q_blk = (nq * i + g).astype(F32)
            gate = (jnp.dot(m_hi, q_hi, preferred_element_type=F32)
                    + jnp.dot(m_lo, q_hi, preferred_element_type=F32)
                    + jnp.dot(m_hi, q_lo, preferred_element_type=F32))
            avail = row < q_blk
            keep = row == q_blk
            val = jnp.where(avail, gate, -jnp.inf)
            for _ in range(MOBA_TOPK):
                best = jnp.max(val, axis=0, keepdims=True)
                cand = (val == best) & avail
                first = jnp.min(jnp.where(cand, row, float(n_blocks)), axis=0, keepdims=True)
                pick = row == first
                keep = keep | pick
                avail = avail & jnp.logical_not(pick)
                val = jnp.where(pick, -jnp.inf, val)
            bias = jnp.where(keep, 0.0, MASKED)
            for n in range(n_blocks):
                bias_sc[hd * nq + g, n] = bias[n:n + 1, :]

    ops = _Operands(
        n_sub=_MOBA_HEADS_PER_STEP, n_qgroups=nq, src_of_sub=list(range(_MOBA_HEADS_PER_STEP)),
        load_k=lambda n, hd: k_ref[pl.ds(pl.multiple_of(n * blk, blk), blk), hd * d:(hd + 1) * d],
        load_vt=lambda n, hd: vt_ref[hd, n],
        load_rhs=lambda c: q_hi_ref[c // nq, c % nq],
        bias_row=lambda n, c: bias_sc[c, n])
    _flash_columns(i, ops, BLOCKS_PER_ITER, acc_sc, s_sc)
    for hd in range(_MOBA_HEADS_PER_STEP):
        for g in range(nq):
            o_ref[g * blk:(g + 1) * blk, hd * d:(hd + 1) * d] = _normalised(acc_sc[hd * nq + g]).astype(BF16).T


def _moba(qa_hi, qa_lo, ka, va_t, ksum, batch, seq, weights_to_cast):
    n_blocks = seq // MOBA_BLOCK
    d = HEAD_DIM
    hps = _MOBA_HEADS_PER_STEP
    n_groups = hps * _MOBA_QGROUPS
    tq = _MOBA_QGROUPS * MOBA_BLOCK
    n_tiles = seq // tq
    grid = (batch, HEADS // hps, n_tiles)
    cast_in, cast_out, cast_shapes = _cast_specs(weights_to_cast, grid)
    q_spec = pl.BlockSpec((hps, _MOBA_QGROUPS, d, MOBA_BLOCK), lambda b, h, i: (h, b * n_tiles + i, 0, 0))
    return pl.pallas_call(
        functools.partial(_moba_kernel, n_blocks=n_blocks, n_cast=len(weights_to_cast)),
        grid=grid,
        in_specs=[q_spec, q_spec,
                  pl.BlockSpec((seq, hps * d), lambda b, h, i: (b, h)),
                  pl.BlockSpec((hps, n_blocks, V_ROWS, MOBA_BLOCK), lambda b, h, i: (h, b, 0, 0)),
                  pl.BlockSpec((n_blocks * SUBLANES, hps * d), lambda b, h, i: (b, h))] + cast_in,
        out_specs=[pl.BlockSpec((tq, hps * d), lambda b, h, i: (b * n_tiles + i, h))] + cast_out,
        out_shape=[jax.ShapeDtypeStruct(ka.shape, BF16)] + cast_shapes,
        scratch_shapes=[pltpu.VMEM((n_groups, n_blocks, 1, MOBA_BLOCK), F32),
                        pltpu.VMEM((n_groups, V_ROWS, MOBA_BLOCK), F32),
                        pltpu.VMEM((2, n_groups, KV_BLOCK, MOBA_BLOCK), F32)],
        compiler_params=_params(("arbitrary", "arbitrary", "arbitrary")),
        name="moba",
    )(qa_hi, qa_lo, ka, va_t, ksum, *weights_to_cast)


_DIFF_QGROUPS = 4
_DIFF_HEADS_PER_STEP = 4


def _diff_kernel(*refs, lam_init, n_cast):
    q_ref, k_ref, vt_ref, lam_ref, subln_ref = refs[:5]
    o_ref = refs[5 + n_cast]
    rhs_sc, acc_sc, s_sc = refs[6 + 2 * n_cast:]
    _cast_blocks(refs[5:5 + n_cast], refs[6 + n_cast:6 + 2 * n_cast])
    i = pl.program_id(2)
    blk = KV_BLOCK
    d = HEAD_DIM
    nq = _DIFF_QGROUPS
    hps = _DIFF_HEADS_PER_STEP

    feat = lax.broadcasted_iota(jnp.int32, (d, blk), 0)
    for hd in range(hps):
        for sub in range(2):
            own = (feat < DIFF_QK_DIM) if sub == 0 else (feat >= DIFF_QK_DIM)
            for g in range(nq):
                q_t = q_ref[hd, g]
                rhs_sc[(2 * hd + sub) * nq + g] = jnp.where(own, q_t, jnp.zeros_like(q_t))

    ops = _Operands(
        n_sub=2 * hps, n_qgroups=nq, src_of_sub=[hd for hd in range(hps) for _ in range(2)],
        load_k=lambda n, hd: k_ref[pl.ds(pl.multiple_of(n * blk, blk), blk), hd * d:(hd + 1) * d],
        load_vt=lambda n, hd: vt_ref[hd, n],
        load_rhs=lambda c: rhs_sc[c])
    _flash_columns(i, ops, BLOCKS_PER_ITER, acc_sc, s_sc)

    lq = lam_ref[...]
    lam = (jnp.exp(jnp.sum(lq[0:1, :] * lq[1:2, :], axis=1, keepdims=True))
           - jnp.exp(jnp.sum(lq[2:3, :] * lq[3:4, :], axis=1, keepdims=True)) + lam_init)
    for hd in range(hps):
        for g in range(nq):
            first, second = acc_sc[2 * hd * nq + g], acc_sc[(2 * hd + 1) * nq + g]
            o = _normalised(first) - lam * _normalised(second)
            o = o * lax.rsqrt(jnp.mean(o * o, axis=0, keepdims=True) + RMS_EPS)
            o_ref[g * blk:(g + 1) * blk, hd * d:(hd + 1) * d] = (
                o.T * subln_ref[...] * (1.0 - lam_init)).astype(BF16)


def _diff(qb_t, kb, vb_t, lambda_qk, subln_w, batch, seq, lam_init, weights_to_cast):
    d = HEAD_DIM
    n_blocks = seq // KV_BLOCK
    hps = _DIFF_HEADS_PER_STEP
    n_groups = 2 * hps * _DIFF_QGROUPS
    tq = _DIFF_QGROUPS * KV_BLOCK
    n_tiles = seq // tq
    grid = (batch, HEADS // hps, n_tiles)
    cast_in, cast_out, cast_shapes = _cast_specs(weights_to_cast, grid)
    return pl.pallas_call(
        functools.partial(_diff_kernel, lam_init=lam_init, n_cast=len(weights_to_cast)),
        grid=grid,
        in_specs=[pl.BlockSpec((hps, _DIFF_QGROUPS, d, KV_BLOCK), lambda b, h, i: (h, b * n_tiles + i, 0, 0)),
                  pl.BlockSpec((seq, hps * d), lambda b, h, i: (b, h)),
                  pl.BlockSpec((hps, n_blocks, V_ROWS, KV_BLOCK), lambda b, h, i: (h, b, 0, 0)),
                  pl.BlockSpec(lambda_qk.shape, lambda b, h, i: (0, 0)),
                  pl.BlockSpec((1, d), lambda b, h, i: (0, 0))] + cast_in,
        out_specs=[pl.BlockSpec((tq, hps * d), lambda b, h, i: (b * n_tiles + i, h))] + cast_out,
        out_shape=[jax.ShapeDtypeStruct(kb.shape, BF16)] + cast_shapes,
        scratch_shapes=[pltpu.VMEM((n_groups, d, KV_BLOCK), BF16),
                        pltpu.VMEM((n_groups, V_ROWS, KV_BLOCK), F32),
                        pltpu.VMEM((2, n_groups, KV_BLOCK, KV_BLOCK), F32)],
        compiler_params=_params(("arbitrary", "arbitrary", "arbitrary")),
        name="diffattn",
    )(qb_t, kb, vb_t, lambda_qk, subln_w.reshape(1, d), *weights_to_cast)


_ROW_TM = 512
_FFN_TH = 512


_MERGE_INPUTS = 6


def _merge_kernel(*refs, n_cast):
    ya_ref, yb_ref, ga_ref, gb_ref, wa_ref, wb_ref = refs[:_MERGE_INPUTS]
    o_ref = refs[_MERGE_INPUTS + n_cast]
    _cast_blocks(refs[_MERGE_INPUTS:_MERGE_INPUTS + n_cast], refs[_MERGE_INPUTS + n_cast + 1:])
    a = jnp.dot(ya_ref[...], wa_ref[...], preferred_element_type=F32)
    b = jnp.dot(yb_ref[...], wb_ref[...], preferred_element_type=F32)
    o_ref[...] = (_sigmoid(ga_ref[...].astype(F32)) * a + _sigmoid(gb_ref[...].astype(F32)) * b).astype(BF16)


def _merge(ya, yb, ga, gb, wa, wb, weights_to_cast):
    t = ya.shape[0]
    tm = _ROW_TM
    grid = (t // tm,)
    row = lambda w: pl.BlockSpec((tm, w), lambda i: (i, 0))
    full = lambda a: pl.BlockSpec(a.shape, lambda i: (0, 0))
    cast_in, cast_out, cast_shapes = _cast_specs(weights_to_cast, grid)
    return pl.pallas_call(
        functools.partial(_merge_kernel, n_cast=len(weights_to_cast)),
        grid=grid,
        in_specs=[row(WIDTH), row(WIDTH), row(D_MODEL), row(D_MODEL), full(wa), full(wb)] + cast_in,
        out_specs=[row(D_MODEL)] + cast_out,
        out_shape=[jax.ShapeDtypeStruct((t, D_MODEL), BF16)] + cast_shapes,
        compiler_params=_params(("arbitrary",)),
        name="merge",
    )(ya, yb, ga, gb, wa, wb, *weights_to_cast)


def _layer_norm(y, g, b):
    mu = jnp.mean(y, axis=1, keepdims=True)
    c = y - mu
    var = jnp.mean(c * c, axis=1, keepdims=True)
    return c * lax.rsqrt(var + LN_EPS) * g + b


def _row_halves(ref):
    half = ref.shape[0] // 2
    return slice(0, half), slice(half, 2 * half)


def _out_ln_kernel(m_ref, w_ref, x_ref, g_ref, b_ref, h_ref, hb_ref, *, alpha):
    halves = _row_halves(m_ref)
    ys = [alpha * x_ref[rows, :] + jnp.dot(m_ref[rows, :], w_ref[...], preferred_element_type=F32)
          for rows in halves]
    for rows, y in zip(halves, ys):
        h = _layer_norm(y, g_ref[...], b_ref[...])
        h_ref[rows, :] = h
        hb_ref[rows, :] = h.astype(BF16)


def _out_ln(m, w_out, x2, g, b, alpha):
    t = m.shape[0]
    tm = _ROW_TM
    row = pl.BlockSpec((tm, D_MODEL), lambda i: (i, 0))
    vec = pl.BlockSpec((1, D_MODEL), lambda i: (0, 0))
    return pl.pallas_call(
        functools.partial(_out_ln_kernel, alpha=alpha),
        grid=(t // tm,),
        in_specs=[row, pl.BlockSpec(w_out.shape, lambda i: (0, 0)), row, vec, vec],
        out_specs=(row, row),
        out_shape=(jax.ShapeDtypeStruct((t, D_MODEL), F32), jax.ShapeDtypeStruct((t, D_MODEL), BF16)),
        compiler_params=_params(("parallel",)),
        name="out_ln",
    )(m, w_out, x2, g.reshape(1, D_MODEL), b.reshape(1, D_MODEL))


def _ffn_kernel(hb_ref, wg_ref, wu_ref, wo_ref, h_ref, g_ref, b_ref, o_ref, *, alpha):
    j = pl.program_id(1)

    @pl.when(j == 0)
    def _():
        o_ref[...] = alpha * h_ref[...]

    hb = hb_ref[...]
    gate = jnp.dot(hb, wg_ref[...], preferred_element_type=F32)
    up = jnp.dot(hb, wu_ref[...], preferred_element_type=F32)
    act = (gate * _sigmoid(gate) * up).astype(BF16)
    o_ref[...] += jnp.dot(act, wo_ref[...], preferred_element_type=F32)

    @pl.when(j == pl.num_programs(1) - 1)
    def _():
        o_ref[...] = _layer_norm(o_ref[...], g_ref[...], b_ref[...])


def _ffn_ln(hb, h, w_ffn_in, w_ffn_out, g, b, alpha):
    t = h.shape[0]
    tm, th = _ROW_TM, _FFN_TH
    nj = FFN_HIDDEN // th
    row = pl.BlockSpec((tm, D_MODEL), lambda i, j: (i, 0))
    vec = pl.BlockSpec((1, D_MODEL), lambda i, j: (0, 0))
    return pl.pallas_call(
        functools.partial(_ffn_kernel, alpha=alpha),
        grid=(t // tm, nj),
        in_specs=[row,
                  pl.BlockSpec((D_MODEL, th), lambda i, j: (0, j)),
                  pl.BlockSpec((D_MODEL, th), lambda i, j: (0, nj + j)),
                  pl.BlockSpec((th, D_MODEL), lambda i, j: (j, 0)),
                  row, vec, vec],
        out_specs=row,
        out_shape=jax.ShapeDtypeStruct((t, D_MODEL), F32),
        compiler_params=_params(("parallel", "arbitrary")),
        name="ffn_ln",
    )(hb, w_ffn_in, w_ffn_in, w_ffn_out, h, g.reshape(1, D_MODEL), b.reshape(1, D_MODEL))


def kernel(x, w_in, lambda_qk, diff_subln_w, w_branch_a, w_branch_b, w_out,
           ln1_g, ln1_b, w_ffn_in, w_ffn_out, ln2_g, ln2_b):
    batch, seq, _ = x.shape
    depth = w_in.shape[0]
    alpha = (2.0 * depth) ** 0.25
    tables = _rope_tables(seq)
    h = x.reshape(batch * seq, D_MODEL)
    for l in range(depth):
        lam_init = 0.8 - 0.6 * math.exp(-0.3 * l)
        qa_hi, qa_lo, ka, ksum, va_t, qb_t, kb, vb_t, ga, gb, w_up = _in_proj(
            h, w_in[l], tables, seq, (w_ffn_in[l],))
        ya, wa, wb, wo, w_down = _moba(qa_hi, qa_lo, ka, va_t, ksum, batch, seq,
                                       (w_branch_a[l], w_branch_b[l], w_out[l], w_ffn_out[l]))
        yb, = _diff(qb_t, kb, vb_t, lambda_qk[l], diff_subln_w[l], batch, seq, lam_init, ())
        m, = _merge(ya, yb, ga, gb, wa, wb, ())
        h, hb = _out_ln(m, wo, h, ln1_g[l], ln1_b[l], alpha)
        h = _ffn_ln(hb, h, w_up, w_down, ln2_g[l], ln2_b[l], alpha)
    return h.reshape(batch, seq, D_MODEL)
```

```python
import functools
import math

import jax
import jax.numpy as jnp
import numpy as np
from jax import lax
from jax.experimental import pallas as pl
from jax.experimental.pallas import tpu as pltpu

D_MODEL = 2048
HEADS = 8
HEAD_DIM = 128
WIDTH = HEADS * HEAD_DIM
MOBA_BLOCK = 256
MOBA_TOPK = 3
DIFF_QK_DIM = 64
ROPE_THETA = 10000.0
FFN_HIDDEN = 5632
LN_EPS = 1e-5
RMS_EPS = 1e-5

LANES = 128
SUBLANES = 8
VMEM_LIMIT = 56 * 1024 * 1024
MASKED = -1e30
LOG2E = math.log2(math.e)
KV_BLOCK = 256
BF16_ROW_TILE = 16
V_ROWS = HEAD_DIM + BF16_ROW_TILE
BLOCKS_PER_ITER = 4
SCORE_LEAD = 2

BF16 = jnp.bfloat16
F32 = jnp.float32


def _params(sem):
    return pltpu.CompilerParams(dimension_semantics=sem, vmem_limit_bytes=VMEM_LIMIT)


_N_ROPE_TABLES = 5
_ROPE_STEP = 64


def _rope_tables(seq):
    step = _ROPE_STEP
    assert seq % step == 0
    half_a, half_b = HEAD_DIM // 2, DIFF_QK_DIM // 2
    lane = np.arange(_N_ROPE_TABLES * LANES)
    table, l = lane // LANES, lane % LANES
    freq = np.where(table < 2, l % half_a, 2 * (l % half_b))
    upper = np.where(table < 2, l % HEAD_DIM >= half_a, l % DIFF_QK_DIM >= half_b)
    use_sin = (table == 1) | (table >= 3)
    coef = np.select([table == 1, table == 3, table == 4],
                     [np.where(upper, 1.0, -1.0),
                      np.where(upper, 0.0, -1.0),
                      np.where(upper, 1.0, 0.0)],
                     default=1.0).astype(np.float32)
    inv_lane = (ROPE_THETA ** (-jnp.asarray(freq, F32) * 2.0 / HEAD_DIM))[None, :]
    coarse = jnp.arange(seq // step, dtype=F32)[:, None] * float(step) * inv_lane
    fine = jnp.arange(step, dtype=F32)[:, None] * inv_lane
    cc, sc = jnp.cos(coarse), jnp.sin(coarse)
    a_coarse = jnp.where(use_sin, sc, cc) * coef
    b_coarse = jnp.where(use_sin, cc, -sc) * coef
    return a_coarse, b_coarse, jnp.cos(fine), jnp.sin(fine)


_IN_TM = 512
_IN_COLS = 6 * HEAD_DIM + 2 * 2 * HEAD_DIM


def _sigmoid(t):
    return 0.5 * jnp.tanh(0.5 * t) + 0.5


_IN_PROJ_INPUTS = 13
_IN_PROJ_OUTPUTS = 10


def _in_proj_kernel(*refs, n_cast):
    x_ref, wqa, wka, wva, wqb, wkb, wvb, wga, wgb = refs[:9]
    rope_a_ref, rope_b_ref, rope_cos_ref, rope_sin_ref = refs[9:_IN_PROJ_INPUTS]
    outs = refs[_IN_PROJ_INPUTS + n_cast:_IN_PROJ_INPUTS + n_cast + _IN_PROJ_OUTPUTS]
    qa_hi_ref, qa_lo_ref, ka_ref, ksum_ref, va_ref, qb_ref, kb_ref, vb_ref, ga_ref, gb_ref = outs
    w_sc = refs[-1]
    _cast_blocks(refs[_IN_PROJ_INPUTS:_IN_PROJ_INPUTS + n_cast], refs[_IN_PROJ_INPUTS + n_cast + _IN_PROJ_OUTPUTS:-1])

    def rope_table(j):
        lanes = slice(j * LANES, (j + 1) * LANES)
        fine_cos, fine_sin = rope_cos_ref[:, lanes], rope_sin_ref[:, lanes]
        return jnp.concatenate([rope_a_ref[r:r + 1, lanes] * fine_cos + rope_b_ref[r:r + 1, lanes] * fine_sin
                                for r in range(_IN_TM // _ROPE_STEP)], axis=0)

    @pl.when(pl.program_id(1) == 0)
    def _():
        off = 0
        for w in (wqa, wka, wva, wqb, wkb, wvb, wga, wgb):
            n = w.shape[1]
            w_sc[:, off:off + n] = w[...].astype(BF16)
            off += n

    z = jnp.dot(x_ref[...].astype(BF16), w_sc[...], preferred_element_type=F32)
    d = HEAD_DIM
    blk = KV_BLOCK
    zqa, zka, zva, zqb, zkb, zvb = (z[:, j * d:(j + 1) * d] for j in range(6))
    zga, zgb = z[:, 6 * d:8 * d], z[:, 8 * d:10 * d]

    cos_a, sin_a, cos_b, sin_b_lo, sin_b_hi = (rope_table(j) for j in range(_N_ROPE_TABLES))

    def rope_a(t):
        return t * cos_a + pltpu.roll(t, d // 2, 1) * sin_a

    def rope_b(t):
        return (t * cos_b + pltpu.roll(t, d - DIFF_QK_DIM // 2, 1) * sin_b_lo
                + pltpu.roll(t, DIFF_QK_DIM // 2, 1) * sin_b_hi)

    def store_transposed(ref, t, dtype):
        for j in range(_IN_TM // blk):
            ref[0, j, 0:d] = t[j * blk:(j + 1) * blk, :].T.astype(dtype)
            if ref.shape[2] > d:
                ref[0, j, d:] = jnp.ones((ref.shape[2] - d, blk), dtype)

    qa = rope_a(zqa)
    store_transposed(qa_hi_ref, qa, BF16)
    store_transposed(qa_lo_ref, qa - qa.astype(BF16).astype(F32), BF16)
    ka = rope_a(zka) * (d ** -0.5 * LOG2E)
    ka_ref[...] = ka.astype(BF16)
    for j in range(_IN_TM // MOBA_BLOCK):
        part = ka[j * MOBA_BLOCK:(j + 1) * MOBA_BLOCK].reshape(MOBA_BLOCK // SUBLANES, SUBLANES, d)
        ksum_ref[j * SUBLANES:(j + 1) * SUBLANES, :] = jnp.sum(part, axis=0)
    store_transposed(qb_ref, rope_b(zqb) * (DIFF_QK_DIM ** -0.5 * LOG2E), BF16)
    kb_ref[...] = rope_b(zkb).astype(BF16)
    store_transposed(va_ref, zva, BF16)
    store_transposed(vb_ref, zvb, BF16)
    ga_ref[...] = zga.astype(BF16)
    gb_ref[...] = zgb.astype(BF16)


def _in_proj(x2, w_in, tables, seq, weights_to_cast):
    t = x2.shape[0]
    tm = _IN_TM
    n_seq_tiles = seq // tm
    grid = (HEADS, t // tm)
    d = HEAD_DIM

    def wspec(group):
        return pl.BlockSpec((D_MODEL, d), lambda h, i, g=group: (0, g * HEADS + h))

    def gspec(base):
        return pl.BlockSpec((D_MODEL, 2 * d), lambda h, i, b=base: (0, b + h))

    rope_lanes = _N_ROPE_TABLES * LANES
    coarse_spec = pl.BlockSpec((tm // _ROPE_STEP, rope_lanes), lambda h, i: (i % n_seq_tiles, 0))
    fine_spec = pl.BlockSpec((_ROPE_STEP, rope_lanes), lambda h, i: (0, 0))
    ospec = pl.BlockSpec((tm, d), lambda h, i: (i, h))
    tr_spec = pl.BlockSpec((1, tm // KV_BLOCK, d, KV_BLOCK), lambda h, i: (h, i, 0, 0))
    gate_ospec = pl.BlockSpec((tm, 2 * d), lambda h, i: (i, h))
    ksum_rows = tm // MOBA_BLOCK * SUBLANES
    tr_shape = (HEADS, t // KV_BLOCK, d, KV_BLOCK)
    v_spec = pl.BlockSpec((1, tm // KV_BLOCK, V_ROWS, KV_BLOCK), lambda h, i: (h, i, 0, 0))
    v_shape = (HEADS, t // KV_BLOCK, V_ROWS, KV_BLOCK)
    out_shape = (
        jax.ShapeDtypeStruct(tr_shape, BF16),
        jax.ShapeDtypeStruct(tr_shape, BF16),
        jax.ShapeDtypeStruct((t, WIDTH), BF16),
        jax.ShapeDtypeStruct((t // MOBA_BLOCK * SUBLANES, WIDTH), F32),
        jax.ShapeDtypeStruct(v_shape, BF16),
        jax.ShapeDtypeStruct(tr_shape, BF16),
        jax.ShapeDtypeStruct((t, WIDTH), BF16),
        jax.ShapeDtypeStruct(v_shape, BF16),
        jax.ShapeDtypeStruct((t, D_MODEL), BF16),
        jax.ShapeDtypeStruct((t, D_MODEL), BF16),
    )
    out_specs = (tr_spec, tr_spec, ospec, pl.BlockSpec((ksum_rows, d), lambda h, i: (i, h)),
                 v_spec, tr_spec, ospec, v_spec, gate_ospec, gate_ospec)
    gate_a_base = 6 * WIDTH // (2 * d)
    gate_b_base = gate_a_base + D_MODEL // (2 * d)
    in_specs = [pl.BlockSpec((tm, D_MODEL), lambda h, i: (i, 0))]
    in_specs += [wspec(g) for g in range(6)]
    in_specs += [gspec(gate_a_base), gspec(gate_b_base)]
    in_specs += [coarse_spec, coarse_spec, fine_spec, fine_spec]
    assert len(in_specs) == _IN_PROJ_INPUTS and len(out_specs) == _IN_PROJ_OUTPUTS
    cast_in, cast_out, cast_shapes = _cast_specs(weights_to_cast, grid)
    return pl.pallas_call(
        functools.partial(_in_proj_kernel, n_cast=len(weights_to_cast)),
        grid=grid,
        in_specs=in_specs + cast_in,
        out_specs=list(out_specs) + cast_out,
        out_shape=list(out_shape) + cast_shapes,
        scratch_shapes=[pltpu.VMEM((D_MODEL, _IN_COLS), BF16)],
        compiler_params=_params(("arbitrary", "arbitrary")),
        name="in_proj",
    )(x2, *([w_in] * 8), *tables, *weights_to_cast)


class _Operands:
    def __init__(self, n_sub, n_qgroups, src_of_sub, load_k, load_vt, load_rhs, bias_row=None):
        self.n_sub, self.n_qgroups = n_sub, n_qgroups
        self.src_of_sub, self.load_k, self.load_vt = src_of_sub, load_k, load_vt
        self.load_rhs, self.bias_row = load_rhs, bias_row

    def src(self, c):
        return self.src_of_sub[c // self.n_qgroups]


def _pipeline_stage(n, buf, cur, masks, nxt, ops, acc_ref, s_ref, ms):
    vt_blk = {src: ops.load_vt(n, src) for src in sorted({ops.src(c) for c in cur})}
    k_next = {src: ops.load_k(n + 1, src) for src in sorted({ops.src(c) for c in nxt})}

    def issue_scores(c):
        s_ref[1 - buf, c] = jnp.dot(k_next[ops.src(c)], ops.load_rhs(c), preferred_element_type=F32)

    for c in nxt[:SCORE_LEAD]:
        issue_scores(c)
    rest = nxt[SCORE_LEAD:]
    for j, (c, mask) in enumerate(zip(cur, masks)):
        s = s_ref[buf, c]
        if mask is not None:
            s = jnp.where(mask, s, MASKED)
        top = jnp.max(s, axis=0, keepdims=True)
        bias = ops.bias_row(n, c) if (ops.bias_row is not None and mask is None) else None
        if bias is None:
            m_new = jnp.maximum(ms[c], top)
            shift = m_new
        else:
            seen = bias == 0.0
            m_new = jnp.maximum(ms[c], jnp.where(seen, top, MASKED))
            shift = jnp.where(seen, m_new, -MASKED)
        alpha = jnp.exp2(ms[c] - m_new)
        p = jnp.exp2(s - shift).astype(BF16)
        acc_ref[c] = alpha * acc_ref[c] + jnp.dot(vt_blk[ops.src(c)], p, preferred_element_type=F32)
        ms[c] = m_new
        if j < len(rest):
            issue_scores(rest[j])


def _flash_columns(i, ops, blocks_per_iter, acc_ref, s_ref):
    n_qgroups = ops.n_qgroups
    assert blocks_per_iter % 2 == 0
    assert n_qgroups % blocks_per_iter == 0
    n_groups = ops.n_sub * n_qgroups
    blk = KV_BLOCK
    everyone = list(range(n_groups))
    init = tuple(jnp.full((1, blk), MASKED, F32) for _ in range(n_groups))
    for c in everyone:
        acc_ref[c] = jnp.zeros(acc_ref.shape[1:], F32)
    k_first = {src: ops.load_k(0, src) for src in sorted(set(ops.src_of_sub))}
    for c in everyone:
        s_ref[0, c] = jnp.dot(k_first[ops.src(c)], ops.load_rhs(c), preferred_element_type=F32)

    stage = functools.partial(_pipeline_stage, ops=ops, acc_ref=acc_ref, s_ref=s_ref)

    def body(it, carry):
        ms = list(carry)
        for j in range(blocks_per_iter):
            stage(blocks_per_iter * it + j, j % 2, everyone, [None] * n_groups, everyone, ms=ms)
        return tuple(ms)

    n_past = n_qgroups * i
    ms = list(lax.fori_loop(0, (n_qgroups // blocks_per_iter) * i, body, init))
    row = lax.broadcasted_iota(jnp.int32, (blk, blk), 0)
    col = lax.broadcasted_iota(jnp.int32, (blk, blk), 1)
    causal = row <= col
    for t in range(n_qgroups):
        cur = [c for c in everyone if c % n_qgroups >= t]
        nxt = [c for c in everyone if c % n_qgroups >= t + 1]
        masks = [causal if c % n_qgroups == t else None for c in cur]
        stage(n_past + t, t % 2, cur, masks, nxt, ms=ms)


def _normalised(acc):
    return acc[0:HEAD_DIM] / acc[HEAD_DIM:HEAD_DIM + 1]


def _cast_specs(weights, grid):
    n_steps = math.prod(grid)

    def step_of(*idx):
        lin = idx[0]
        for extent, j in zip(grid[1:], idx[1:]):
            lin = lin * extent + j
        return lin

    in_specs, out_specs, out_shapes = [], [], []
    for w in weights:
        rows, cols = w.shape
        share = 1
        while rows * share % (n_steps * BF16_ROW_TILE):
            share *= 2
            assert share <= n_steps, (rows, n_steps)
        spec = pl.BlockSpec((rows * share // n_steps, cols),
                            lambda *idx, s=share: (step_of(*idx) // s, 0))
        in_specs.append(spec)
        out_specs.append(spec)
        out_shapes.append(jax.ShapeDtypeStruct(w.shape, BF16))
    return in_specs, out_specs, out_shapes


def _cast_blocks(src_refs, dst_refs):
    for src, dst in zip(src_refs, dst_refs):
        dst[...] = src[...].astype(BF16)


_MOBA_QGROUPS = 4
_MOBA_HEADS_PER_STEP = 4


def _split_bf16(a):
    hi = a.astype(BF16)
    lo = (a - hi.astype(F32)).astype(BF16)
    return hi, lo


def _moba_kernel(*refs, n_blocks, n_cast):
    q_hi_ref, q_lo_ref, k_ref, vt_ref, ksum_ref = refs[:5]
    o_ref = refs[5 + n_cast]
    bias_sc, acc_sc, s_sc = refs[6 + 2 * n_cast:]
    _cast_blocks(refs[5:5 + n_cast], refs[6 + n_cast:6 + 2 * n_cast])
    i = pl.program_id(2)
    blk = MOBA_BLOCK
    d = HEAD_DIM
    nq = _MOBA_QGROUPS
    row = lax.broadcasted_iota(jnp.int32, (n_blocks, blk), 0).astype(F32)

    for hd in range(_MOBA_HEADS_PER_STEP):
        ksum = ksum_ref[:, hd * d:(hd + 1) * d]
        kmean = jnp.sum(ksum.reshape(n_blocks, SUBLANES, d), axis=1) * (1.0 / blk)
        m_hi, m_lo = _split_bf16(kmean)
        for g in range(nq):
            q_hi, q_lo = q_hi_ref[hd, g], q_lo_ref[hd, g]
            q_blk = (nq * i + g).astype(F32)
            gate = (jnp.dot(m_hi, q_hi, preferred_element_type=F32)
                    + jnp.dot(m_lo, q_hi, preferred_element_type=F32)
                    + jnp.dot(m_hi, q_lo, preferred_element_type=F32))
            avail = row < q_blk
            keep = row == q_blk
            val = jnp.where(avail, gate, -jnp.inf)
            for _ in range(MOBA_TOPK):
                best = jnp.max(val, axis=0, keepdims=True)
                cand = (val == best) & avail
                first = jnp.min(jnp.where(cand, row, float(n_blocks)), axis=0, keepdims=True)
                pick = row == first
                keep = keep | pick
                avail = avail & jnp.logical_not(pick)
                val = jnp.where(pick, -jnp.inf, val)
            bias = jnp.where(keep, 0.0, MASKED)
            for n in range(n_blocks):
                bias_sc[hd * nq + g, n] = bias[n:n + 1, :]

    ops = _Operands(
        n_sub=_MOBA_HEADS_PER_STEP, n_qgroups=nq, src_of_sub=list(range(_MOBA_HEADS_PER_STEP)),
        load_k=lambda n, hd: k_ref[pl.ds(pl.multiple_of(n * blk, blk), blk), hd * d:(hd + 1) * d],
        load_vt=lambda n, hd: vt_ref[hd, n],
        load_rhs=lambda c: q_hi_ref[c // nq, c % nq],
        bias_row=lambda n, c: bias_sc[c, n])
    _flash_columns(i, ops, BLOCKS_PER_ITER, acc_sc, s_sc)
    for hd in range(_MOBA_HEADS_PER_STEP):
        for g in range(nq):
            o_ref[g * blk:(g + 1) * blk, hd * d:(hd + 1) * d] = _normalised(acc_sc[hd * nq + g]).astype(BF16).T


def _moba(qa_hi, qa_lo, ka, va_t, ksum, batch, seq, weights_to_cast):
    n_blocks = seq // MOBA_BLOCK
    d = HEAD_DIM
    hps = _MOBA_HEADS_PER_STEP
    n_groups = hps * _MOBA_QGROUPS
    tq = _MOBA_QGROUPS * MOBA_BLOCK
    n_tiles = seq // tq
    grid = (batch, HEADS // hps, n_tiles)
    cast_in, cast_out, cast_shapes = _cast_specs(weights_to_cast, grid)
    q_spec = pl.BlockSpec((hps, _MOBA_QGROUPS, d, MOBA_BLOCK), lambda b, h, i: (h, b * n_tiles + i, 0, 0))
    return pl.pallas_call(
        functools.partial(_moba_kernel, n_blocks=n_blocks, n_cast=len(weights_to_cast)),
        grid=grid,
        in_specs=[q_spec, q_spec,
                  pl.BlockSpec((seq, hps * d), lambda b, h, i: (b, h)),
                  pl.BlockSpec((hps, n_blocks, V_ROWS, MOBA_BLOCK), lambda b, h, i: (h, b, 0, 0)),
                  pl.BlockSpec((n_blocks * SUBLANES, hps * d), lambda b, h, i: (b, h))] + cast_in,
        out_specs=[pl.BlockSpec((tq, hps * d), lambda b, h, i: (b * n_tiles + i, h))] + cast_out,
        out_shape=[jax.ShapeDtypeStruct(ka.shape, BF16)] + cast_shapes,
        scratch_shapes=[pltpu.VMEM((n_groups, n_blocks, 1, MOBA_BLOCK), F32),
                        pltpu.VMEM((n_groups, V_ROWS, MOBA_BLOCK), F32),
                        pltpu.VMEM((2, n_groups, KV_BLOCK, MOBA_BLOCK), F32)],
        compiler_params=_params(("arbitrary", "arbitrary", "arbitrary")),
        name="moba",
    )(qa_hi, qa_lo, ka, va_t, ksum, *weights_to_cast)


_DIFF_QGROUPS = 4
_DIFF_HEADS_PER_STEP = 4


def _diff_kernel(*refs, lam_init, n_cast):
    q_ref, k_ref, vt_ref, lam_ref, subln_ref = refs[:5]
    o_ref = refs[5 + n_cast]
    rhs_sc, acc_sc, s_sc = refs[6 + 2 * n_cast:]
    _cast_blocks(refs[5:5 + n_cast], refs[6 + n_cast:6 + 2 * n_cast])
    i = pl.program_id(2)
    blk = KV_BLOCK
    d = HEAD_DIM
    nq = _DIFF_QGROUPS
    hps = _DIFF_HEADS_PER_STEP

    feat = lax.broadcasted_iota(jnp.int32, (d, blk), 0)
    for hd in range(hps):
        for sub in range(2):
            own = (feat < DIFF_QK_DIM) if sub == 0 else (feat >= DIFF_QK_DIM)
            for g in range(nq):
                q_t = q_ref[hd, g]
                rhs_sc[(2 * hd + sub) * nq + g] = jnp.where(own, q_t, jnp.zeros_like(q_t))

    ops = _Operands(
        n_sub=2 * hps, n_qgroups=nq, src_of_sub=[hd for hd in range(hps) for _ in range(2)],
        load_k=lambda n, hd: k_ref[pl.ds(pl.multiple_of(n * blk, blk), blk), hd * d:(hd + 1) * d],
        load_vt=lambda n, hd: vt_ref[hd, n],
        load_rhs=lambda c: rhs_sc[c])
    _flash_columns(i, ops, BLOCKS_PER_ITER, acc_sc, s_sc)

    lq = lam_ref[...]
    lam = (jnp.exp(jnp.sum(lq[0:1, :] * lq[1:2, :], axis=1, keepdims=True))
           - jnp.exp(jnp.sum(lq[2:3, :] * lq[3:4, :], axis=1, keepdims=True)) + lam_init)
    for hd in range(hps):
        for g in range(nq):
            first, second = acc_sc[2 * hd * nq + g], acc_sc[(2 * hd + 1) * nq + g]
            o = _normalised(first) - lam * _normalised(second)
            o = o * lax.rsqrt(jnp.mean(o * o, axis=0, keepdims=True) + RMS_EPS)
            o_ref[g * blk:(g + 1) * blk, hd * d:(hd + 1) * d] = (
                o.T * subln_ref[...] * (1.0 - lam_init)).astype(BF16)


def _diff(qb_t, kb, vb_t, lambda_qk, subln_w, batch, seq, lam_init, weights_to_cast):
    d = HEAD_DIM
    n_blocks = seq // KV_BLOCK
    hps = _DIFF_HEADS_PER_STEP
    n_groups = 2 * hps * _DIFF_QGROUPS
    tq = _DIFF_QGROUPS * KV_BLOCK
    n_tiles = seq // tq
    grid = (batch, HEADS // hps, n_tiles)
    cast_in, cast_out, cast_shapes = _cast_specs(weights_to_cast, grid)
    return pl.pallas_call(
        functools.partial(_diff_kernel, lam_init=lam_init, n_cast=len(weights_to_cast)),
        grid=grid,
        in_specs=[pl.BlockSpec((hps, _DIFF_QGROUPS, d, KV_BLOCK), lambda b, h, i: (h, b * n_tiles + i, 0, 0)),
                  pl.BlockSpec((seq, hps * d), lambda b, h, i: (b, h)),
                  pl.BlockSpec((hps, n_blocks, V_ROWS, KV_BLOCK), lambda b, h, i: (h, b, 0, 0)),
                  pl.BlockSpec(lambda_qk.shape, lambda b, h, i: (0, 0)),
                  pl.BlockSpec((1, d), lambda b, h, i: (0, 0))] + cast_in,
        out_specs=[pl.BlockSpec((tq, hps * d), lambda b, h, i: (b * n_tiles + i, h))] + cast_out,
        out_shape=[jax.ShapeDtypeStruct(kb.shape, BF16)] + cast_shapes,
        scratch_shapes=[pltpu.VMEM((n_groups, d, KV_BLOCK), BF16),
                        pltpu.VMEM((n_groups, V_ROWS, KV_BLOCK), F32),
                        pltpu.VMEM((2, n_groups, KV_BLOCK, KV_BLOCK), F32)],
        compiler_params=_params(("arbitrary", "arbitrary", "arbitrary")),
        name="diffattn",
    )(qb_t, kb, vb_t, lambda_qk, subln_w.reshape(1, d), *weights_to_cast)


_ROW_TM = 512
_FFN_TH = 512


_MERGE_INPUTS = 6


def _merge_kernel(*refs, n_cast):
    ya_ref, yb_ref, ga_ref, gb_ref, wa_ref, wb_ref = refs[:_MERGE_INPUTS]
    o_ref = refs[_MERGE_INPUTS + n_cast]
    _cast_blocks(refs[_MERGE_INPUTS:_MERGE_INPUTS + n_cast], refs[_MERGE_INPUTS + n_cast + 1:])
    a = jnp.dot(ya_ref[...], wa_ref[...], preferred_element_type=F32)
    b = jnp.dot(yb_ref[...], wb_ref[...], preferred_element_type=F32)
    o_ref[...] = (_sigmoid(ga_ref[...].astype(F32)) * a + _sigmoid(gb_ref[...].astype(F32)) * b).astype(BF16)


def _merge(ya, yb, ga, gb, wa, wb, weights_to_cast):
    t = ya.shape[0]
    tm = _ROW_TM
    grid = (t // tm,)
    row = lambda w: pl.BlockSpec((tm, w), lambda i: (i, 0))
    full = lambda a: pl.BlockSpec(a.shape, lambda i: (0, 0))
    cast_in, cast_out, cast_shapes = _cast_specs(weights_to_cast, grid)
    return pl.pallas_call(
        functools.partial(_merge_kernel, n_cast=len(weights_to_cast)),
        grid=grid,
        in_specs=[row(WIDTH), row(WIDTH), row(D_MODEL), row(D_MODEL), full(wa), full(wb)] + cast_in,
        out_specs=[row(D_MODEL)] + cast_out,
        out_shape=[jax.ShapeDtypeStruct((t, D_MODEL), BF16)] + cast_shapes,
        compiler_params=_params(("arbitrary",)),
        name="merge",
    )(ya, yb, ga, gb, wa, wb, *weights_to_cast)


def _layer_norm(y, g, b):
    mu = jnp.mean(y, axis=1, keepdims=True)
    c = y - mu
    var = jnp.mean(c * c, axis=1, keepdims=True)
    return c * lax.rsqrt(var + LN_EPS) * g + b


_LN_ROWS = 32


def _layer_norm_rows(ref, rows, g_ref, b_ref, bf16_ref=None):
    for start in range(rows.start, rows.stop, _LN_ROWS):
        chunk = slice(start, start + _LN_ROWS)
        h = _layer_norm(ref[chunk, :], g_ref[...], b_ref[...])
        ref[chunk, :] = h
        if bf16_ref is not None:
            bf16_ref[chunk, :] = h.astype(BF16)


def _row_halves(ref):
    half = ref.shape[0] // 2
    return slice(0, half), slice(half, 2 * half)


def _out_ln_kernel(m_ref, w_ref, x_ref, g_ref, b_ref, h_ref, hb_ref, *, alpha):
    halves = _row_halves(m_ref)
    for rows in halves:
        h_ref[rows, :] = alpha * x_ref[rows, :] + jnp.dot(m_ref[rows, :], w_ref[...], preferred_element_type=F32)
    for rows in halves:
        _layer_norm_rows(h_ref, rows, g_ref, b_ref, bf16_ref=hb_ref)


def _out_ln(m, w_out, x2, g, b, alpha):
    t = m.shape[0]
    tm = _ROW_TM
    row = pl.BlockSpec((tm, D_MODEL), lambda i: (i, 0))
    vec = pl.BlockSpec((1, D_MODEL), lambda i: (0, 0))
    return pl.pallas_call(
        functools.partial(_out_ln_kernel, alpha=alpha),
        grid=(t // tm,),
        in_specs=[row, pl.BlockSpec(w_out.shape, lambda i: (0, 0)), row, vec, vec],
        out_specs=(row, row),
        out_shape=(jax.ShapeDtypeStruct((t, D_MODEL), F32), jax.ShapeDtypeStruct((t, D_MODEL), BF16)),
        compiler_params=_params(("parallel",)),
        name="out_ln",
    )(m, w_out, x2, g.reshape(1, D_MODEL), b.reshape(1, D_MODEL))


def _ffn_kernel(hb_ref, wg_ref, wu_ref, wo_ref, h_ref, g_ref, b_ref, o_ref, *, alpha):
    j = pl.program_id(1)

    @pl.when(j == 0)
    def _():
        o_ref[...] = alpha * h_ref[...]

    hb = hb_ref[...]
    gate = jnp.dot(hb, wg_ref[...], preferred_element_type=F32)
    up = jnp.dot(hb, wu_ref[...], preferred_element_type=F32)
    act = (gate * _sigmoid(gate) * up).astype(BF16)
    o_ref[...] += jnp.dot(act, wo_ref[...], preferred_element_type=F32)

    @pl.when(j == pl.num_programs(1) - 1)
    def _():
        o_ref[...] = _layer_norm(o_ref[...], g_ref[...], b_ref[...])


def _ffn_ln(hb, h, w_ffn_in, w_ffn_out, g, b, alpha):
    t = h.shape[0]
    tm, th = _ROW_TM, _FFN_TH
    nj = FFN_HIDDEN // th
    row = pl.BlockSpec((tm, D_MODEL), lambda i, j: (i, 0))
    vec = pl.BlockSpec((1, D_MODEL), lambda i, j: (0, 0))
    return pl.pallas_call(
        functools.partial(_ffn_kernel, alpha=alpha),
        grid=(t // tm, nj),
        in_specs=[row,
                  pl.BlockSpec((D_MODEL, th), lambda i, j: (0, j)),
                  pl.BlockSpec((D_MODEL, th), lambda i, j: (0, nj + j)),
                  pl.BlockSpec((th, D_MODEL), lambda i, j: (j, 0)),
                  row, vec, vec],
        out_specs=row,
        out_shape=jax.ShapeDtypeStruct((t, D_MODEL), F32),
        compiler_params=_params(("parallel", "arbitrary")),
        name="ffn_ln",
    )(hb, w_ffn_in, w_ffn_in, w_ffn_out, h, g.reshape(1, D_MODEL), b.reshape(1, D_MODEL))


def kernel(x, w_in, lambda_qk, diff_subln_w, w_branch_a, w_branch_b, w_out,
           ln1_g, ln1_b, w_ffn_in, w_ffn_out, ln2_g, ln2_b):
    batch, seq, _ = x.shape
    depth = w_in.shape[0]
    alpha = (2.0 * depth) ** 0.25
    tables = _rope_tables(seq)
    h = x.reshape(batch * seq, D_MODEL)
    for l in range(depth):
        lam_init = 0.8 - 0.6 * math.exp(-0.3 * l)
        qa_hi, qa_lo, ka, ksum, va_t, qb_t, kb, vb_t, ga, gb, w_up = _in_proj(
            h, w_in[l], tables, seq, (w_ffn_in[l],))
        ya, wa, wb, wo, w_down = _moba(qa_hi, qa_lo, ka, va_t, ksum, batch, seq,
                                       (w_branch_a[l], w_branch_b[l], w_out[l], w_ffn_out[l]))
        yb, = _diff(qb_t, kb, vb_t, lambda_qk[l], diff_subln_w[l], batch, seq, lam_init, ())
        m, = _merge(ya, yb, ga, gb, wa, wb, ())
        h, hb = _out_ln(m, wo, h, ln1_g[l], ln1_b[l], alpha)
        h = _ffn_ln(hb, h, w_up, w_down, ln2_g[l], ln2_b[l], alpha)
    return h.reshape(batch, seq, D_MODEL)
```
